```python
import jax, jax.numpy as jnp
from jax import lax
import numpy as np

D_MODEL = 1024
BATCH = 16
SEQ = 256
DEPTH = 1
DEC_BATCH = 2
DEC_SEQ = 2048
PAST_LEN = 256

GRID_W = 64
D_MIX = D_MODEL
D_CONV = D_MIX // 2
D_RWKV = D_MIX - D_CONV
HEAD = 64
N_RHEADS = D_RWKV // HEAD
N_DIR = 2
CONV_W = 31
SHIFT_W = 3
FFN_CONV_W = 3
R_W = 64
R_A = 64
R_G = 128
D_FF = (5 * D_MODEL) // 2
RW_COLS = 3 * D_RWKV + N_DIR * R_W + N_DIR * R_A + R_G
P_IN = 2 * D_CONV + RW_COLS
RW_SPLITS = [D_RWKV, 2 * D_RWKV, 3 * D_RWKV, 3 * D_RWKV + N_DIR * R_W,
             3 * D_RWKV + N_DIR * (R_W + R_A)]
EPS_RMS = 1e-6
EPS_LN = 1e-5
EPS_GN = 64e-5
L2_EPS = 1e-24
DECAY_OFFSET = 0.5

kernel_name = "hybrid_conformer_rwkv7_prefix_diffusion_step"


def rmsnorm(x, g):
    x32 = x.astype(jnp.float32)
    y = x32 * lax.rsqrt(jnp.mean(x32 * x32, axis=-1, keepdims=True) + EPS_RMS)
    return (y * g.astype(jnp.float32)).astype(x.dtype)


def layernorm(x, g, b):
    x32 = x.astype(jnp.float32)
    mu = jnp.mean(x32, axis=-1, keepdims=True)
    var = jnp.mean(jnp.square(x32 - mu), axis=-1, keepdims=True)
    y = (x32 - mu) * lax.rsqrt(var + EPS_LN)
    return (y * g.astype(jnp.float32) + b.astype(jnp.float32)).astype(x.dtype)


def dwconv1d(x, k):
    w = k.shape[0]
    return lax.conv_general_dilated(
        x, k[:, None, :].astype(x.dtype), window_strides=(1,),
        padding=[(w // 2, w // 2)], dimension_numbers=('NWC', 'WIO', 'NWC'),
        feature_group_count=x.shape[-1])


def dwconv_grid(x, k):
    b, t, c = x.shape
    rows = t // GRID_W
    xg = x.reshape(b, rows, GRID_W, c)
    kh, kw = k.shape[0], k.shape[1]
    y = lax.conv_general_dilated(
        xg, k[:, :, None, :].astype(x.dtype), window_strides=(1, 1),
        padding=[(kh // 2, kh // 2), (kw // 2, kw // 2)],
        dimension_numbers=('NHWC', 'HWIO', 'NHWC'), feature_group_count=c)
    return y.reshape(b, t, c)


def to_heads(z):
    return z.reshape(z.shape[:-1] + (N_RHEADS, HEAD))


def adaln(cond, ada_w, ada_b):
    m = jax.nn.silu(cond) @ ada_w + ada_b
    return jnp.split(m[:, None, :], 6, axis=-1)


def wkv_scan(r, w, k, v, kk, a, s0, reverse):
    def step(S, inp):
        r_t, w_t, k_t, v_t, kk_t, a_t = inp
        sa = jnp.einsum('bhvk,bhk->bhv', S, kk_t)
        S = (S * w_t[:, :, None, :]
             - sa[..., None] * (kk_t * a_t)[:, :, None, :]
             + v_t[..., None] * k_t[:, :, None, :])
        return S, jnp.einsum('bhvk,bhk->bhv', S, r_t)
    xs = (jnp.swapaxes(r, 0, 1), jnp.swapaxes(w, 0, 1), jnp.swapaxes(k, 0, 1),
          jnp.swapaxes(v, 0, 1), jnp.swapaxes(kk, 0, 1), jnp.swapaxes(a, 0, 1))
    s_fin, ys = lax.scan(step, s0, xs, reverse=reverse)
    return jnp.swapaxes(ys, 0, 1), s_fin


def token_mixer(h, s0, lp):
    bsz, t, _ = h.shape
    proj = h @ lp['w_in']
    cv, cg, rw = jnp.split(proj, [D_CONV, 2 * D_CONV], axis=-1)
    u = cv * jax.nn.sigmoid(cg)
    u = dwconv1d(u, lp['conv_dw']) + lp['conv_dw_b']
    u = jax.nn.silu(layernorm(u, lp['conv_ln_g'], lp['conv_ln_b']))
    rw = dwconv1d(rw, lp['shift_k']).astype(jnp.float32)
    r, k, v, wlo, alo, glo = jnp.split(rw, RW_SPLITS, axis=-1)
    wlo = wlo.reshape(bsz, t, N_DIR, R_W)
    alo = alo.reshape(bsz, t, N_DIR, R_A)
    w_raw = lp['w0'] + jnp.einsum('btdr,drc->btdc', jnp.tanh(wlo), lp['w2'])
    decay = jnp.exp(-jnp.exp(-jax.nn.softplus(-w_raw) - DECAY_OFFSET))
    a = jax.nn.sigmoid(lp['a0'] + jnp.einsum('btdr,drc->btdc', alo, lp['a2']))
    g = jax.nn.sigmoid(glo) @ lp['g2']
    kk = to_heads(k * lp['k_k'])
    kk = kk * lax.rsqrt(jnp.maximum(jnp.sum(kk * kk, axis=-1, keepdims=True), L2_EPS))
    k_dir = k[:, :, None, :] * (1.0 + (a - 1.0) * lp['k_a'])
    rh, kh, vh = to_heads(r), to_heads(k), to_heads(v)
    s0 = s0.astype(jnp.float32)
    y_f, s_f = wkv_scan(rh, to_heads(decay[:, :, 0]), to_heads(k_dir[:, :, 0]), vh, kk,
                        to_heads(a[:, :, 0]), s0[:, 0], False)
    y_b, s_b = wkv_scan(rh, to_heads(decay[:, :, 1]), to_heads(k_dir[:, :, 1]), vh, kk,
                        to_heads(a[:, :, 1]), s0[:, 1], True)
    o = y_f + y_b
    mu = jnp.mean(o, axis=-1, keepdims=True)
    var = jnp.mean(jnp.square(o - mu), axis=-1, keepdims=True)
    o = (o - mu) * lax.rsqrt(var + EPS_GN)
    o = o.reshape(bsz, t, D_RWKV) * lp['ln_x_g'] + lp['ln_x_b']
    bonus = jnp.sum(rh * kh * lp['r_k'], axis=-1, keepdims=True) * vh
    o = (o + bonus.reshape(bsz, t, D_RWKV)) * g
    mixed = jnp.concatenate([u, o.astype(h.dtype)], axis=-1) @ lp['w_out']
    return mixed, jnp.stack([s_f, s_b], axis=1)


def channel_mixer(h, lp, grid):
    up = h @ lp['ffn_up']
    if grid:
        up = dwconv_grid(up, lp['ffn_dw'])
    else:
        up = dwconv1d(up, lp['ffn_dw'][FFN_CONV_W // 2])
    up = up + lp['ffn_dw_b']
    val, gate = jnp.split(up, 2, axis=-1)
    return (val * jax.nn.silu(gate)) @ lp['ffn_down']


def block(x, cond, s0, lp, grid):
    sh1, sc1, g1, sh2, sc2, g2 = adaln(cond, lp['ada_w'], lp['ada_b'])
    h = rmsnorm(x, lp['norm1_g']) * (1.0 + sc1) + sh1
    m, s_fin = token_mixer(h, s0, lp)
    x = x + g1 * m
    h = rmsnorm(x, lp['norm2_g']) * (1.0 + sc2) + sh2
    x = x + g2 * channel_mixer(h, lp, grid)
    return x, s_fin


def setup_inputs(seed: int = 0) -> dict:
    key = jax.random.key(seed)
    ks = jax.random.split(key, 40)
    L = DEPTH

    def nrm(k, shape, s):
        return s * jax.random.normal(k, shape, jnp.float32)

    return {
        'x_prompt': nrm(ks[0], (BATCH, SEQ, D_MODEL), 1.0),
        'x_sample': nrm(ks[1], (DEC_BATCH, DEC_SEQ, D_MODEL), 1.0),
        'state_wkv': nrm(ks[2], (DEC_BATCH, L, N_DIR, N_RHEADS, HEAD, HEAD), 0.3),
        'c': nrm(ks[3], (DEC_BATCH, D_MODEL), 1.0),
        'c_ctx': nrm(ks[4], (D_MODEL,), 1.0),
        'ada_w': nrm(ks[5], (L, D_MODEL, 6 * D_MODEL), 0.5 * D_MODEL ** -0.5),
        'ada_b': nrm(ks[6], (L, 6 * D_MODEL), 0.01),
        'norm1_g': 1.0 + nrm(ks[7], (L, D_MODEL), 0.01),
        'w_in': nrm(ks[8], (L, D_MODEL, P_IN), D_MODEL ** -0.5),
        'shift_k': jnp.array([0.25, 0.5, 0.25], jnp.float32)[None, :, None]
                   + nrm(ks[9], (L, SHIFT_W, RW_COLS), 0.1),
        'conv_dw': nrm(ks[10], (L, CONV_W, D_CONV), CONV_W ** -0.5),
        'conv_dw_b': nrm(ks[11], (L, D_CONV), 0.01),
        'conv_ln_g': 1.0 + nrm(ks[12], (L, D_CONV), 0.01),
        'conv_ln_b': nrm(ks[13], (L, D_CONV), 0.01),
        'w0': nrm(ks[14], (L, N_DIR, D_RWKV), 0.5),
        'w2': nrm(ks[15], (L, N_DIR, R_W, D_RWKV), 0.5 * R_W ** -0.5),
        'a0': nrm(ks[16], (L, N_DIR, D_RWKV), 0.5),
        'a2': nrm(ks[17], (L, N_DIR, R_A, D_RWKV), 0.5 * R_A ** -0.5),
        'g2': nrm(ks[18], (L, R_G, D_RWKV), R_G ** -0.5),
        'k_k': 0.85 + nrm(ks[19], (L, D_RWKV), 0.05),
        'k_a': 1.0 + nrm(ks[20], (L, D_RWKV), 0.05),
        'r_k': nrm(ks[21], (L, N_RHEADS, HEAD), 0.1),
        'ln_x_g': 1.0 + nrm(ks[22], (L, D_RWKV), 0.01),
        'ln_x_b': nrm(ks[23], (L, D_RWKV), 0.01),
        'w_out': nrm(ks[24], (L, D_MIX, D_MODEL), D_MIX ** -0.5),
        'norm2_g': 1.0 + nrm(ks[25], (L, D_MODEL), 0.01),
        'ffn_up': nrm(ks[26], (L, D_MODEL, 2 * D_FF), D_MODEL ** -0.5),
        'ffn_dw': nrm(ks[27], (L, FFN_CONV_W, FFN_CONV_W, 2 * D_FF), 1.0 / 3.0),
        'ffn_dw_b': nrm(ks[28], (L, 2 * D_FF), 0.01),
        'ffn_down': nrm(ks[29], (L, D_FF, D_MODEL), D_FF ** -0.5),
        'final_g': 1.0 + nrm(ks[30], (D_MODEL,), 0.01),
    }


def reference(x_prompt, x_sample, state_wkv, c, c_ctx, ada_w, ada_b, norm1_g, w_in,
              shift_k, conv_dw, conv_dw_b, conv_ln_g, conv_ln_b, w0, w2, a0, a2, g2,
              k_k, k_a, r_k, ln_x_g, ln_x_b, w_out, norm2_g, ffn_up, ffn_dw, ffn_dw_b,
              ffn_down, final_g):
    ctx = x_prompt
    lat = x_sample
    zero_state = jnp.zeros((x_prompt.shape[0], N_DIR, N_RHEADS, HEAD, HEAD), jnp.float32)
    ctx_cond = c_ctx[None, :]
    ctx_states = []
    for l in range(DEPTH):
        lp = {
            'ada_w': ada_w[l], 'ada_b': ada_b[l], 'norm1_g': norm1_g[l], 'w_in': w_in[l],
            'shift_k': shift_k[l], 'conv_dw': conv_dw[l], 'conv_dw_b': conv_dw_b[l],
            'conv_ln_g': conv_ln_g[l], 'conv_ln_b': conv_ln_b[l], 'w0': w0[l], 'w2': w2[l],
            'a0': a0[l], 'a2': a2[l], 'g2': g2[l], 'k_k': k_k[l], 'k_a': k_a[l],
            'r_k': r_k[l], 'ln_x_g': ln_x_g[l], 'ln_x_b': ln_x_b[l], 'w_out': w_out[l],
            'norm2_g': norm2_g[l], 'ffn_up': ffn_up[l], 'ffn_dw': ffn_dw[l],
            'ffn_dw_b': ffn_dw_b[l], 'ffn_down': ffn_down[l],
        }
        ctx, s_ctx = block(ctx, ctx_cond, zero_state, lp, False)
        ctx_states.append(s_ctx)
        lat, _ = block(lat, c, state_wkv[:, l], lp, True)
    y_prompt = rmsnorm(ctx, final_g)
    y_sample = rmsnorm(lat, final_g)
    new_state_wkv = jnp.stack(ctx_states, axis=1).astype(state_wkv.dtype)
    return (y_prompt, y_sample, new_state_wkv)
```

```python
import functools
import math

import jax
import jax.numpy as jnp
from jax import lax
from jax.experimental import pallas as pl
from jax.experimental.pallas import tpu as pltpu

F32 = jnp.float32
BF16 = jnp.bfloat16

D_MODEL = 1024
D_CONV = 512
D_RWKV = 512
HEAD = 64
N_RHEADS = 8
CONV_W = 31
R_LO = 64
R_G = 128
D_FF = 2560
RW_COLS = 3 * D_RWKV + 4 * R_LO + R_G
P_IN = 2 * D_CONV + RW_COLS
GRID_W = 64
EPS_RMS = 1e-6
EPS_LN = 1e-5
EPS_GN = 64e-5
L2_EPS = 1e-24
EXP_M05 = math.exp(-0.5)

CHUNK = 64
GROUP = 256
HEADS_PER_GROUP = GROUP // HEAD
N_GROUPS = D_RWKV // GROUP
CONV_HALO = 16
FFN_HALO = GRID_W
VMEM_LIMIT = 56 * 1024 * 1024

NN = (((1,), (0,)), ((), ()))
NT = (((1,), (1,)), ((), ()))


def _dot(a, b, dims=NN):
    return lax.dot_general(a, b, dims, preferred_element_type=F32)


def _split2(x):
    hi = x.astype(BF16)
    lo = (x - hi.astype(F32)).astype(BF16)
    return hi, lo


def _split3(x):
    hi = x.astype(BF16)
    r1 = x - hi.astype(F32)
    mid = r1.astype(BF16)
    lo = (r1 - mid.astype(F32)).astype(BF16)
    return hi, mid, lo


def _mm3(a, b, dims=NN):
    ah, al = a
    bh, bl = b
    return _dot(ah, bh, dims) + (_dot(al, bh, dims) + _dot(ah, bl, dims))


def _mm_exact_rhs(x, w_bf16):
    hi, mid, lo = _split3(x)
    return _dot(hi, w_bf16) + (_dot(mid, w_bf16) + _dot(lo, w_bf16))


def _sigmoid(x):
    return jax.nn.sigmoid(x)


def _head_ones():
    r = lax.broadcasted_iota(jnp.int32, (GROUP, GROUP), 0) // HEAD
    c = lax.broadcasted_iota(jnp.int32, (GROUP, GROUP), 1) // HEAD
    return jnp.where(r == c, 1.0, 0.0).astype(BF16)


def _head_sum(x, ones_bd):
    parts = [_mm_exact_rhs(x[:, g * GROUP:(g + 1) * GROUP], ones_bd) for g in range(N_GROUPS)]
    return jnp.concatenate(parts, axis=1)


def _rmsnorm(x, g):
    ms = jnp.mean(x * x, axis=-1, keepdims=True)
    return x * lax.rsqrt(ms + EPS_RMS) * g


def _adaln_kernel(c_ref, w_ref, b_ref, o_ref):
    c = c_ref[...]
    s = (c * _sigmoid(c)).astype(BF16)
    o_ref[...] = _dot(s, w_ref[...].astype(BF16)) + b_ref[...]


def _adaln(cond8, ada_w, ada_b):
    n_out = ada_w.shape[1]
    bn = 1024
    return pl.pallas_call(
        _adaln_kernel,
        grid=(n_out // bn,),
        in_specs=[
            pl.BlockSpec((8, D_MODEL), lambda j: (0, 0)),
            pl.BlockSpec((D_MODEL, bn), lambda j: (0, j)),
            pl.BlockSpec((1, bn), lambda j: (0, j)),
        ],
        out_specs=pl.BlockSpec((8, bn), lambda j: (0, j)),
        out_shape=jax.ShapeDtypeStruct((8, n_out), F32),
        compiler_params=pltpu.CompilerParams(dimension_semantics=("parallel",)),
        name="adaln",
    )(cond8, ada_w, ada_b)


def _pre_kernel(*refs, tile, n_tiles, has_halo):
    if has_halo:
        (x_ref, xp_ref, xn_ref, mods_ref, n1g_ref, win_ref, sk_ref, cdw_ref, cdb_ref,
         lng_ref, lnb_ref, u_ref, rws_ref, ubuf, rwbuf) = refs
    else:
        (x_ref, mods_ref, n1g_ref, win_ref, sk_ref, cdw_ref, cdb_ref,
         lng_ref, lnb_ref, u_ref, rws_ref, ubuf, rwbuf) = refs
    i = pl.program_id(1)
    sh1 = mods_ref[0, 0:1, :]
    sc1 = mods_ref[0, 1:2, :]

    if has_halo:
        x = jnp.concatenate([xp_ref[0], x_ref[0], xn_ref[0]], axis=0)
    else:
        x = x_ref[0]
    h = _rmsnorm(x, n1g_ref[...]) * (1.0 + sc1) + sh1
    proj = _dot(h.astype(BF16), win_ref[...])
    cv = proj[:, 0:D_CONV]
    cg = proj[:, D_CONV:2 * D_CONV]
    rw = proj[:, 2 * D_CONV:]
    u0 = cv * _sigmoid(cg)

    if has_halo:
        pv = jnp.where(i > 0, 1.0, 0.0).astype(F32)
        nv = jnp.where(i < n_tiles - 1, 1.0, 0.0).astype(F32)
        ubuf[0:CONV_HALO, :] = u0[0:CONV_HALO] * pv
        ubuf[CONV_HALO:CONV_HALO + tile, :] = u0[CONV_HALO:CONV_HALO + tile]
        ubuf[CONV_HALO + tile:, :] = u0[CONV_HALO + tile:] * nv
        rwbuf[0:8, :] = rw[CONV_HALO - 8:CONV_HALO] * pv
        rwbuf[8:8 + tile, :] = rw[CONV_HALO:CONV_HALO + tile]
        rwbuf[8 + tile:, :] = rw[CONV_HALO + tile:CONV_HALO + tile + 8] * nv
    else:
        ubuf[0:CONV_HALO, :] = jnp.zeros((CONV_HALO, D_CONV), F32)
        ubuf[CONV_HALO:CONV_HALO + tile, :] = u0
        ubuf[CONV_HALO + tile:, :] = jnp.zeros((CONV_HALO, D_CONV), F32)
        rwbuf[0:8, :] = jnp.zeros((8, RW_COLS), F32)
        rwbuf[8:8 + tile, :] = rw
        rwbuf[8 + tile:, :] = jnp.zeros((8, RW_COLS), F32)

    acc = jnp.zeros((tile, D_CONV), F32)
    for j in range(CONV_W):
        acc = acc + ubuf[pl.ds(CONV_HALO - CONV_W // 2 + j, tile), :] * cdw_ref[j:j + 1, :]
    acc = acc + cdb_ref[...]
    mu = jnp.mean(acc, axis=-1, keepdims=True)
    dv = acc - mu
    var = jnp.mean(dv * dv, axis=-1, keepdims=True)
    ln = dv * lax.rsqrt(var + EPS_LN) * lng_ref[...] + lnb_ref[...]
    u_ref[0] = ln * _sigmoid(ln)

    rws = rwbuf[pl.ds(7, tile), :] * sk_ref[0:1, :]
    rws = rws + rwbuf[pl.ds(8, tile), :] * sk_ref[1:2, :]
    rws = rws + rwbuf[pl.ds(9, tile), :] * sk_ref[2:3, :]
    rws_ref[0] = rws


def _const_spec(shape):
    nd = len(shape)
    return pl.BlockSpec(shape, lambda b, i: (0,) * nd)


def _pre(x, mods, cond_of_b, p, tile, has_halo):
    bsz, seq, _ = x.shape
    n_tiles = seq // tile
    hb = tile // CONV_HALO
    n_hblk = seq // CONV_HALO
    in_specs = [pl.BlockSpec((1, tile, D_MODEL), lambda b, i: (b, i, 0))]
    args = [x]
    if has_halo:
        in_specs += [
            pl.BlockSpec((1, CONV_HALO, D_MODEL), lambda b, i: (b, jnp.maximum(i * hb - 1, 0), 0)),
            pl.BlockSpec((1, CONV_HALO, D_MODEL),
                         lambda b, i: (b, jnp.minimum((i + 1) * hb, n_hblk - 1), 0)),
        ]
        args += [x, x]
    in_specs += [
        pl.BlockSpec((1, 6, D_MODEL), lambda b, i: (cond_of_b(b), 0, 0)),
        _const_spec((1, D_MODEL)),
        _const_spec((D_MODEL, P_IN)),
        _const_spec((3, RW_COLS)),
        _const_spec((32, D_CONV)),
        _const_spec((1, D_CONV)),
        _const_spec((1, D_CONV)),
        _const_spec((1, D_CONV)),
    ]
    args += [mods, p['norm1_g'], p['w_in'], p['shift_k'], p['conv_dw'], p['conv_dw_b'],
             p['conv_ln_g'], p['conv_ln_b']]
    return pl.pallas_call(
        functools.partial(_pre_kernel, tile=tile, n_tiles=n_tiles, has_halo=has_halo),
        grid=(bsz, n_tiles),
        in_specs=in_specs,
        out_specs=[
            pl.BlockSpec((1, tile, D_CONV), lambda b, i: (b, i, 0)),
            pl.BlockSpec((1, tile, RW_COLS), lambda b, i: (b, i, 0)),
        ],
        out_shape=[
            jax.ShapeDtypeStruct((bsz, seq, D_CONV), F32),
            jax.ShapeDtypeStruct((bsz, seq, RW_COLS), F32),
        ],
        scratch_shapes=[
            pltpu.VMEM((tile + 2 * CONV_HALO, D_CONV), F32),
            pltpu.VMEM((tile + 16, RW_COLS), F32),
        ],
        compiler_params=pltpu.CompilerParams(
            dimension_semantics=("parallel", "parallel"), vmem_limit_bytes=VMEM_LIMIT),
        name="pre_halo" if has_halo else "pre",
    )(*args)


def _block_diag(x, head_masks):
    return jnp.concatenate([x * m for m in head_masks], axis=0)


def _wkv_kernel(*refs, tile, n_tiles, has_s0, has_sout):
    refs = list(refs)
    rf_ref, rb_ref = refs[0], refs[1]
    pos = 2
    s0_ref = None
    if has_s0:
        s0_ref = refs[pos]
        pos += 1
    w0_ref, w2_ref, a0_ref, a2_ref, kk_ref, ka_ref = refs[pos:pos + 6]
    pos += 6
    yf_ref, yb_ref = refs[pos], refs[pos + 1]
    pos += 2
    sout_ref = None
    if has_sout:
        sout_ref = refs[pos]
        pos += 1
    mst, kt_s, q_s, kh_s, bh_s, khp_s, bhp_s, v_s, epc_s = refs[pos:]

    i = pl.program_id(1)
    n_chunks = tile // CHUNK

    @pl.when(i == 0)
    def _init():
        if has_s0:
            mst[...] = s0_ref[0]
        else:
            mst[...] = jnp.zeros(mst.shape, F32)

    ones_bd = _head_ones()
    row = lax.broadcasted_iota(jnp.int32, (tile, tile), 0)
    col = lax.broadcasted_iota(jnp.int32, (tile, tile), 1)
    same_chunk = (row // CHUNK) == (col // CHUNK)
    tri_tot = jnp.where(same_chunk, 1.0, 0.0).astype(BF16)

    for d in range(2):
        src = rf_ref if d == 0 else rb_ref
        r = src[0, :, 0:D_RWKV]
        k = src[0, :, D_RWKV:2 * D_RWKV]
        v = src[0, :, 2 * D_RWKV:3 * D_RWKV]
        wlo = src[0, :, 3 * D_RWKV:3 * D_RWKV + 2 * R_LO]
        alo = src[0, :, 3 * D_RWKV + 2 * R_LO:3 * D_RWKV + 4 * R_LO]
        w_raw = w0_ref[d:d + 1, :] + _dot(jnp.tanh(wlo).astype(BF16), w2_ref[d])
        logw = -_sigmoid(w_raw) * EXP_M05
        a = _sigmoid(a0_ref[d:d + 1, :] + _dot(alo.astype(BF16), a2_ref[d]))
        kk = k * kk_ref[...]
        ss = _head_sum(kk * kk, ones_bd)
        kk = kk * lax.rsqrt(jnp.maximum(ss, L2_EPS))
        kdir = k * (1.0 + (a - 1.0) * ka_ref[...])
        bvec = kk * a
        if d == 0:
            tri = jnp.where(same_chunk & (col <= row), 1.0, 0.0).astype(BF16)
        else:
            tri = jnp.where(same_chunk & (col >= row), 1.0, 0.0).astype(BF16)
        lw3 = _split3(logw)
        cum = _dot(tri, lw3[0]) + (_dot(tri, lw3[1]) + _dot(tri, lw3[2]))
        tot = _dot(tri_tot, lw3[0]) + (_dot(tri_tot, lw3[1]) + _dot(tri_tot, lw3[2]))
        e_ex = jnp.exp(cum - logw)
        e_in = jnp.exp(cum)
        e_neg = jnp.exp(-cum)
        e_rem = jnp.exp(tot - cum)
        e_tot = jnp.exp(tot)
        kt = kk * e_ex
        q = r * e_in
        kh = kdir * e_neg
        bh = bvec * e_neg
        khp = kdir * e_rem
        bhp = bvec * e_rem
        for g in range(N_GROUPS):
            sl = slice(g * GROUP, (g + 1) * GROUP)
            kt_s[d, g] = kt[:, sl]
            q_s[d, g] = q[:, sl]
            kh_s[d, g] = kh[:, sl]
            bh_s[d, g] = bh[:, sl]
            khp_s[d, g] = khp[:, sl]
            bhp_s[d, g] = bhp[:, sl]
            v_s[d, g] = v[:, sl]
            epc_s[d, g] = e_tot[:, sl]

    lane_head = lax.broadcasted_iota(jnp.int32, (1, GROUP), 1) // HEAD
    head_masks = [jnp.where(lane_head == hh, 1.0, 0.0).astype(F32) for hh in range(HEADS_PER_GROUP)]
    t_idx = lax.broadcasted_iota(jnp.int32, (CHUNK, GROUP), 0)
    s_idx = lax.broadcasted_iota(jnp.int32, (CHUNK, GROUP), 1) % CHUNK
    eye_a = jnp.where(s_idx == t_idx, 1.0, 0.0).astype(F32)
    strict = [jnp.where(s_idx < t_idx, 1.0, 0.0).astype(F32),
              jnp.where(s_idx > t_idx, 1.0, 0.0).astype(F32)]
    incl = [jnp.where(s_idx <= t_idx, 1.0, 0.0).astype(F32),
            jnp.where(s_idx >= t_idx, 1.0, 0.0).astype(F32)]
    r256 = lax.broadcasted_iota(jnp.int32, (GROUP, GROUP), 0)
    c256 = lax.broadcasted_iota(jnp.int32, (GROUP, GROUP), 1)
    bd_mask = jnp.where((r256 // HEAD) == (c256 // HEAD), 1.0, 0.0).astype(F32)
    eye256 = jnp.where(r256 == c256, 1.0, 0.0).astype(F32)

    def bd2(x):
        return _split2(_block_diag(x, head_masks))

    def chunk_step(c, carry):
        for d in range(2):
            cc = c if d == 0 else n_chunks - 1 - c
            cs = pl.multiple_of(cc * CHUNK, CHUNK)
            y_ref = yf_ref if d == 0 else yb_ref
            for g in range(N_GROUPS):
                rows = pl.ds(cs, CHUNK)
                kt = kt_s[d, g, rows, :]
                q = q_s[d, g, rows, :]
                kh = kh_s[d, g, rows, :]
                bh = bh_s[d, g, rows, :]
                khp = khp_s[d, g, rows, :]
                bhp = bhp_s[d, g, rows, :]
                v = v_s[d, g, rows, :]
                epc = epc_s[d, g, pl.ds(cs, 1), :]

                lhs2 = _split2(jnp.concatenate([kt, q], axis=0))
                a1 = _mm3(lhs2, bd2(kh), NT)
                a2 = _mm3(lhs2, bd2(bh), NT)
                a_ak = a1[0:CHUNK] * strict[d]
                a_qk = a1[CHUNK:] * incl[d]
                a_ab = a2[0:CHUNK] * strict[d]
                a_qb = a2[CHUNK:] * incl[d]

                n_pow = -a_ab
                t_inv = eye_a + n_pow
                for _ in range(5):
                    n_s = _split2(n_pow)
                    n_pow = _mm3(n_s, bd2(n_pow))
                    t_inv = t_inv + _mm3(_split2(t_inv), bd2(n_pow))
                t_inv_s = _split2(t_inv)

                v_bd = bd2(v)
                av = _mm3(_split2(a_ak), v_bd)
                ktp = _mm3(t_inv_s, bd2(kt))
                vp = _mm3(t_inv_s, bd2(av))
                a_qb_s = _split2(a_qb)
                qp = q - _mm3(a_qb_s, bd2(ktp))
                yp = _mm3(_split2(a_qk), v_bd) - _mm3(a_qb_s, bd2(vp))

                xt = jnp.concatenate([khp, -bhp], axis=0).T
                z = jnp.concatenate(
                    [jnp.concatenate([v, jnp.zeros((CHUNK, GROUP), F32)], axis=1),
                     jnp.concatenate([vp, ktp], axis=1)], axis=0)
                hz = _mm3(_split2(xt), _split2(z))
                hm = hz[:, 0:GROUP] * bd_mask
                gm = eye256 * epc + hz[:, GROUP:] * bd_mask

                m = mst[d, g]
                m_s = _split2(m)
                y = _mm3(_split2(qp), m_s) + yp
                y_ref[0, rows, g * GROUP:(g + 1) * GROUP] = y
                mst[d, g] = _mm3(_split2(gm), m_s) + hm
        return carry

    lax.fori_loop(0, n_chunks, chunk_step, 0)

    if has_sout:
        @pl.when(i == n_tiles - 1)
        def _fin():
            sout_ref[0] = mst[...]


def _wkv(rws, s0_bd, p, tile, has_sout):
    bsz, seq, _ = rws.shape
    n_tiles = seq // tile
    has_s0 = s0_bd is not None
    in_specs = [
        pl.BlockSpec((1, tile, RW_COLS), lambda b, i: (b, i, 0)),
        pl.BlockSpec((1, tile, RW_COLS), lambda b, i: (b, n_tiles - 1 - i, 0)),
    ]
    args = [rws, rws]
    if has_s0:
        in_specs.append(pl.BlockSpec((1, 2, N_GROUPS, GROUP, GROUP), lambda b, i: (b, 0, 0, 0, 0)))
        args.append(s0_bd)
    in_specs += [
        _const_spec((2, D_RWKV)),
        _const_spec((2, 2 * R_LO, D_RWKV)),
        _const_spec((2, D_RWKV)),
        _const_spec((2, 2 * R_LO, D_RWKV)),
        _const_spec((1, D_RWKV)),
        _const_spec((1, D_RWKV)),
    ]
    args += [p['w0'], p['w2p'], p['a0'], p['a2p'], p['k_k'], p['k_a']]
    out_specs = [
        pl.BlockSpec((1, tile, D_RWKV), lambda b, i: (b, i, 0)),
        pl.BlockSpec((1, tile, D_RWKV), lambda b, i: (b, n_tiles - 1 - i, 0)),
    ]
    out_shape = [jax.ShapeDtypeStruct((bsz, seq, D_RWKV), F32)] * 2
    if has_sout:
        out_specs.append(pl.BlockSpec((1, 2, N_GROUPS, GROUP, GROUP), lambda b, i: (b, 0, 0, 0, 0)))
        out_shape.append(jax.ShapeDtypeStruct((bsz, 2, N_GROUPS, GROUP, GROUP), F32))
    tok = (2, N_GROUPS, tile, GROUP)
    return pl.pallas_call(
        functools.partial(_wkv_kernel, tile=tile, n_tiles=n_tiles, has_s0=has_s0,
                          has_sout=has_sout),
        grid=(bsz, n_tiles),
        in_specs=in_specs,
        out_specs=out_specs,
        out_shape=out_shape,
        scratch_shapes=[pltpu.VMEM((2, N_GROUPS, GROUP, GROUP), F32)]
        + [pltpu.VMEM(tok, F32) for _ in range(8)],
        compiler_params=pltpu.CompilerParams(
            dimension_semantics=("arbitrary", "arbitrary"), vmem_limit_bytes=VMEM_LIMIT),
        name="wkv_s0" if has_s0 else "wkv",
    )(*args)


def _mix_kernel(x_ref, yf_ref, yb_ref, rws_ref, u_ref, mods_ref, rk_ref, lxg_ref, lxb_ref,
                g2_ref, wout_ref, x1_ref):
    ones_bd = _head_ones()
    g1 = mods_ref[0, 2:3, :]
    o = yf_ref[0] + yb_ref[0]
    mu = _head_sum(o, ones_bd) * (1.0 / HEAD)
    dv = o - mu
    var = _head_sum(dv * dv, ones_bd) * (1.0 / HEAD)
    on = dv * lax.rsqrt(var + EPS_GN) * lxg_ref[...] + lxb_ref[...]
    r = rws_ref[0, :, 0:D_RWKV]
    k = rws_ref[0, :, D_RWKV:2 * D_RWKV]
    v = rws_ref[0, :, 2 * D_RWKV:3 * D_RWKV]
    glo = rws_ref[0, :, 3 * D_RWKV + 4 * R_LO:]
    bonus = _head_sum(r * k * rk_ref[...], ones_bd) * v
    gate = _dot(_sigmoid(glo).astype(BF16), g2_ref[...])
    o2 = (on + bonus) * gate
    mixed = _dot(u_ref[0].astype(BF16), wout_ref[0:D_CONV, :])
    mixed = mixed + _dot(o2.astype(BF16), wout_ref[D_CONV:, :])
    x1_ref[0] = x_ref[0] + g1 * mixed


def _mix(x, yf, yb, rws, u, mods, cond_of_b, p, tile):
    bsz, seq, _ = x.shape
    n_tiles = seq // tile

    def tok(width):
        return pl.BlockSpec((1, tile, width), lambda b, i: (b, i, 0))

    return pl.pallas_call(
        _mix_kernel,
        grid=(bsz, n_tiles),
        in_specs=[
            tok(D_MODEL), tok(D_RWKV), tok(D_RWKV), tok(RW_COLS), tok(D_CONV),
            pl.BlockSpec((1, 6, D_MODEL), lambda b, i: (cond_of_b(b), 0, 0)),
            _const_spec((1, D_RWKV)),
            _const_spec((1, D_RWKV)),
            _const_spec((1, D_RWKV)),
            _const_spec((R_G, D_RWKV)),
            _const_spec((D_MODEL, D_MODEL)),
        ],
        out_specs=tok(D_MODEL),
        out_shape=jax.ShapeDtypeStruct((bsz, seq, D_MODEL), F32),
        compiler_params=pltpu.CompilerParams(
            dimension_semantics=("parallel", "parallel"), vmem_limit_bytes=VMEM_LIMIT),
        name="mix",
    )(x, yf, yb, rws, u, mods, p['r_k'], p['ln_x_g'], p['ln_x_b'], p['g2'], p['w_out'])


FFN_COLS = 512
FFN_PAD = 8


def _ffn_kernel(*refs, tile, n_tiles, grid_conv):
    if grid_conv:
        (x_ref, xp_ref, xn_ref, mods_ref, n2g_ref, up_ref, dw_ref, dwb_ref, down_ref, fg_ref,
         o_ref, vbuf, gbuf) = refs
    else:
        (x_ref, mods_ref, n2g_ref, up_ref, dw_ref, dwb_ref, down_ref, fg_ref,
         o_ref, vbuf, gbuf) = refs
    i = pl.program_id(1)
    sh2 = mods_ref[0, 3:4, :]
    sc2 = mods_ref[0, 4:5, :]
    g2 = mods_ref[0, 5:6, :]
    x1 = x_ref[0]
    halo = FFN_HALO if grid_conv else 0
    if grid_conv:
        xe = jnp.concatenate([xp_ref[0], x1, xn_ref[0]], axis=0)
    else:
        xe = x1
    rows_e = tile + 2 * halo
    hb = (_rmsnorm(xe, n2g_ref[...]) * (1.0 + sc2) + sh2).astype(BF16)

    if grid_conv:
        e_idx = lax.broadcasted_iota(jnp.int32, (rows_e, 1), 0)
        pv = jnp.where(i > 0, 1.0, 0.0).astype(F32)
        nv = jnp.where(i < n_tiles - 1, 1.0, 0.0).astype(F32)
        ok = jnp.where(e_idx >= halo, 1.0, pv) * jnp.where(e_idx < halo + tile, 1.0, nv)
        colpos = lax.broadcasted_iota(jnp.int32, (tile, 1), 0) % GRID_W
        m_left = jnp.where(colpos >= 1, 1.0, 0.0)
        m_right = jnp.where(colpos <= GRID_W - 2, 1.0, 0.0)
        taps = [(dr, dc) for dr in (-1, 0, 1) for dc in (-1, 0, 1)]
    else:
        taps = [(0, -1), (0, 0), (0, 1)]

    zpad = jnp.zeros((FFN_PAD, FFN_COLS), F32)
    vbuf[0:FFN_PAD, :] = zpad
    vbuf[FFN_PAD + rows_e:, :] = zpad
    gbuf[0:FFN_PAD, :] = zpad
    gbuf[FFN_PAD + rows_e:, :] = zpad

    def conv(buf, col0):
        acc = None
        for ti, (dr, dc) in enumerate(taps):
            win = buf[pl.ds(FFN_PAD + halo + GRID_W * dr + dc, tile), :]
            if grid_conv and dc == -1:
                win = win * m_left
            if grid_conv and dc == 1:
                win = win * m_right
            term = win * dw_ref[ti:ti + 1, col0:col0 + FFN_COLS]
            acc = term if acc is None else acc + term
        return acc + dwb_ref[:, col0:col0 + FFN_COLS]

    acc_out = jnp.zeros((tile, D_MODEL), F32)
    for cc in range(D_FF // FFN_COLS):
        c0 = cc * FFN_COLS
        upv = _dot(hb, up_ref[:, c0:c0 + FFN_COLS])
        upg = _dot(hb, up_ref[:, D_FF + c0:D_FF + c0 + FFN_COLS])
        if grid_conv:
            upv = upv * ok
            upg = upg * ok
        vbuf[FFN_PAD:FFN_PAD + rows_e, :] = upv
        gbuf[FFN_PAD:FFN_PAD + rows_e, :] = upg
        val = conv(vbuf, c0)
        gate = conv(gbuf, D_FF + c0)
        act = val * (gate * _sigmoid(gate))
        acc_out = acc_out + _dot(act.astype(BF16), down_ref[c0:c0 + FFN_COLS, :])

    x2 = x1 + g2 * acc_out
    o_ref[0] = _rmsnorm(x2, fg_ref[...])


def _ffn(x1, mods, cond_of_b, p, tile, grid_conv):
    bsz, seq, _ = x1.shape
    n_tiles = seq // tile
    in_specs = [pl.BlockSpec((1, tile, D_MODEL), lambda b, i: (b, i, 0))]
    args = [x1]
    if grid_conv:
        hb = tile // FFN_HALO
        n_hblk = seq // FFN_HALO
        in_specs += [
            pl.BlockSpec((1, FFN_HALO, D_MODEL), lambda b, i: (b, jnp.maximum(i * hb - 1, 0), 0)),
            pl.BlockSpec((1, FFN_HALO, D_MODEL),
                         lambda b, i: (b, jnp.minimum((i + 1) * hb, n_hblk - 1), 0)),
        ]
        args += [x1, x1]
        dw = p['ffn_dw9']
    else:
        dw = p['ffn_dw3']
    n_taps = dw.shape[0]
    in_specs += [
        pl.BlockSpec((1, 6, D_MODEL), lambda b, i: (cond_of_b(b), 0, 0)),
        _const_spec((1, D_MODEL)),
        _const_spec((D_MODEL, 2 * D_FF)),
        _const_spec((n_taps, 2 * D_FF)),
        _const_spec((1, 2 * D_FF)),
        _const_spec((D_FF, D_MODEL)),
        _const_spec((1, D_MODEL)),
    ]
    args += [mods, p['norm2_g'], p['ffn_up'], dw, p['ffn_dw_b'], p['ffn_down'], p['final_g']]
    halo = FFN_HALO if grid_conv else 0
    buf_rows = tile + 2 * halo + 2 * FFN_PAD
    return pl.pallas_call(
        functools.partial(_ffn_kernel, tile=tile, n_tiles=n_tiles, grid_conv=grid_conv),
        grid=(bsz, n_tiles),
        in_specs=in_specs,
        out_specs=pl.BlockSpec((1, tile, D_MODEL), lambda b, i: (b, i, 0)),
        out_shape=jax.ShapeDtypeStruct((bsz, seq, D_MODEL), F32),
        scratch_shapes=[pltpu.VMEM((buf_rows, FFN_COLS), F32),
                        pltpu.VMEM((buf_rows, FFN_COLS), F32)],
        compiler_params=pltpu.CompilerParams(
            dimension_semantics=("parallel", "parallel"), vmem_limit_bytes=VMEM_LIMIT),
        name="ffn_grid" if grid_conv else "ffn",
    )(*args)


def _pad_lowrank(w):
    z = jnp.zeros_like(w[0])
    return jnp.stack([jnp.concatenate([w[0], z], axis=0),
                      jnp.concatenate([z, w[1]], axis=0)]).astype(BF16)


def _state_to_block_diag(s):
    bsz = s.shape[0]
    m = jnp.swapaxes(s, -1, -2).reshape(bsz, 2, N_GROUPS, HEADS_PER_GROUP, HEAD, HEAD)
    eye = jnp.eye(HEADS_PER_GROUP, dtype=s.dtype)
    bd = jnp.einsum('bdghkv,hj->bdghkjv', m, eye)
    return bd.reshape(bsz, 2, N_GROUPS, GROUP, GROUP)


def _block_diag_to_state(bd):
    bsz = bd.shape[0]
    x = bd.reshape(bsz, 2, N_GROUPS, HEADS_PER_GROUP, HEAD, HEADS_PER_GROUP, HEAD)
    dg = jnp.diagonal(x, axis1=3, axis2=5)
    m = jnp.moveaxis(dg, -1, 3)
    return jnp.swapaxes(m, -1, -2).reshape(bsz, 2, N_RHEADS, HEAD, HEAD)


def kernel(x_prompt, x_sample, state_wkv, c, c_ctx, ada_w, ada_b, norm1_g, w_in, shift_k, conv_dw, conv_dw_b, conv_ln_g, conv_ln_b, w0, w2, a0, a2, g2, k_k, k_a, r_k, ln_x_g, ln_x_b, w_out, norm2_g, ffn_up, ffn_dw, ffn_dw_b, ffn_down, final_g):
    assert ada_w.shape[0] == 1, "single layer"
    dec_b = x_sample.shape[0]
    cond8 = jnp.zeros((8, D_MODEL), F32).at[0].set(c_ctx).at[1:1 + dec_b].set(c)
    mods = _adaln(cond8, ada_w[0], ada_b).reshape(8, 6, D_MODEL)

    p = {
        'norm1_g': norm1_g, 'w_in': w_in[0].astype(BF16), 'shift_k': shift_k[0],
        'conv_dw': jnp.concatenate([conv_dw[0], jnp.zeros((1, D_CONV), F32)], axis=0),
        'conv_dw_b': conv_dw_b, 'conv_ln_g': conv_ln_g, 'conv_ln_b': conv_ln_b,
        'w0': w0[0], 'w2p': _pad_lowrank(w2[0]), 'a0': a0[0], 'a2p': _pad_lowrank(a2[0]),
        'k_k': k_k, 'k_a': k_a, 'r_k': r_k.reshape(1, D_RWKV),
        'ln_x_g': ln_x_g, 'ln_x_b': ln_x_b, 'g2': g2[0].astype(BF16),
        'w_out': w_out[0].astype(BF16), 'norm2_g': norm2_g,
        'ffn_up': ffn_up[0].astype(BF16), 'ffn_dw9': ffn_dw[0].reshape(9, 2 * D_FF),
        'ffn_dw3': ffn_dw[0, 1], 'ffn_dw_b': ffn_dw_b, 'ffn_down': ffn_down[0].astype(BF16),
        'final_g': final_g.reshape(1, D_MODEL),
    }

    def ctx_cond(b):
        return 0

    def lat_cond(b):
        return b + 1

    seq = x_prompt.shape[1]
    u, rws = _pre(x_prompt, mods, ctx_cond, p, tile=seq, has_halo=False)
    yf, yb, sfin = _wkv(rws, None, p, tile=seq, has_sout=True)
    x1 = _mix(x_prompt, yf, yb, rws, u, mods, ctx_cond, p, tile=seq)
    y_prompt = _ffn(x1, mods, ctx_cond, p, tile=seq, grid_conv=False)
    new_state = _block_diag_to_state(sfin)[:, None].astype(state_wkv.dtype)

    s0_bd = _state_to_block_diag(state_wkv[:, 0].astype(F32))
    u, rws = _pre(x_sample, mods, lat_cond, p, tile=512, has_halo=True)
    yf, yb = _wkv(rws, s0_bd, p, tile=256, has_sout=False)
    x1 = _mix(x_sample, yf, yb, rws, u, mods, lat_cond, p, tile=512)
    y_sample = _ffn(x1, mods, lat_cond, p, tile=512, grid_conv=True)

    return (y_prompt, y_sample, new_state)
```

```python
import functools
import math

import jax
import jax.numpy as jnp
from jax import lax
from jax.experimental import pallas as pl
from jax.experimental.pallas import tpu as pltpu

F32 = jnp.float32
BF16 = jnp.bfloat16

D_MODEL = 1024
D_CONV = 512
D_RWKV = 512
HEAD = 64
N_RHEADS = 8
CONV_W = 31
R_LO = 64
R_G = 128
D_FF = 2560
RW_COLS = 3 * D_RWKV + 4 * R_LO + R_G
P_IN = 2 * D_CONV + RW_COLS
GRID_W = 64
EPS_RMS = 1e-6
EPS_LN = 1e-5
EPS_GN = 64e-5
L2_EPS = 1e-24
EXP_M05 = math.exp(-0.5)

CHUNK = 64
GROUP = 256
HEADS_PER_GROUP = GROUP // HEAD
N_GROUPS = D_RWKV // GROUP
CONV_HALO = 16
assert CONV_HALO - CONV_W // 2 == 1
FFN_HALO = GRID_W
VMEM_LIMIT = 56 * 1024 * 1024

P_A = 1
P_INV = 1
NEWTON = True
P_APPLY = 1
P_HZ = 1
P_REC = 1

NN = (((1,), (0,)), ((), ()))
NT = (((1,), (1,)), ((), ()))


def _dot(a, b, dims=NN):
    return lax.dot_general(a, b, dims, preferred_element_type=F32)


def _split2(x):
    hi = x.astype(BF16)
    lo = (x - hi.astype(F32)).astype(BF16)
    return hi, lo


def _split3(x):
    hi = x.astype(BF16)
    r1 = x - hi.astype(F32)
    mid = r1.astype(BF16)
    lo = (r1 - mid.astype(F32)).astype(BF16)
    return hi, mid, lo


def _parts(x, n):
    return _split2(x) if n == 2 else (x.astype(BF16),)


def _mmp(a, b, dims=NN):
    acc = _dot(a[0], b[0], dims)
    if len(a) == 2:
        acc = acc + _dot(a[1], b[0], dims)
    if len(b) == 2:
        acc = acc + _dot(a[0], b[1], dims)
    return acc


def _mm_exact_rhs(x, w_bf16):
    hi, mid, lo = _split3(x)
    return _dot(hi, w_bf16) + (_dot(mid, w_bf16) + _dot(lo, w_bf16))


def _sigmoid(x):
    return jax.nn.sigmoid(x)


def _head_ones():
    r = lax.broadcasted_iota(jnp.int32, (GROUP, GROUP), 0) // HEAD
    c = lax.broadcasted_iota(jnp.int32, (GROUP, GROUP), 1) // HEAD
    return jnp.where(r == c, 1.0, 0.0).astype(BF16)


def _head_sum(x, ones_bd):
    parts = [_mm_exact_rhs(x[:, g * GROUP:(g + 1) * GROUP], ones_bd) for g in range(N_GROUPS)]
    return jnp.concatenate(parts, axis=1)


def _rmsnorm(x, g):
    ms = jnp.mean(x * x, axis=-1, keepdims=True)
    return x * lax.rsqrt(ms + EPS_RMS) * g


def _adaln_kernel(c_ref, w_ref, b_ref, o_ref):
    c = c_ref[...]
    s = (c * _sigmoid(c)).astype(BF16)
    o_ref[...] = _dot(s, w_ref[...].astype(BF16)) + b_ref[...]


def _adaln(cond8, ada_w, ada_b):
    n_out = ada_w.shape[1]
    bn = 1024
    return pl.pallas_call(
        _adaln_kernel,
        grid=(n_out // bn,),
        in_specs=[
            pl.BlockSpec((8, D_MODEL), lambda j: (0, 0)),
            pl.BlockSpec((D_MODEL, bn), lambda j: (0, j)),
            pl.BlockSpec((1, bn), lambda j: (0, j)),
        ],
        out_specs=pl.BlockSpec((8, bn), lambda j: (0, j)),
        out_shape=jax.ShapeDtypeStruct((8, n_out), F32),
        compiler_params=pltpu.CompilerParams(dimension_semantics=("parallel",)),
        name="adaln",
    )(cond8, ada_w, ada_b)


def _pre_kernel(*refs, tile, n_tiles, has_halo):
    if has_halo:
        (x_ref, xp_ref, xn_ref, mods_ref, n1g_ref, win_ref, sk_ref, cdw_ref, cdb_ref,
         lng_ref, lnb_ref, u_ref, rws_ref, ubuf, rwbuf, pbuf) = refs
    else:
        (x_ref, mods_ref, n1g_ref, win_ref, sk_ref, cdw_ref, cdb_ref,
         lng_ref, lnb_ref, u_ref, rws_ref, ubuf, rwbuf, pbuf) = refs
    i = pl.program_id(1)
    sh1 = mods_ref[0, 0:1, :]
    sc1 = mods_ref[0, 1:2, :]

    if has_halo:
        x = jnp.concatenate([xp_ref[0], x_ref[0], xn_ref[0]], axis=0)
    else:
        x = x_ref[0]
    h = _rmsnorm(x, n1g_ref[...]) * (1.0 + sc1) + sh1
    proj = _dot(h.astype(BF16), win_ref[...])
    cv = proj[:, 0:D_CONV]
    cg = proj[:, D_CONV:2 * D_CONV]
    rw = proj[:, 2 * D_CONV:]
    u0 = cv * _sigmoid(cg)

    if has_halo:
        pv = jnp.where(i > 0, 1.0, 0.0).astype(F32)
        nv = jnp.where(i < n_tiles - 1, 1.0, 0.0).astype(F32)
        ubuf[0:CONV_HALO, :] = u0[0:CONV_HALO] * pv
        ubuf[CONV_HALO:CONV_HALO + tile, :] = u0[CONV_HALO:CONV_HALO + tile]
        ubuf[CONV_HALO + tile:, :] = u0[CONV_HALO + tile:] * nv
        rwbuf[0:8, :] = rw[CONV_HALO - 8:CONV_HALO] * pv
        rwbuf[8:8 + tile, :] = rw[CONV_HALO:CONV_HALO + tile]
        rwbuf[8 + tile:, :] = rw[CONV_HALO + tile:CONV_HALO + tile + 8] * nv
    else:
        ubuf[0:CONV_HALO, :] = jnp.zeros((CONV_HALO, D_CONV), F32)
        ubuf[CONV_HALO:CONV_HALO + tile, :] = u0
        ubuf[CONV_HALO + tile:, :] = jnp.zeros((CONV_HALO, D_CONV), F32)
        rwbuf[0:8, :] = jnp.zeros((8, RW_COLS), F32)
        rwbuf[8:8 + tile, :] = rw
        rwbuf[8 + tile:, :] = jnp.zeros((8, RW_COLS), F32)

    ext = tile + 8
    for s in range(8):
        ps = None
        for m in range(4):
            j = 8 * m + s - 1
            if 0 <= j < CONV_W:
                term = ubuf[pl.ds(8 * m, ext), :] * cdw_ref[j:j + 1, :]
                ps = term if ps is None else ps + term
        pbuf[s] = ps
    acc = pbuf[0, 0:tile, :] + cdb_ref[...]
    for s in range(1, 8):
        acc = acc + pbuf[s, pl.ds(s, tile), :]
    mu = jnp.mean(acc, axis=-1, keepdims=True)
    dv = acc - mu
    var = jnp.mean(dv * dv, axis=-1, keepdims=True)
    ln = dv * lax.rsqrt(var + EPS_LN) * lng_ref[...] + lnb_ref[...]
    u_ref[0] = ln * _sigmoid(ln)

    rws = rwbuf[pl.ds(7, tile), :] * sk_ref[0:1, :]
    rws = rws + rwbuf[pl.ds(8, tile), :] * sk_ref[1:2, :]
    rws = rws + rwbuf[pl.ds(9, tile), :] * sk_ref[2:3, :]
    rws_ref[0] = rws


def _const_spec(shape):
    nd = len(shape)
    return pl.BlockSpec(shape, lambda b, i: (0,) * nd)


def _pre(x, mods, cond_of_b, p, tile, has_halo):
    bsz, seq, _ = x.shape
    n_tiles = seq // tile
    hb = tile // CONV_HALO
    n_hblk = seq // CONV_HALO
    in_specs = [pl.BlockSpec((1, tile, D_MODEL), lambda b, i: (b, i, 0))]
    args = [x]
    if has_halo:
        in_specs += [
            pl.BlockSpec((1, CONV_HALO, D_MODEL), lambda b, i: (b, jnp.maximum(i * hb - 1, 0), 0)),
            pl.BlockSpec((1, CONV_HALO, D_MODEL),
                         lambda b, i: (b, jnp.minimum((i + 1) * hb, n_hblk - 1), 0)),
        ]
        args += [x, x]
    in_specs += [
        pl.BlockSpec((1, 6, D_MODEL), lambda b, i: (cond_of_b(b), 0, 0)),
        _const_spec((1, D_MODEL)),
        _const_spec((D_MODEL, P_IN)),
        _const_spec((3, RW_COLS)),
        _const_spec((32, D_CONV)),
        _const_spec((1, D_CONV)),
        _const_spec((1, D_CONV)),
        _const_spec((1, D_CONV)),
    ]
    args += [mods, p['norm1_g'], p['w_in'], p['shift_k'], p['conv_dw'], p['conv_dw_b'],
             p['conv_ln_g'], p['conv_ln_b']]
    return pl.pallas_call(
        functools.partial(_pre_kernel, tile=tile, n_tiles=n_tiles, has_halo=has_halo),
        grid=(bsz, n_tiles),
        in_specs=in_specs,
        out_specs=[
            pl.BlockSpec((1, tile, D_CONV), lambda b, i: (b, i, 0)),
            pl.BlockSpec((1, tile, RW_COLS), lambda b, i: (b, i, 0)),
        ],
        out_shape=[
            jax.ShapeDtypeStruct((bsz, seq, D_CONV), F32),
            jax.ShapeDtypeStruct((bsz, seq, RW_COLS), F32),
        ],
        scratch_shapes=[
            pltpu.VMEM((tile + 2 * CONV_HALO, D_CONV), F32),
            pltpu.VMEM((tile + 16, RW_COLS), F32),
            pltpu.VMEM((8, tile + 8, D_CONV), F32),
        ],
        compiler_params=pltpu.CompilerParams(
            dimension_semantics=("parallel", "parallel"), vmem_limit_bytes=VMEM_LIMIT),
        name="pre_halo" if has_halo else "pre",
    )(*args)


def _wkv_kernel(*refs, tile, n_tiles, has_s0, has_sout):
    refs = list(refs)
    rf_ref, rb_ref = refs[0], refs[1]
    pos = 2
    s0_ref = None
    if has_s0:
        s0_ref = refs[pos]
        pos += 1
    w0_ref, w2_ref, a0_ref, a2_ref, kk_ref, ka_ref = refs[pos:pos + 6]
    pos += 6
    yf_ref, yb_ref = refs[pos], refs[pos + 1]
    pos += 2
    sout_ref = None
    if has_sout:
        sout_ref = refs[pos]
        pos += 1
    mst, kt_s, q_s, kh_s, bh_s, khp_s, bhp_s, v_s, epc_s = refs[pos:]

    i = pl.program_id(1)
    n_chunks = tile // CHUNK

    lane_head = lax.broadcasted_iota(jnp.int32, (1, GROUP), 1) // HEAD
    head_masks_f32 = [jnp.where(lane_head == hh, 1.0, 0.0).astype(F32)
                      for hh in range(HEADS_PER_GROUP)]

    @pl.when(i == 0)
    def _init():
        if has_s0:
            for d in range(2):
                for g in range(N_GROUPS):
                    x0 = s0_ref[0, d, g]
                    st_bd = jnp.concatenate([x0 * m for m in head_masks_f32], axis=0)
                    mst[d, g] = st_bd.T
        else:
            mst[...] = jnp.zeros(mst.shape, F32)

    ones_bd = _head_ones()
    row = lax.broadcasted_iota(jnp.int32, (tile, tile), 0)
    col = lax.broadcasted_iota(jnp.int32, (tile, tile), 1)
    same_chunk = (row // CHUNK) == (col // CHUNK)
    tri_tot = jnp.where(same_chunk, 1.0, 0.0).astype(BF16)

    for d in range(2):
        src = rf_ref if d == 0 else rb_ref
        r = src[0, :, 0:D_RWKV]
        k = src[0, :, D_RWKV:2 * D_RWKV]
        v = src[0, :, 2 * D_RWKV:3 * D_RWKV]
        wlo = src[0, :, 3 * D_RWKV:3 * D_RWKV + 2 * R_LO]
        alo = src[0, :, 3 * D_RWKV + 2 * R_LO:3 * D_RWKV + 4 * R_LO]
        w_raw = w0_ref[d:d + 1, :] + _dot(jnp.tanh(wlo).astype(BF16), w2_ref[d])
        logw = -_sigmoid(w_raw) * EXP_M05
        a = _sigmoid(a0_ref[d:d + 1, :] + _dot(alo.astype(BF16), a2_ref[d]))
        kk = k * kk_ref[...]
        ss = _head_sum(kk * kk, ones_bd)
        kk = kk * lax.rsqrt(jnp.maximum(ss, L2_EPS))
        kdir = k * (1.0 + (a - 1.0) * ka_ref[...])
        bvec = kk * a
        if d == 0:
            tri = jnp.where(same_chunk & (col <= row), 1.0, 0.0).astype(BF16)
        else:
            tri = jnp.where(same_chunk & (col >= row), 1.0, 0.0).astype(BF16)
        lw3 = _split3(logw)
        cum = _dot(tri, lw3[0]) + (_dot(tri, lw3[1]) + _dot(tri, lw3[2]))
        tot = _dot(tri_tot, lw3[0]) + (_dot(tri_tot, lw3[1]) + _dot(tri_tot, lw3[2]))
        e_ex = jnp.exp(cum - logw)
        e_in = jnp.exp(cum)
        e_neg = jnp.exp(-cum)
        e_rem = jnp.exp(tot - cum)
        e_tot = jnp.exp(tot)
        kt = kk * e_ex
        q = r * e_in
        kh = kdir * e_neg
        bh = bvec * e_neg
        khp = kdir * e_rem
        bhp = bvec * e_rem
        for g in range(N_GROUPS):
            sl = slice(g * GROUP, (g + 1) * GROUP)
            kt_s[d, g] = kt[:, sl]
            q_s[d, g] = q[:, sl]
            kh_s[d, g] = kh[:, sl]
            bh_s[d, g] = bh[:, sl]
            khp_s[d, g] = khp[:, sl]
            bhp_s[d, g] = bhp[:, sl]
            v_s[d, g] = v[:, sl]
            epc_s[d, g] = e_tot[:, sl]

    head_masks = [m.astype(BF16) for m in head_masks_f32]
    t_idx = lax.broadcasted_iota(jnp.int32, (CHUNK, GROUP), 0)
    s_idx = lax.broadcasted_iota(jnp.int32, (CHUNK, GROUP), 1) % CHUNK
    eye_a = jnp.where(s_idx == t_idx, 1.0, 0.0).astype(F32)
    strict = [jnp.where(s_idx < t_idx, 1.0, 0.0).astype(F32),
              jnp.where(s_idx > t_idx, 1.0, 0.0).astype(F32)]
    incl = [jnp.where(s_idx <= t_idx, 1.0, 0.0).astype(F32),
            jnp.where(s_idx >= t_idx, 1.0, 0.0).astype(F32)]
    r256 = lax.broadcasted_iota(jnp.int32, (GROUP, GROUP), 0)
    c256 = lax.broadcasted_iota(jnp.int32, (GROUP, GROUP), 1)
    bd_mask = jnp.where((r256 // HEAD) == (c256 // HEAD), 1.0, 0.0).astype(F32)
    eye256 = jnp.where(r256 == c256, 1.0, 0.0).astype(F32)

    def bd(x, n):
        return tuple(jnp.concatenate([part * m for m in head_masks], axis=0)
                     for part in _parts(x, n))

    combos = [(d, g) for d in range(2) for g in range(N_GROUPS)]

    def chunk_step(c, carry):
        rows = [pl.ds(pl.multiple_of(c * CHUNK, CHUNK), CHUNK),
                pl.ds(pl.multiple_of((n_chunks - 1 - c) * CHUNK, CHUNK), CHUNK)]
        first = [pl.ds(pl.multiple_of(c * CHUNK, CHUNK), 1),
                 pl.ds(pl.multiple_of((n_chunks - 1 - c) * CHUNK, CHUNK), 1)]

        def ld(ref):
            return [ref[d, g, rows[d], :] for d, g in combos]

        kt, q, kh, bh, khp, bhp, v = (ld(r_) for r_ in (kt_s, q_s, kh_s, bh_s, khp_s, bhp_s, v_s))
        epc = [epc_s[d, g, first[d], :] for d, g in combos]
        nc = range(len(combos))

        lhs2 = [_parts(jnp.concatenate([kt[j], q[j]], axis=0), P_A) for j in nc]
        a1 = [_mmp(lhs2[j], bd(kh[j], P_A), NT) for j in nc]
        a2 = [_mmp(lhs2[j], bd(bh[j], P_A), NT) for j in nc]
        a_ak = [a1[j][0:CHUNK] * strict[combos[j][0]] for j in nc]
        a_qk = [a1[j][CHUNK:] * incl[combos[j][0]] for j in nc]
        a_ab = [a2[j][0:CHUNK] * strict[combos[j][0]] for j in nc]
        a_qb = [a2[j][CHUNK:] * incl[combos[j][0]] for j in nc]

        n_pow = [-a_ab[j] for j in nc]
        t_inv = [eye_a + n_pow[j] for j in nc]
        for _ in range(5):
            n_pow = [_mmp(_parts(n_pow[j], P_INV), bd(n_pow[j], P_INV)) for j in nc]
            t_inv = [t_inv[j] + _mmp(_parts(t_inv[j], P_INV), bd(n_pow[j], P_INV)) for j in nc]
        if NEWTON:
            res = [eye_a - t_inv[j] - _mmp(_parts(a_ab[j], 2), bd(t_inv[j], 2)) for j in nc]
            t_inv = [t_inv[j] + _mmp(_parts(t_inv[j], 1), bd(res[j], 1)) for j in nc]
        t_inv_p = [_parts(t_inv[j], P_APPLY) for j in nc]

        v_bd = [bd(v[j], P_APPLY) for j in nc]
        av = [_mmp(_parts(a_ak[j], P_APPLY), v_bd[j]) for j in nc]
        ktp = [_mmp(t_inv_p[j], bd(kt[j], P_APPLY)) for j in nc]
        vp = [_mmp(t_inv_p[j], bd(av[j], P_APPLY)) for j in nc]
        a_qb_p = [_parts(a_qb[j], P_APPLY) for j in nc]
        qp = [q[j] - _mmp(a_qb_p[j], bd(ktp[j], P_APPLY)) for j in nc]
        yp = [_mmp(_parts(a_qk[j], P_APPLY), v_bd[j]) - _mmp(a_qb_p[j], bd(vp[j], P_APPLY))
              for j in nc]

        zero = jnp.zeros((CHUNK, GROUP), F32)
        hz = []
        for j in nc:
            xt = jnp.concatenate([khp[j], -bhp[j]], axis=0).T
            z = jnp.concatenate([jnp.concatenate([v[j], zero], axis=1),
                                 jnp.concatenate([vp[j], ktp[j]], axis=1)], axis=0)
            hz.append(_mmp(_parts(xt, P_HZ), _parts(z, P_HZ)))
        for j in nc:
            d, g = combos[j]
            hm = hz[j][:, 0:GROUP] * bd_mask
            gm = eye256 * epc[j] + hz[j][:, GROUP:] * bd_mask
            m_p = _parts(mst[d, g], P_REC)
            y = _mmp(_parts(qp[j], P_REC), m_p) + yp[j]
            y_ref = yf_ref if d == 0 else yb_ref
            y_ref[0, rows[d], g * GROUP:(g + 1) * GROUP] = y
            mst[d, g] = _mmp(_parts(gm, P_REC), m_p) + hm
        return carry

    lax.fori_loop(0, n_chunks, chunk_step, 0)

    if has_sout:
        @pl.when(i == n_tiles - 1)
        def _fin():
            for d in range(2):
                for g in range(N_GROUPS):
                    st_bd = mst[d, g].T
                    packed = st_bd[0:HEAD] * head_masks_f32[0]
                    for hh in range(1, HEADS_PER_GROUP):
                        packed = packed + st_bd[hh * HEAD:(hh + 1) * HEAD] * head_masks_f32[hh]
                    sout_ref[0, d, g] = packed


def _wkv(rws, s0_bd, p, tile, has_sout):
    bsz, seq, _ = rws.shape
    n_tiles = seq // tile
    has_s0 = s0_bd is not None
    in_specs = [
        pl.BlockSpec((1, tile, RW_COLS), lambda b, i: (b, i, 0)),
        pl.BlockSpec((1, tile, RW_COLS), lambda b, i: (b, n_tiles - 1 - i, 0)),
    ]
    args = [rws, rws]
    if has_s0:
        in_specs.append(pl.BlockSpec((1, 2, N_GROUPS, HEAD, GROUP), lambda b, i: (b, 0, 0, 0, 0)))
        args.append(s0_bd)
    in_specs += [
        _const_spec((2, D_RWKV)),
        _const_spec((2, 2 * R_LO, D_RWKV)),
        _const_spec((2, D_RWKV)),
        _const_spec((2, 2 * R_LO, D_RWKV)),
        _const_spec((1, D_RWKV)),
        _const_spec((1, D_RWKV)),
    ]
    args += [p['w0'], p['w2p'], p['a0'], p['a2p'], p['k_k'], p['k_a']]
    out_specs = [
        pl.BlockSpec((1, tile, D_RWKV), lambda b, i: (b, i, 0)),
        pl.BlockSpec((1, tile, D_RWKV), lambda b, i: (b, n_tiles - 1 - i, 0)),
    ]
    out_shape = [jax.ShapeDtypeStruct((bsz, seq, D_RWKV), F32)] * 2
    if has_sout:
        out_specs.append(pl.BlockSpec((1, 2, N_GROUPS, HEAD, GROUP), lambda b, i: (b, 0, 0, 0, 0)))
        out_shape.append(jax.ShapeDtypeStruct((bsz, 2, N_GROUPS, HEAD, GROUP), F32))
    tok = (2, N_GROUPS, tile, GROUP)
    return pl.pallas_call(
        functools.partial(_wkv_kernel, tile=tile, n_tiles=n_tiles, has_s0=has_s0,
                          has_sout=has_sout),
        grid=(bsz, n_tiles),
        in_specs=in_specs,
        out_specs=out_specs,
        out_shape=out_shape,
        scratch_shapes=[pltpu.VMEM((2, N_GROUPS, GROUP, GROUP), F32)]
        + [pltpu.VMEM(tok, F32) for _ in range(8)],
        compiler_params=pltpu.CompilerParams(
            dimension_semantics=("arbitrary", "arbitrary"), vmem_limit_bytes=VMEM_LIMIT),
        name="wkv_s0" if has_s0 else "wkv",
    )(*args)


def _mix_kernel(x_ref, yf_ref, yb_ref, rws_ref, u_ref, mods_ref, rk_ref, lxg_ref, lxb_ref,
                g2_ref, wout_ref, x1_ref):
    ones_bd = _head_ones()
    g1 = mods_ref[0, 2:3, :]
    o = yf_ref[0] + yb_ref[0]
    mu = _head_sum(o, ones_bd) * (1.0 / HEAD)
    dv = o - mu
    var = _head_sum(dv * dv, ones_bd) * (1.0 / HEAD)
    on = dv * lax.rsqrt(var + EPS_GN) * lxg_ref[...] + lxb_ref[...]
    r = rws_ref[0, :, 0:D_RWKV]
    k = rws_ref[0, :, D_RWKV:2 * D_RWKV]
    v = rws_ref[0, :, 2 * D_RWKV:3 * D_RWKV]
    glo = rws_ref[0, :, 3 * D_RWKV + 4 * R_LO:]
    bonus = _head_sum(r * k * rk_ref[...], ones_bd) * v
    gate = _dot(_sigmoid(glo).astype(BF16), g2_ref[...])
    o2 = (on + bonus) * gate
    mixed = _dot(u_ref[0].astype(BF16), wout_ref[0:D_CONV, :])
    mixed = mixed + _dot(o2.astype(BF16), wout_ref[D_CONV:, :])
    x1_ref[0] = x_ref[0] + g1 * mixed


def _mix(x, yf, yb, rws, u, mods, cond_of_b, p, tile):
    bsz, seq, _ = x.shape
    n_tiles = seq // tile

    def tok(width):
        return pl.BlockSpec((1, tile, width), lambda b, i: (b, i, 0))

    return pl.pallas_call(
        _mix_kernel,
        grid=(bsz, n_tiles),
        in_specs=[
            tok(D_MODEL), tok(D_RWKV), tok(D_RWKV), tok(RW_COLS), tok(D_CONV),
            pl.BlockSpec((1, 6, D_MODEL), lambda b, i: (cond_of_b(b), 0, 0)),
            _const_spec((1, D_RWKV)),
            _const_spec((1, D_RWKV)),
            _const_spec((1, D_RWKV)),
            _const_spec((R_G, D_RWKV)),
            _const_spec((D_MODEL, D_MODEL)),
        ],
        out_specs=tok(D_MODEL),
        out_shape=jax.ShapeDtypeStruct((bsz, seq, D_MODEL), F32),
        compiler_params=pltpu.CompilerParams(
            dimension_semantics=("parallel", "parallel"), vmem_limit_bytes=VMEM_LIMIT),
        name="mix",
    )(x, yf, yb, rws, u, mods, p['r_k'], p['ln_x_g'], p['ln_x_b'], p['g2'], p['w_out'])


FFN_COLS = 512
FFN_PAD = 8


def _ffn_kernel(*refs, tile, n_tiles, grid_conv):
    if grid_conv:
        (x_ref, xp_ref, xn_ref, mods_ref, n2g_ref, up_ref, dw_ref, dwb_ref, down_ref, fg_ref,
         o_ref, vbuf, gbuf, vl, vr, gl, gr) = refs
    else:
        (x_ref, mods_ref, n2g_ref, up_ref, dw_ref, dwb_ref, down_ref, fg_ref,
         o_ref, vbuf, gbuf, vl, vr, gl, gr) = refs
    i = pl.program_id(1)
    sh2 = mods_ref[0, 3:4, :]
    sc2 = mods_ref[0, 4:5, :]
    g2 = mods_ref[0, 5:6, :]
    x1 = x_ref[0]
    halo = FFN_HALO if grid_conv else 0
    if grid_conv:
        xe = jnp.concatenate([xp_ref[0], x1, xn_ref[0]], axis=0)
    else:
        xe = x1
    rows_e = tile + 2 * halo
    hb = (_rmsnorm(xe, n2g_ref[...]) * (1.0 + sc2) + sh2).astype(BF16)

    if grid_conv:
        e_idx = lax.broadcasted_iota(jnp.int32, (rows_e, 1), 0)
        pv = jnp.where(i > 0, 1.0, 0.0).astype(F32)
        nv = jnp.where(i < n_tiles - 1, 1.0, 0.0).astype(F32)
        ok = jnp.where(e_idx >= halo, 1.0, pv) * jnp.where(e_idx < halo + tile, 1.0, nv)
        colpos = lax.broadcasted_iota(jnp.int32, (tile, 1), 0) % GRID_W
        m_left = jnp.where(colpos >= 1, 1.0, 0.0)
        m_right = jnp.where(colpos <= GRID_W - 2, 1.0, 0.0)
        taps = [(dr, dc) for dr in (-1, 0, 1) for dc in (-1, 0, 1)]
    else:
        taps = [(0, -1), (0, 0), (0, 1)]

    zpad = jnp.zeros((FFN_PAD, FFN_COLS), F32)
    vbuf[0:FFN_PAD, :] = zpad
    vbuf[FFN_PAD + rows_e:, :] = zpad
    gbuf[0:FFN_PAD, :] = zpad
    gbuf[FFN_PAD + rows_e:, :] = zpad

    def conv(buf, side_l, side_r, col0):
        ext = tile + 2 * FFN_PAD
        sums = {}
        for ti, (dr, dc) in enumerate(taps):
            win = buf[pl.ds(halo + GRID_W * dr, ext), :]
            term = win * dw_ref[ti:ti + 1, col0:col0 + FFN_COLS]
            sums[dc] = term if dc not in sums else sums[dc] + term
        side_l[...] = sums[-1]
        side_r[...] = sums[1]
        left = side_l[pl.ds(FFN_PAD - 1, tile), :]
        right = side_r[pl.ds(FFN_PAD + 1, tile), :]
        if grid_conv:
            left = left * m_left
            right = right * m_right
        centre = sums[0][FFN_PAD:FFN_PAD + tile]
        return centre + left + right + dwb_ref[:, col0:col0 + FFN_COLS]

    acc_out = jnp.zeros((tile, D_MODEL), F32)
    for cc in range(D_FF // FFN_COLS):
        c0 = cc * FFN_COLS
        upv = _dot(hb, up_ref[:, c0:c0 + FFN_COLS])
        upg = _dot(hb, up_ref[:, D_FF + c0:D_FF + c0 + FFN_COLS])
        if grid_conv:
            upv = upv * ok
            upg = upg * ok
        vbuf[FFN_PAD:FFN_PAD + rows_e, :] = upv
        gbuf[FFN_PAD:FFN_PAD + rows_e, :] = upg
        val = conv(vbuf, vl, vr, c0)
        gate = conv(gbuf, gl, gr, D_FF + c0)
        act = val * (gate * _sigmoid(gate))
        acc_out = acc_out + _dot(act.astype(BF16), down_ref[c0:c0 + FFN_COLS, :])

    x2 = x1 + g2 * acc_out
    o_ref[0] = _rmsnorm(x2, fg_ref[...])


def _ffn(x1, mods, cond_of_b, p, tile, grid_conv):
    bsz, seq, _ = x1.shape
    n_tiles = seq // tile
    in_specs = [pl.BlockSpec((1, tile, D_MODEL), lambda b, i: (b, i, 0))]
    args = [x1]
    if grid_conv:
        hb = tile // FFN_HALO
        n_hblk = seq // FFN_HALO
        in_specs += [
            pl.BlockSpec((1, FFN_HALO, D_MODEL), lambda b, i: (b, jnp.maximum(i * hb - 1, 0), 0)),
            pl.BlockSpec((1, FFN_HALO, D_MODEL),
                         lambda b, i: (b, jnp.minimum((i + 1) * hb, n_hblk - 1), 0)),
        ]
        args += [x1, x1]
        dw = p['ffn_dw9']
    else:
        dw = p['ffn_dw3']
    n_taps = dw.shape[0]
    in_specs += [
        pl.BlockSpec((1, 6, D_MODEL), lambda b, i: (cond_of_b(b), 0, 0)),
        _const_spec((1, D_MODEL)),
        _const_spec((D_MODEL, 2 * D_FF)),
        _const_spec((n_taps, 2 * D_FF)),
        _const_spec((1, 2 * D_FF)),
        _const_spec((D_FF, D_MODEL)),
        _const_spec((1, D_MODEL)),
    ]
    args += [mods, p['norm2_g'], p['ffn_up'], dw, p['ffn_dw_b'], p['ffn_down'], p['final_g']]
    halo = FFN_HALO if grid_conv else 0
    buf_rows = tile + 2 * halo + 2 * FFN_PAD
    return pl.pallas_call(
        functools.partial(_ffn_kernel, tile=tile, n_tiles=n_tiles, grid_conv=grid_conv),
        grid=(bsz, n_tiles),
        in_specs=in_specs,
        out_specs=pl.BlockSpec((1, tile, D_MODEL), lambda b, i: (b, i, 0)),
        out_shape=jax.ShapeDtypeStruct((bsz, seq, D_MODEL), F32),
        scratch_shapes=[pltpu.VMEM((buf_rows, FFN_COLS), F32),
                        pltpu.VMEM((buf_rows, FFN_COLS), F32)]
        + [pltpu.VMEM((tile + 2 * FFN_PAD, FFN_COLS), F32) for _ in range(4)],
        compiler_params=pltpu.CompilerParams(
            dimension_semantics=("parallel", "parallel"), vmem_limit_bytes=VMEM_LIMIT),
        name="ffn_grid" if grid_conv else "ffn",
    )(*args)


def _pad_lowrank(w):
    z = jnp.zeros_like(w[0])
    return jnp.stack([jnp.concatenate([w[0], z], axis=0),
                      jnp.concatenate([z, w[1]], axis=0)]).astype(BF16)


def _pack_state(s):
    bsz = s.shape[0]
    x = s.reshape(bsz, 2, N_GROUPS, HEADS_PER_GROUP, HEAD, HEAD)
    return jnp.swapaxes(x, 3, 4).reshape(bsz, 2, N_GROUPS, HEAD, GROUP)


def _unpack_state(x):
    bsz = x.shape[0]
    s = x.reshape(bsz, 2, N_GROUPS, HEAD, HEADS_PER_GROUP, HEAD)
    return jnp.swapaxes(s, 3, 4).reshape(bsz, 2, N_RHEADS, HEAD, HEAD)


def kernel(x_prompt, x_sample, state_wkv, c, c_ctx, ada_w, ada_b, norm1_g, w_in, shift_k, conv_dw, conv_dw_b, conv_ln_g, conv_ln_b, w0, w2, a0, a2, g2, k_k, k_a, r_k, ln_x_g, ln_x_b, w_out, norm2_g, ffn_up, ffn_dw, ffn_dw_b, ffn_down, final_g):
    assert ada_w.shape[0] == 1, "single layer"
    dec_b = x_sample.shape[0]
    cond8 = jnp.zeros((8, D_MODEL), F32).at[0].set(c_ctx).at[1:1 + dec_b].set(c)
    mods = _adaln(cond8, ada_w[0], ada_b).reshape(8, 6, D_MODEL)

    p = {
        'norm1_g': norm1_g, 'w_in': w_in[0].astype(BF16), 'shift_k': shift_k[0],
        'conv_dw': jnp.concatenate([conv_dw[0], jnp.zeros((1, D_CONV), F32)], axis=0),
        'conv_dw_b': conv_dw_b, 'conv_ln_g': conv_ln_g, 'conv_ln_b': conv_ln_b,
        'w0': w0[0], 'w2p': _pad_lowrank(w2[0]), 'a0': a0[0], 'a2p': _pad_lowrank(a2[0]),
        'k_k': k_k, 'k_a': k_a, 'r_k': r_k.reshape(1, D_RWKV),
        'ln_x_g': ln_x_g, 'ln_x_b': ln_x_b, 'g2': g2[0].astype(BF16),
        'w_out': w_out[0].astype(BF16), 'norm2_g': norm2_g,
        'ffn_up': ffn_up[0].astype(BF16), 'ffn_dw9': ffn_dw[0].reshape(9, 2 * D_FF),
        'ffn_dw3': ffn_dw[0, 1], 'ffn_dw_b': ffn_dw_b, 'ffn_down': ffn_down[0].astype(BF16),
        'final_g': final_g.reshape(1, D_MODEL),
    }

    def ctx_cond(b):
        return 0

    def lat_cond(b):
        return b + 1

    seq = x_prompt.shape[1]
    u, rws = _pre(x_prompt, mods, ctx_cond, p, tile=seq, has_halo=False)
    yf, yb, sfin = _wkv(rws, None, p, tile=seq, has_sout=True)
    x1 = _mix(x_prompt, yf, yb, rws, u, mods, ctx_cond, p, tile=seq)
    y_prompt = _ffn(x1, mods, ctx_cond, p, tile=seq, grid_conv=False)
    new_state = _unpack_state(sfin)[:, None].astype(state_wkv.dtype)

    s0_bd = _pack_state(state_wkv[:, 0].astype(F32))
    u, rws = _pre(x_sample, mods, lat_cond, p, tile=512, has_halo=True)
    yf, yb = _wkv(rws, s0_bd, p, tile=256, has_sout=False)
    x1 = _mix(x_sample, yf, yb, rws, u, mods, lat_cond, p, tile=512)
    y_sample = _ffn(x1, mods, lat_cond, p, tile=512, grid_conv=True)

    return (y_prompt, y_sample, new_state)
```

```python
import functools
import math

import jax
import jax.numpy as jnp
from jax import lax
from jax.experimental import pallas as pl
from jax.experimental.pallas import tpu as pltpu

F32 = jnp.float32
BF16 = jnp.bfloat16

D_MODEL = 1024
D_CONV = 512
D_RWKV = 512
HEAD = 64
N_RHEADS = 8
CONV_W = 31
R_LO = 64
R_G = 128
D_FF = 2560
RW_COLS = 3 * D_RWKV + 4 * R_LO + R_G
P_IN = 2 * D_CONV + RW_COLS
GRID_W = 64
EPS_RMS = 1e-6
EPS_LN = 1e-5
EPS_GN = 64e-5
L2_EPS = 1e-24
EXP_M05 = math.exp(-0.5)

CHUNK = 64
GROUP = 256
HEADS_PER_GROUP = GROUP // HEAD
N_GROUPS = D_RWKV // GROUP
CONV_HALO = 16
assert CONV_HALO - CONV_W // 2 == 1
FFN_HALO = GRID_W
VMEM_LIMIT = 56 * 1024 * 1024

WKV_UNROLL = 2

NN = (((1,), (0,)), ((), ()))
NT = (((1,), (1,)), ((), ()))


def _dot(a, b, dims=NN):
    return lax.dot_general(a, b, dims, preferred_element_type=F32)


def _split2(x):
    hi = x.astype(BF16)
    lo = (x - hi.astype(F32)).astype(BF16)
    return hi, lo


def _split3(x):
    hi = x.astype(BF16)
    r1 = x - hi.astype(F32)
    mid = r1.astype(BF16)
    lo = (r1 - mid.astype(F32)).astype(BF16)
    return hi, mid, lo


def _bdot(a, b, dims=NN):
    return _dot(a.astype(BF16), b.astype(BF16), dims)


def _sigmoid(x):
    return jax.nn.sigmoid(x)


def _head_ones():
    r = lax.broadcasted_iota(jnp.int32, (GROUP, GROUP), 0) // HEAD
    c = lax.broadcasted_iota(jnp.int32, (GROUP, GROUP), 1) // HEAD
    return jnp.where(r == c, 1.0, 0.0).astype(BF16)


def _head_sum(x, ones_bd):
    t = x.shape[0]
    pieces = []
    for g in range(N_GROUPS):
        pieces.extend(_split2(x[:, g * GROUP:(g + 1) * GROUP]))
    s = _dot(jnp.concatenate(pieces, axis=0), ones_bd)
    return jnp.concatenate([s[(2 * g) * t:(2 * g + 1) * t] + s[(2 * g + 1) * t:(2 * g + 2) * t]
                            for g in range(N_GROUPS)], axis=1)


def _rmsnorm(x, g):
    ms = jnp.mean(x * x, axis=-1, keepdims=True)
    return x * lax.rsqrt(ms + EPS_RMS) * g


def _adaln_kernel(c_ref, w_ref, b_ref, o_ref):
    c = c_ref[...]
    s = (c * _sigmoid(c)).astype(BF16)
    o_ref[...] = _dot(s, w_ref[...].astype(BF16)) + b_ref[...]


def _adaln(cond8, ada_w, ada_b):
    n_out = ada_w.shape[1]
    bn = 1024
    return pl.pallas_call(
        _adaln_kernel,
        grid=(n_out // bn,),
        in_specs=[
            pl.BlockSpec((8, D_MODEL), lambda j: (0, 0)),
            pl.BlockSpec((D_MODEL, bn), lambda j: (0, j)),
            pl.BlockSpec((1, bn), lambda j: (0, j)),
        ],
        out_specs=pl.BlockSpec((8, bn), lambda j: (0, j)),
        out_shape=jax.ShapeDtypeStruct((8, n_out), F32),
        compiler_params=pltpu.CompilerParams(dimension_semantics=("parallel",)),
        name="adaln",
    )(cond8, ada_w, ada_b)


def _pre_kernel(*refs, tile, n_tiles, has_halo):
    if has_halo:
        (x_ref, xp_ref, xn_ref, mods_ref, n1g_ref, win_ref, sk_ref, cdw_ref, cdb_ref,
         lng_ref, lnb_ref, u_ref, rws_ref, ubuf, rwbuf, pbuf) = refs
    else:
        (x_ref, mods_ref, n1g_ref, win_ref, sk_ref, cdw_ref, cdb_ref,
         lng_ref, lnb_ref, u_ref, rws_ref, ubuf, rwbuf, pbuf) = refs
    i = pl.program_id(1)
    sh1 = mods_ref[0, 0:1, :]
    sc1 = mods_ref[0, 1:2, :]

    if has_halo:
        x = jnp.concatenate([xp_ref[0], x_ref[0], xn_ref[0]], axis=0)
    else:
        x = x_ref[0]
    h = _rmsnorm(x, n1g_ref[...]) * (1.0 + sc1) + sh1
    proj = _dot(h.astype(BF16), win_ref[...])
    cv = proj[:, 0:D_CONV]
    cg = proj[:, D_CONV:2 * D_CONV]
    rw = proj[:, 2 * D_CONV:]
    u0 = cv * _sigmoid(cg)

    if has_halo:
        pv = jnp.where(i > 0, 1.0, 0.0).astype(F32)
        nv = jnp.where(i < n_tiles - 1, 1.0, 0.0).astype(F32)
        ubuf[0:CONV_HALO, :] = u0[0:CONV_HALO] * pv
        ubuf[CONV_HALO:CONV_HALO + tile, :] = u0[CONV_HALO:CONV_HALO + tile]
        ubuf[CONV_HALO + tile:, :] = u0[CONV_HALO + tile:] * nv
        rwbuf[0:8, :] = rw[CONV_HALO - 8:CONV_HALO] * pv
        rwbuf[8:8 + tile, :] = rw[CONV_HALO:CONV_HALO + tile]
        rwbuf[8 + tile:, :] = rw[CONV_HALO + tile:CONV_HALO + tile + 8] * nv
    else:
        ubuf[0:CONV_HALO, :] = jnp.zeros((CONV_HALO, D_CONV), F32)
        ubuf[CONV_HALO:CONV_HALO + tile, :] = u0
        ubuf[CONV_HALO + tile:, :] = jnp.zeros((CONV_HALO, D_CONV), F32)
        rwbuf[0:8, :] = jnp.zeros((8, RW_COLS), F32)
        rwbuf[8:8 + tile, :] = rw
        rwbuf[8 + tile:, :] = jnp.zeros((8, RW_COLS), F32)

    ext = tile + 8
    for s in range(8):
        ps = None
        for m in range(4):
            j = 8 * m + s - 1
            if 0 <= j < CONV_W:
                term = ubuf[pl.ds(8 * m, ext), :] * cdw_ref[j:j + 1, :]
                ps = term if ps is None else ps + term
        pbuf[s] = ps
    acc = pbuf[0, 0:tile, :] + cdb_ref[...]
    for s in range(1, 8):
        acc = acc + pbuf[s, pl.ds(s, tile), :]
    mu = jnp.mean(acc, axis=-1, keepdims=True)
    dv = acc - mu
    var = jnp.mean(dv * dv, axis=-1, keepdims=True)
    ln = dv * lax.rsqrt(var + EPS_LN) * lng_ref[...] + lnb_ref[...]
    u_ref[0] = ln * _sigmoid(ln)

    rws = rwbuf[pl.ds(7, tile), :] * sk_ref[0:1, :]
    rws = rws + rwbuf[pl.ds(8, tile), :] * sk_ref[1:2, :]
    rws = rws + rwbuf[pl.ds(9, tile), :] * sk_ref[2:3, :]
    rws_ref[0] = rws


def _const_spec(shape):
    nd = len(shape)
    return pl.BlockSpec(shape, lambda b, i: (0,) * nd)


def _pre(x, mods, cond_of_b, p, tile, has_halo):
    bsz, seq, _ = x.shape
    n_tiles = seq // tile
    hb = tile // CONV_HALO
    n_hblk = seq // CONV_HALO
    in_specs = [pl.BlockSpec((1, tile, D_MODEL), lambda b, i: (b, i, 0))]
    args = [x]
    if has_halo:
        in_specs += [
            pl.BlockSpec((1, CONV_HALO, D_MODEL), lambda b, i: (b, jnp.maximum(i * hb - 1, 0), 0)),
            pl.BlockSpec((1, CONV_HALO, D_MODEL),
                         lambda b, i: (b, jnp.minimum((i + 1) * hb, n_hblk - 1), 0)),
        ]
        args += [x, x]
    in_specs += [
        pl.BlockSpec((1, 6, D_MODEL), lambda b, i: (cond_of_b(b), 0, 0)),
        _const_spec((1, D_MODEL)),
        _const_spec((D_MODEL, P_IN)),
        _const_spec((3, RW_COLS)),
        _const_spec((32, D_CONV)),
        _const_spec((1, D_CONV)),
        _const_spec((1, D_CONV)),
        _const_spec((1, D_CONV)),
    ]
    args += [mods, p['norm1_g'], p['w_in'], p['shift_k'], p['conv_dw'], p['conv_dw_b'],
             p['conv_ln_g'], p['conv_ln_b']]
    return pl.pallas_call(
        functools.partial(_pre_kernel, tile=tile, n_tiles=n_tiles, has_halo=has_halo),
        grid=(bsz, n_tiles),
        in_specs=in_specs,
        out_specs=[
            pl.BlockSpec((1, tile, D_CONV), lambda b, i: (b, i, 0)),
            pl.BlockSpec((1, tile, RW_COLS), lambda b, i: (b, i, 0)),
        ],
        out_shape=[
            jax.ShapeDtypeStruct((bsz, seq, D_CONV), F32),
            jax.ShapeDtypeStruct((bsz, seq, RW_COLS), F32),
        ],
        scratch_shapes=[
            pltpu.VMEM((tile + 2 * CONV_HALO, D_CONV), F32),
            pltpu.VMEM((tile + 16, RW_COLS), F32),
            pltpu.VMEM((8, tile + 8, D_CONV), F32),
        ],
        compiler_params=pltpu.CompilerParams(
            dimension_semantics=("parallel", "parallel"), vmem_limit_bytes=VMEM_LIMIT),
        name="pre_halo" if has_halo else "pre",
    )(*args)


def _wkv_kernel(*refs, tile, n_tiles, has_s0, has_sout):
    merged = n_tiles == 1
    refs = list(refs)
    if merged:
        rf_ref = rb_ref = refs[0]
        pos = 1
    else:
        rf_ref, rb_ref = refs[0], refs[1]
        pos = 2
    s0_ref = None
    if has_s0:
        s0_ref = refs[pos]
        pos += 1
    w0_ref, w2_ref, a0_ref, a2_ref, kk_ref, ka_ref = refs[pos:pos + 6]
    pos += 6
    if merged:
        yf_ref = yb_ref = refs[pos]
        pos += 1
    else:
        yf_ref, yb_ref = refs[pos], refs[pos + 1]
        pos += 2
    sout_ref = None
    if has_sout:
        sout_ref = refs[pos]
        pos += 1
    (mst,) = refs[pos:]

    i = pl.program_id(1)
    n_chunks = tile // CHUNK

    lane_head = lax.broadcasted_iota(jnp.int32, (1, GROUP), 1) // HEAD
    head_masks = [jnp.where(lane_head == hh, 1.0, 0.0).astype(BF16)
                  for hh in range(HEADS_PER_GROUP)]
    t_idx = lax.broadcasted_iota(jnp.int32, (CHUNK, GROUP), 0)
    s_idx = lax.broadcasted_iota(jnp.int32, (CHUNK, GROUP), 1) % CHUNK
    eye_a = jnp.where(s_idx == t_idx, 1.0, 0.0).astype(F32)
    eye_b = eye_a.astype(BF16)

    def bd(x):
        xb = x.astype(BF16)
        return jnp.concatenate([xb * m for m in head_masks], axis=0)

    def block_transpose_exact(x):
        acc = None
        for part in _split3(x):
            term = _dot(eye_b, bd(part), NT)
            acc = term if acc is None else acc + term
        return acc

    @pl.when(i == 0)
    def _init():
        if has_s0:
            for d in range(2):
                for g in range(N_GROUPS):
                    mst[d, g] = block_transpose_exact(s0_ref[0, d, g])
        else:
            mst[...] = jnp.zeros(mst.shape, F32)
    if merged:
        yf_ref[...] = jnp.zeros(yf_ref.shape, F32)

    ones_bd = _head_ones()
    slab = WKV_UNROLL * CHUNK
    row = lax.broadcasted_iota(jnp.int32, (slab, slab), 0)
    col = lax.broadcasted_iota(jnp.int32, (slab, slab), 1)
    same_chunk = (row // CHUNK) == (col // CHUNK)
    tri = [jnp.where(same_chunk & (col <= row), 1.0, 0.0).astype(BF16),
           jnp.where(same_chunk & (col >= row), 1.0, 0.0).astype(BF16)]
    strict = [jnp.where(s_idx < t_idx, 1.0, 0.0).astype(F32),
              jnp.where(s_idx > t_idx, 1.0, 0.0).astype(F32)]
    incl = [jnp.where(s_idx <= t_idx, 1.0, 0.0).astype(F32),
            jnp.where(s_idx >= t_idx, 1.0, 0.0).astype(F32)]

    chains = [(d, g, u) for u in range(WKV_UNROLL) for d in range(2) for g in range(N_GROUPS)]

    def loop_step(it, carry):
        slab0 = [pl.multiple_of(it * slab, slab), pl.multiple_of(tile - (it + 1) * slab, slab)]

        def chunk_lo(d, u):
            return (u if d == 0 else WKV_UNROLL - 1 - u) * CHUNK

        rows = {(d, u): pl.ds(pl.multiple_of(slab0[d] + chunk_lo(d, u), CHUNK), CHUNK)
                for d in range(2) for u in range(WKV_UNROLL)}

        kt, q, kh, bh, khp, bhp, v, epc = ({} for _ in range(8))
        for d in range(2):
            src = rf_ref if d == 0 else rb_ref
            srows = pl.ds(slab0[d], slab)
            r_t = src[0, srows, 0:D_RWKV]
            k_t = src[0, srows, D_RWKV:2 * D_RWKV]
            v_t = src[0, srows, 2 * D_RWKV:3 * D_RWKV]
            wlo = src[0, srows, 3 * D_RWKV:3 * D_RWKV + 2 * R_LO]
            alo = src[0, srows, 3 * D_RWKV + 2 * R_LO:3 * D_RWKV + 4 * R_LO]
            kk = k_t * kk_ref[...]
            ss = _head_sum(kk * kk, ones_bd)
            kk = kk * lax.rsqrt(jnp.maximum(ss, L2_EPS))
            w_raw = w0_ref[d:d + 1, :] + _dot(jnp.tanh(wlo).astype(BF16), w2_ref[d])
            logw = -_sigmoid(w_raw) * EXP_M05
            a = _sigmoid(a0_ref[d:d + 1, :] + _dot(alo.astype(BF16), a2_ref[d]))
            kdir = k_t * (1.0 + (a - 1.0) * ka_ref[...])
            bvec = kk * a
            lw_hi, lw_lo = _split2(logw)
            cum = _dot(tri[d], lw_hi) + _dot(tri[d], lw_lo)
            edge = CHUNK - 1 if d == 0 else 0
            tot = jnp.concatenate(
                [jnp.broadcast_to(cum[c * CHUNK + edge:c * CHUNK + edge + 1, :], (CHUNK, D_RWKV))
                 for c in range(WKV_UNROLL)], axis=0)
            e_in = jnp.exp(cum)
            e_neg = jnp.exp(-cum)
            e_tot = jnp.exp(tot)
            e_rem = e_tot * e_neg
            full = {'kt': kk * jnp.exp(cum - logw), 'q': r_t * e_in, 'kh': kdir * e_neg,
                    'bh': bvec * e_neg, 'khp': kdir * e_rem, 'bhp': bvec * e_rem, 'v': v_t,
                    'epc': e_tot}
            for g in range(N_GROUPS):
                lanes = slice(g * GROUP, (g + 1) * GROUP)
                for u in range(WKV_UNROLL):
                    lo = chunk_lo(d, u)
                    ch = (d, g, u)
                    for dst, name in ((kt, 'kt'), (q, 'q'), (kh, 'kh'), (bh, 'bh'), (khp, 'khp'),
                                      (bhp, 'bhp'), (v, 'v')):
                        dst[ch] = full[name][lo:lo + CHUNK, lanes]
                    epc[ch] = full['epc'][lo:lo + 1, lanes]

        def stack(*xs):
            return jnp.concatenate([x.astype(BF16) for x in xs], axis=0)

        lhs2 = {ch: stack(kt[ch], q[ch]) for ch in chains}
        a1 = {ch: _dot(lhs2[ch], bd(kh[ch]), NT) for ch in chains}
        a2 = {ch: _dot(lhs2[ch], bd(bh[ch]), NT) for ch in chains}
        a_ak = {ch: a1[ch][0:CHUNK] * strict[ch[0]] for ch in chains}
        a_qk = {ch: a1[ch][CHUNK:] * incl[ch[0]] for ch in chains}
        a_ab = {ch: a2[ch][0:CHUNK] * strict[ch[0]] for ch in chains}
        a_qb = {ch: a2[ch][CHUNK:] * incl[ch[0]] for ch in chains}

        kb_t = {ch: _dot(eye_b, jnp.concatenate([bd(khp[ch]), bd(bhp[ch])], axis=0), NT)
                for ch in chains}
        khp_t = {ch: kb_t[ch][:, 0:GROUP] for ch in chains}
        bhp_t = {ch: kb_t[ch][:, GROUP:] for ch in chains}

        t_inv = {ch: eye_a - a_ab[ch] for ch in chains}
        n_pow = {ch: _bdot(a_ab[ch], bd(a_ab[ch])) for ch in chains}
        for _ in range(4):
            both = {ch: _dot(stack(t_inv[ch], n_pow[ch]), bd(n_pow[ch])) for ch in chains}
            t_inv = {ch: t_inv[ch] + both[ch][0:CHUNK] for ch in chains}
            n_pow = {ch: both[ch][CHUNK:] for ch in chains}
        t_inv = {ch: t_inv[ch] + _bdot(t_inv[ch], bd(n_pow[ch])) for ch in chains}

        on_v = {ch: _dot(stack(a_ak[ch], a_qk[ch], khp_t[ch]), bd(v[ch])) for ch in chains}
        av = {ch: on_v[ch][0:CHUNK] for ch in chains}
        lhs_qb = {ch: _dot(stack(a_qb[ch], bhp_t[ch]), bd(t_inv[ch])).astype(BF16) for ch in chains}
        on_ktp = {ch: _dot(lhs_qb[ch], bd(kt[ch])) for ch in chains}
        on_vp = {ch: _dot(lhs_qb[ch], bd(av[ch])) for ch in chains}
        qp = {ch: q[ch] - on_ktp[ch][0:CHUNK] for ch in chains}
        yp = {ch: on_v[ch][CHUNK:2 * CHUNK] - on_vp[ch][0:CHUNK] for ch in chains}
        gp = {ch: eye_a * epc[ch] - on_ktp[ch][CHUNK:] for ch in chains}
        hp = {ch: on_v[ch][2 * CHUNK:] - on_vp[ch][CHUNK:] for ch in chains}

        for d in range(2):
            y_ref = yf_ref if d == 0 else yb_ref
            for g in range(N_GROUPS):
                m = mst[d, g]
                for u in range(WKV_UNROLL):
                    ch = (d, g, u)
                    on_m = _dot(stack(qp[ch], gp[ch]), bd(m))
                    y = on_m[0:CHUNK] + yp[ch]
                    lanes = slice(g * GROUP, (g + 1) * GROUP)
                    if merged:
                        y_ref[0, rows[(d, u)], lanes] += y
                    else:
                        y_ref[0, rows[(d, u)], lanes] = y
                    m = on_m[CHUNK:] + hp[ch]
                mst[d, g] = m
        return carry

    lax.fori_loop(0, n_chunks // WKV_UNROLL, loop_step, 0)

    if has_sout:
        @pl.when(i == n_tiles - 1)
        def _fin():
            for d in range(2):
                for g in range(N_GROUPS):
                    sout_ref[0, d, g] = block_transpose_exact(mst[d, g])


def _wkv(rws, s0_bd, p, tile, has_sout):
    bsz, seq, _ = rws.shape
    n_tiles = seq // tile
    has_s0 = s0_bd is not None
    merged = n_tiles == 1
    in_specs = [pl.BlockSpec((1, tile, RW_COLS), lambda b, i: (b, i, 0))]
    args = [rws]
    if not merged:
        in_specs.append(pl.BlockSpec((1, tile, RW_COLS), lambda b, i: (b, n_tiles - 1 - i, 0)))
        args.append(rws)
    if has_s0:
        in_specs.append(pl.BlockSpec((1, 2, N_GROUPS, HEAD, GROUP), lambda b, i: (b, 0, 0, 0, 0)))
        args.append(s0_bd)
    in_specs += [
        _const_spec((2, D_RWKV)),
        _const_spec((2, 2 * R_LO, D_RWKV)),
        _const_spec((2, D_RWKV)),
        _const_spec((2, 2 * R_LO, D_RWKV)),
        _const_spec((1, D_RWKV)),
        _const_spec((1, D_RWKV)),
    ]
    args += [p['w0'], p['w2p'], p['a0'], p['a2p'], p['k_k'], p['k_a']]
    out_specs = [pl.BlockSpec((1, tile, D_RWKV), lambda b, i: (b, i, 0))]
    out_shape = [jax.ShapeDtypeStruct((bsz, seq, D_RWKV), F32)]
    if not merged:
        out_specs.append(pl.BlockSpec((1, tile, D_RWKV), lambda b, i: (b, n_tiles - 1 - i, 0)))
        out_shape.append(jax.ShapeDtypeStruct((bsz, seq, D_RWKV), F32))
    if has_sout:
        out_specs.append(pl.BlockSpec((1, 2, N_GROUPS, HEAD, GROUP), lambda b, i: (b, 0, 0, 0, 0)))
        out_shape.append(jax.ShapeDtypeStruct((bsz, 2, N_GROUPS, HEAD, GROUP), F32))
    return pl.pallas_call(
        functools.partial(_wkv_kernel, tile=tile, n_tiles=n_tiles, has_s0=has_s0,
                          has_sout=has_sout),
        grid=(bsz, n_tiles),
        in_specs=in_specs,
        out_specs=out_specs,
        out_shape=out_shape,
        scratch_shapes=[pltpu.VMEM((2, N_GROUPS, HEAD, GROUP), F32)],
        compiler_params=pltpu.CompilerParams(
            dimension_semantics=("arbitrary", "arbitrary"), vmem_limit_bytes=VMEM_LIMIT),
        name="wkv_s0" if has_s0 else "wkv",
    )(*args)


def _mix_kernel(*refs, n_y):
    x_ref = refs[0]
    y_refs = refs[1:1 + n_y]
    (rws_ref, u_ref, mods_ref, rk_ref, lxg_ref, lxb_ref, g2_ref, wout_ref,
     x1_ref) = refs[1 + n_y:]
    ones_bd = _head_ones()
    g1 = mods_ref[0, 2:3, :]
    o = y_refs[0][0]
    for y_ref in y_refs[1:]:
        o = o + y_ref[0]
    mu = _head_sum(o, ones_bd) * (1.0 / HEAD)
    dv = o - mu
    var = _head_sum(dv * dv, ones_bd) * (1.0 / HEAD)
    on = dv * lax.rsqrt(var + EPS_GN) * lxg_ref[...] + lxb_ref[...]
    r = rws_ref[0, :, 0:D_RWKV]
    k = rws_ref[0, :, D_RWKV:2 * D_RWKV]
    v = rws_ref[0, :, 2 * D_RWKV:3 * D_RWKV]
    glo = rws_ref[0, :, 3 * D_RWKV + 4 * R_LO:]
    bonus = _head_sum(r * k * rk_ref[...], ones_bd) * v
    gate = _dot(_sigmoid(glo).astype(BF16), g2_ref[...])
    o2 = (on + bonus) * gate
    mixed = _dot(u_ref[0].astype(BF16), wout_ref[0:D_CONV, :])
    mixed = mixed + _dot(o2.astype(BF16), wout_ref[D_CONV:, :])
    x1_ref[0] = x_ref[0] + g1 * mixed


def _mix(x, ys, rws, u, mods, cond_of_b, p, tile):
    bsz, seq, _ = x.shape
    n_tiles = seq // tile

    def tok(width):
        return pl.BlockSpec((1, tile, width), lambda b, i: (b, i, 0))

    return pl.pallas_call(
        functools.partial(_mix_kernel, n_y=len(ys)),
        grid=(bsz, n_tiles),
        in_specs=[tok(D_MODEL)] + [tok(D_RWKV) for _ in ys] + [
            tok(RW_COLS), tok(D_CONV),
            pl.BlockSpec((1, 6, D_MODEL), lambda b, i: (cond_of_b(b), 0, 0)),
            _const_spec((1, D_RWKV)),
            _const_spec((1, D_RWKV)),
            _const_spec((1, D_RWKV)),
            _const_spec((R_G, D_RWKV)),
            _const_spec((D_MODEL, D_MODEL)),
        ],
        out_specs=tok(D_MODEL),
        out_shape=jax.ShapeDtypeStruct((bsz, seq, D_MODEL), F32),
        compiler_params=pltpu.CompilerParams(
            dimension_semantics=("parallel", "parallel"), vmem_limit_bytes=VMEM_LIMIT),
        name="mix",
    )(x, *ys, rws, u, mods, p['r_k'], p['ln_x_g'], p['ln_x_b'], p['g2'], p['w_out'])


FFN_COLS = 512
FFN_PAD = 8


def _ffn_kernel(*refs, tile, n_tiles, grid_conv):
    if grid_conv:
        (x_ref, xp_ref, xn_ref, mods_ref, n2g_ref, up_ref, dw_ref, dwb_ref, down_ref, fg_ref,
         o_ref, vbuf, gbuf, vl, vr, gl, gr, act_s) = refs
    else:
        (x_ref, mods_ref, n2g_ref, up_ref, dw_ref, dwb_ref, down_ref, fg_ref,
         o_ref, vbuf, gbuf, vl, vr, gl, gr, act_s) = refs
    i = pl.program_id(1)
    sh2 = mods_ref[0, 3:4, :]
    sc2 = mods_ref[0, 4:5, :]
    g2 = mods_ref[0, 5:6, :]
    x1 = x_ref[0]
    halo = FFN_HALO if grid_conv else 0
    if grid_conv:
        xe = jnp.concatenate([xp_ref[0], x1, xn_ref[0]], axis=0)
    else:
        xe = x1
    rows_e = tile + 2 * halo
    h2 = _rmsnorm(xe, n2g_ref[...]) * (1.0 + sc2) + sh2

    if grid_conv:
        e_idx = lax.broadcasted_iota(jnp.int32, (rows_e, 1), 0)
        pv = jnp.where(i > 0, 1.0, 0.0).astype(F32)
        nv = jnp.where(i < n_tiles - 1, 1.0, 0.0).astype(F32)
        h2 = h2 * (jnp.where(e_idx >= halo, 1.0, pv) * jnp.where(e_idx < halo + tile, 1.0, nv))
        colpos = lax.broadcasted_iota(jnp.int32, (tile, 1), 0) % GRID_W
        m_left = jnp.where(colpos >= 1, 1.0, 0.0)
        m_right = jnp.where(colpos <= GRID_W - 2, 1.0, 0.0)
        taps = [(dr, dc) for dr in (-1, 0, 1) for dc in (-1, 0, 1)]
    else:
        taps = [(0, -1), (0, 0), (0, 1)]

    zpad = jnp.zeros((FFN_PAD, FFN_COLS), F32)
    vbuf[0:FFN_PAD, :] = zpad
    vbuf[FFN_PAD + rows_e:, :] = zpad
    gbuf[0:FFN_PAD, :] = zpad
    gbuf[FFN_PAD + rows_e:, :] = zpad

    def conv(buf, side_l, side_r, col0):
        ext = tile + 2 * FFN_PAD
        sums = {}
        for ti, (dr, dc) in enumerate(taps):
            win = buf[pl.ds(halo + GRID_W * dr, ext), :]
            term = win * dw_ref[ti:ti + 1, col0:col0 + FFN_COLS]
            sums[dc] = term if dc not in sums else sums[dc] + term
        side_l[...] = sums[-1]
        side_r[...] = sums[1]
        left = side_l[pl.ds(FFN_PAD - 1, tile), :]
        right = side_r[pl.ds(FFN_PAD + 1, tile), :]
        if grid_conv:
            left = left * m_left
            right = right * m_right
        centre = sums[0][FFN_PAD:FFN_PAD + tile]
        return centre + left + right + dwb_ref[:, col0:col0 + FFN_COLS]

    hb = h2.astype(BF16)
    for cc in range(D_FF // FFN_COLS):
        c0 = cc * FFN_COLS
        vbuf[FFN_PAD:FFN_PAD + rows_e, :] = _dot(hb, up_ref[:, c0:c0 + FFN_COLS])
        gbuf[FFN_PAD:FFN_PAD + rows_e, :] = _dot(hb, up_ref[:, D_FF + c0:D_FF + c0 + FFN_COLS])
        val = conv(vbuf, vl, vr, c0)
        gate = conv(gbuf, gl, gr, D_FF + c0)
        act_s[:, c0:c0 + FFN_COLS] = (val * (gate * _sigmoid(gate))).astype(BF16)

    x2 = x1 + g2 * _dot(act_s[...], down_ref[...])
    o_ref[0] = _rmsnorm(x2, fg_ref[...])


def _ffn(x1, mods, cond_of_b, p, tile, grid_conv):
    bsz, seq, _ = x1.shape
    n_tiles = seq // tile
    in_specs = [pl.BlockSpec((1, tile, D_MODEL), lambda b, i: (b, i, 0))]
    args = [x1]
    if grid_conv:
        hb = tile // FFN_HALO
        n_hblk = seq // FFN_HALO
        in_specs += [
            pl.BlockSpec((1, FFN_HALO, D_MODEL), lambda b, i: (b, jnp.maximum(i * hb - 1, 0), 0)),
            pl.BlockSpec((1, FFN_HALO, D_MODEL),
                         lambda b, i: (b, jnp.minimum((i + 1) * hb, n_hblk - 1), 0)),
        ]
        args += [x1, x1]
        dw = p['ffn_dw9']
    else:
        dw = p['ffn_dw3']
    n_taps = dw.shape[0]
    in_specs += [
        pl.BlockSpec((1, 6, D_MODEL), lambda b, i: (cond_of_b(b), 0, 0)),
        _const_spec((1, D_MODEL)),
        _const_spec((D_MODEL, 2 * D_FF)),
        _const_spec((n_taps, 2 * D_FF)),
        _const_spec((1, 2 * D_FF)),
        _const_spec((D_FF, D_MODEL)),
        _const_spec((1, D_MODEL)),
    ]
    args += [mods, p['norm2_g'], p['ffn_up'], dw, p['ffn_dw_b'], p['ffn_down'], p['final_g']]
    halo = FFN_HALO if grid_conv else 0
    buf_rows = tile + 2 * halo + 2 * FFN_PAD
    return pl.pallas_call(
        functools.partial(_ffn_kernel, tile=tile, n_tiles=n_tiles, grid_conv=grid_conv),
        grid=(bsz, n_tiles),
        in_specs=in_specs,
        out_specs=pl.BlockSpec((1, tile, D_MODEL), lambda b, i: (b, i, 0)),
        out_shape=jax.ShapeDtypeStruct((bsz, seq, D_MODEL), F32),
        scratch_shapes=[pltpu.VMEM((buf_rows, FFN_COLS), F32),
                        pltpu.VMEM((buf_rows, FFN_COLS), F32)]
        + [pltpu.VMEM((tile + 2 * FFN_PAD, FFN_COLS), F32) for _ in range(4)]
        + [pltpu.VMEM((tile, D_FF), BF16)],
        compiler_params=pltpu.CompilerParams(
            dimension_semantics=("parallel", "parallel"), vmem_limit_bytes=VMEM_LIMIT),
        name="ffn_grid" if grid_conv else "ffn",
    )(*args)


def _pad_lowrank(w):
    z = jnp.zeros_like(w[0])
    return jnp.stack([jnp.concatenate([w[0], z], axis=0),
                      jnp.concatenate([z, w[1]], axis=0)]).astype(BF16)


def _pack_state(s):
    bsz = s.shape[0]
    x = s.reshape(bsz, 2, N_GROUPS, HEADS_PER_GROUP, HEAD, HEAD)
    return jnp.swapaxes(x, 3, 4).reshape(bsz, 2, N_GROUPS, HEAD, GROUP)


def _unpack_state(x):
    bsz = x.shape[0]
    s = x.reshape(bsz, 2, N_GROUPS, HEAD, HEADS_PER_GROUP, HEAD)
    return jnp.swapaxes(s, 3, 4).reshape(bsz, 2, N_RHEADS, HEAD, HEAD)


def kernel(x_prompt, x_sample, state_wkv, c, c_ctx, ada_w, ada_b, norm1_g, w_in, shift_k, conv_dw, conv_dw_b, conv_ln_g, conv_ln_b, w0, w2, a0, a2, g2, k_k, k_a, r_k, ln_x_g, ln_x_b, w_out, norm2_g, ffn_up, ffn_dw, ffn_dw_b, ffn_down, final_g):
    assert ada_w.shape[0] == 1, "single layer"
    dec_b = x_sample.shape[0]
    cond8 = jnp.zeros((8, D_MODEL), F32).at[0].set(c_ctx).at[1:1 + dec_b].set(c)
    mods = _adaln(cond8, ada_w[0], ada_b).reshape(8, 6, D_MODEL)

    p = {
        'norm1_g': norm1_g, 'w_in': w_in[0].astype(BF16), 'shift_k': shift_k[0],
        'conv_dw': jnp.concatenate([conv_dw[0], jnp.zeros((1, D_CONV), F32)], axis=0),
        'conv_dw_b': conv_dw_b, 'conv_ln_g': conv_ln_g, 'conv_ln_b': conv_ln_b,
        'w0': w0[0], 'w2p': _pad_lowrank(w2[0]), 'a0': a0[0], 'a2p': _pad_lowrank(a2[0]),
        'k_k': k_k, 'k_a': k_a, 'r_k': r_k.reshape(1, D_RWKV),
        'ln_x_g': ln_x_g, 'ln_x_b': ln_x_b, 'g2': g2[0].astype(BF16),
        'w_out': w_out[0].astype(BF16), 'norm2_g': norm2_g,
        'ffn_up': ffn_up[0].astype(BF16), 'ffn_dw9': ffn_dw[0].reshape(9, 2 * D_FF),
        'ffn_dw3': ffn_dw[0, 1], 'ffn_dw_b': ffn_dw_b, 'ffn_down': ffn_down[0].astype(BF16),
        'final_g': final_g.reshape(1, D_MODEL),
    }

    def ctx_cond(b):
        return 0

    def lat_cond(b):
        return b + 1

    seq = x_prompt.shape[1]
    u, rws = _pre(x_prompt, mods, ctx_cond, p, tile=seq, has_halo=False)
    y, sfin = _wkv(rws, None, p, tile=seq, has_sout=True)
    x1 = _mix(x_prompt, [y], rws, u, mods, ctx_cond, p, tile=seq)
    y_prompt = _ffn(x1, mods, ctx_cond, p, tile=seq, grid_conv=False)
    new_state = _unpack_state(sfin)[:, None].astype(state_wkv.dtype)

    s0_bd = _pack_state(state_wkv[:, 0].astype(F32))
    u, rws = _pre(x_sample, mods, lat_cond, p, tile=512, has_halo=True)
    yf, yb = _wkv(rws, s0_bd, p, tile=256, has_sout=False)
    x1 = _mix(x_sample, [yf, yb], rws, u, mods, lat_cond, p, tile=512)
    y_sample = _ffn(x1, mods, lat_cond, p, tile=512, grid_conv=True)

    return (y_prompt, y_sample, new_state)
```

```python
import functools
import math

import jax
import jax.numpy as jnp
from jax import lax
from jax.experimental import pallas as pl
from jax.experimental.pallas import tpu as pltpu

F32 = jnp.float32
BF16 = jnp.bfloat16

D_MODEL = 1024
D_CONV = 512
D_RWKV = 512
HEAD = 64
N_RHEADS = 8
CONV_W = 31
R_LO = 64
R_G = 128
D_FF = 2560
RW_COLS = 3 * D_RWKV + 4 * R_LO + R_G
P_IN = 2 * D_CONV + RW_COLS
GRID_W = 64
EPS_RMS = 1e-6
EPS_LN = 1e-5
EPS_GN = 64e-5
L2_EPS = 1e-24
EXP_M05 = math.exp(-0.5)

CHUNK = 64
GROUP = 256
HEADS_PER_GROUP = GROUP // HEAD
N_GROUPS = D_RWKV // GROUP
CONV_HALO = 16
assert CONV_HALO - CONV_W // 2 == 1
FFN_HALO = GRID_W
VMEM_LIMIT = 56 * 1024 * 1024

WKV_UNROLL = 2

NN = (((1,), (0,)), ((), ()))
NT = (((1,), (1,)), ((), ()))


def _dot(a, b, dims=NN):
    return lax.dot_general(a, b, dims, preferred_element_type=F32)


def _split2(x):
    hi = x.astype(BF16)
    lo = (x - hi.astype(F32)).astype(BF16)
    return hi, lo


def _bdot(a, b, dims=NN):
    return _dot(a.astype(BF16), b.astype(BF16), dims)


def _sigmoid(x):
    return jax.nn.sigmoid(x)


def _head_ones():
    r = lax.broadcasted_iota(jnp.int32, (GROUP, GROUP), 0) // HEAD
    c = lax.broadcasted_iota(jnp.int32, (GROUP, GROUP), 1) // HEAD
    return jnp.where(r == c, 1.0, 0.0).astype(BF16)


def _head_sum(x, ones_bd):
    t = x.shape[0]
    pieces = []
    for g in range(N_GROUPS):
        pieces.extend(_split2(x[:, g * GROUP:(g + 1) * GROUP]))
    s = _dot(jnp.concatenate(pieces, axis=0), ones_bd)
    return jnp.concatenate([s[(2 * g) * t:(2 * g + 1) * t] + s[(2 * g + 1) * t:(2 * g + 2) * t]
                            for g in range(N_GROUPS)], axis=1)


def _rmsnorm(x, g):
    ms = jnp.mean(x * x, axis=-1, keepdims=True)
    return x * lax.rsqrt(ms + EPS_RMS) * g


def _adaln_kernel(c_ref, w_ref, b_ref, o_ref):
    c = c_ref[...]
    s = (c * _sigmoid(c)).astype(BF16)
    o_ref[...] = _dot(s, w_ref[...].astype(BF16)) + b_ref[...]


def _adaln(cond8, ada_w, ada_b):
    n_out = ada_w.shape[1]
    bn = 1024
    return pl.pallas_call(
        _adaln_kernel,
        grid=(n_out // bn,),
        in_specs=[
            pl.BlockSpec((8, D_MODEL), lambda j: (0, 0)),
            pl.BlockSpec((D_MODEL, bn), lambda j: (0, j)),
            pl.BlockSpec((1, bn), lambda j: (0, j)),
        ],
        out_specs=pl.BlockSpec((8, bn), lambda j: (0, j)),
        out_shape=jax.ShapeDtypeStruct((8, n_out), F32),
        compiler_params=pltpu.CompilerParams(dimension_semantics=("parallel",)),
        name="adaln",
    )(cond8, ada_w, ada_b)


def _pre_kernel(*refs, tile, n_tiles, has_halo):
    if has_halo:
        (x_ref, xp_ref, xn_ref, mods_ref, n1g_ref, win_ref, sk_ref, cdw_ref, cdb_ref,
         lng_ref, lnb_ref, u_ref, rws_ref, ubuf, rwbuf, pbuf) = refs
    else:
        (x_ref, mods_ref, n1g_ref, win_ref, sk_ref, cdw_ref, cdb_ref,
         lng_ref, lnb_ref, u_ref, rws_ref, ubuf, rwbuf, pbuf) = refs
    i = pl.program_id(1)
    sh1 = mods_ref[0, 0:1, :]
    sc1 = mods_ref[0, 1:2, :]

    if has_halo:
        x = jnp.concatenate([xp_ref[0], x_ref[0], xn_ref[0]], axis=0)
    else:
        x = x_ref[0]
    h = _rmsnorm(x, n1g_ref[...]) * (1.0 + sc1) + sh1
    proj = _dot(h.astype(BF16), win_ref[...])
    cv = proj[:, 0:D_CONV]
    cg = proj[:, D_CONV:2 * D_CONV]
    rw = proj[:, 2 * D_CONV:]
    u0 = cv * _sigmoid(cg)

    if has_halo:
        pv = jnp.where(i > 0, 1.0, 0.0).astype(F32)
        nv = jnp.where(i < n_tiles - 1, 1.0, 0.0).astype(F32)
        ubuf[0:CONV_HALO, :] = u0[0:CONV_HALO] * pv
        ubuf[CONV_HALO:CONV_HALO + tile, :] = u0[CONV_HALO:CONV_HALO + tile]
        ubuf[CONV_HALO + tile:, :] = u0[CONV_HALO + tile:] * nv
        rwbuf[0:8, :] = rw[CONV_HALO - 8:CONV_HALO] * pv
        rwbuf[8:8 + tile, :] = rw[CONV_HALO:CONV_HALO + tile]
        rwbuf[8 + tile:, :] = rw[CONV_HALO + tile:CONV_HALO + tile + 8] * nv
    else:
        ubuf[0:CONV_HALO, :] = jnp.zeros((CONV_HALO, D_CONV), F32)
        ubuf[CONV_HALO:CONV_HALO + tile, :] = u0
        ubuf[CONV_HALO + tile:, :] = jnp.zeros((CONV_HALO, D_CONV), F32)
        rwbuf[0:8, :] = jnp.zeros((8, RW_COLS), F32)
        rwbuf[8:8 + tile, :] = rw
        rwbuf[8 + tile:, :] = jnp.zeros((8, RW_COLS), F32)

    ext = tile + 8
    for s in range(8):
        ps = None
        for m in range(4):
            j = 8 * m + s - 1
            if 0 <= j < CONV_W:
                term = ubuf[pl.ds(8 * m, ext), :] * cdw_ref[j:j + 1, :]
                ps = term if ps is None else ps + term
        pbuf[s] = ps
    acc = pbuf[0, 0:tile, :] + cdb_ref[...]
    for s in range(1, 8):
        acc = acc + pbuf[s, pl.ds(s, tile), :]
    mu = jnp.mean(acc, axis=-1, keepdims=True)
    dv = acc - mu
    var = jnp.mean(dv * dv, axis=-1, keepdims=True)
    ln = dv * lax.rsqrt(var + EPS_LN) * lng_ref[...] + lnb_ref[...]
    u_ref[0] = ln * _sigmoid(ln)

    rws = rwbuf[pl.ds(7, tile), :] * sk_ref[0:1, :]
    rws = rws + rwbuf[pl.ds(8, tile), :] * sk_ref[1:2, :]
    rws = rws + rwbuf[pl.ds(9, tile), :] * sk_ref[2:3, :]
    rws_ref[0] = rws


def _const_spec(shape):
    nd = len(shape)
    return pl.BlockSpec(shape, lambda b, i: (0,) * nd)


def _pre(x, mods, cond_of_b, p, tile, has_halo):
    bsz, seq, _ = x.shape
    n_tiles = seq // tile
    hb = tile // CONV_HALO
    n_hblk = seq // CONV_HALO
    in_specs = [pl.BlockSpec((1, tile, D_MODEL), lambda b, i: (b, i, 0))]
    args = [x]
    if has_halo:
        in_specs += [
            pl.BlockSpec((1, CONV_HALO, D_MODEL), lambda b, i: (b, jnp.maximum(i * hb - 1, 0), 0)),
            pl.BlockSpec((1, CONV_HALO, D_MODEL),
                         lambda b, i: (b, jnp.minimum((i + 1) * hb, n_hblk - 1), 0)),
        ]
        args += [x, x]
    in_specs += [
        pl.BlockSpec((1, 6, D_MODEL), lambda b, i: (cond_of_b(b), 0, 0)),
        _const_spec((1, D_MODEL)),
        _const_spec((D_MODEL, P_IN)),
        _const_spec((3, RW_COLS)),
        _const_spec((32, D_CONV)),
        _const_spec((1, D_CONV)),
        _const_spec((1, D_CONV)),
        _const_spec((1, D_CONV)),
    ]
    args += [mods, p['norm1_g'], p['w_in'], p['shift_k'], p['conv_dw'], p['conv_dw_b'],
             p['conv_ln_g'], p['conv_ln_b']]
    return pl.pallas_call(
        functools.partial(_pre_kernel, tile=tile, n_tiles=n_tiles, has_halo=has_halo),
        grid=(bsz, n_tiles),
        in_specs=in_specs,
        out_specs=[
            pl.BlockSpec((1, tile, D_CONV), lambda b, i: (b, i, 0)),
            pl.BlockSpec((1, tile, RW_COLS), lambda b, i: (b, i, 0)),
        ],
        out_shape=[
            jax.ShapeDtypeStruct((bsz, seq, D_CONV), F32),
            jax.ShapeDtypeStruct((bsz, seq, RW_COLS), F32),
        ],
        scratch_shapes=[
            pltpu.VMEM((tile + 2 * CONV_HALO, D_CONV), F32),
            pltpu.VMEM((tile + 16, RW_COLS), F32),
            pltpu.VMEM((8, tile + 8, D_CONV), F32),
        ],
        compiler_params=pltpu.CompilerParams(
            dimension_semantics=("parallel", "parallel"), vmem_limit_bytes=VMEM_LIMIT),
        name="pre_halo" if has_halo else "pre",
    )(*args)


def _wkv_kernel(*refs, tile, n_tiles, has_s0, has_sout):
    merged = n_tiles == 1
    refs = list(refs)
    if merged:
        rf_ref = rb_ref = refs[0]
        pos = 1
    else:
        rf_ref, rb_ref = refs[0], refs[1]
        pos = 2
    s0_ref = None
    if has_s0:
        s0_ref = refs[pos]
        pos += 1
    w0_ref, w2_ref, a0_ref, a2_ref, kk_ref, ka_ref = refs[pos:pos + 6]
    pos += 6
    if merged:
        yf_ref = yb_ref = refs[pos]
        pos += 1
    else:
        yf_ref, yb_ref = refs[pos], refs[pos + 1]
        pos += 2
    sout_ref = None
    if has_sout:
        sout_ref = refs[pos]
        pos += 1
    (mst,) = refs[pos:]

    i = pl.program_id(1)
    n_chunks = tile // CHUNK

    lane_head = lax.broadcasted_iota(jnp.int32, (1, GROUP), 1) // HEAD
    head_masks = [jnp.where(lane_head == hh, 1.0, 0.0).astype(BF16)
                  for hh in range(HEADS_PER_GROUP)]
    t_idx = lax.broadcasted_iota(jnp.int32, (CHUNK, GROUP), 0)
    s_idx = lax.broadcasted_iota(jnp.int32, (CHUNK, GROUP), 1) % CHUNK
    eye_a = jnp.where(s_idx == t_idx, 1.0, 0.0).astype(F32)

    def bd(x):
        xb = x.astype(BF16)
        return jnp.concatenate([xb * m for m in head_masks], axis=0)

    def block_transpose(x):
        xt = x.T
        return jnp.concatenate([xt[hh * HEAD:(hh + 1) * HEAD, :] for hh in range(HEADS_PER_GROUP)],
                               axis=1)

    @pl.when(i == 0)
    def _init():
        if has_s0:
            for d in range(2):
                for g in range(N_GROUPS):
                    mst[d, g] = block_transpose(s0_ref[0, d, g])
        else:
            mst[...] = jnp.zeros(mst.shape, F32)
    if merged:
        yf_ref[...] = jnp.zeros(yf_ref.shape, F32)

    ones_bd = _head_ones()
    slab = WKV_UNROLL * CHUNK
    row = lax.broadcasted_iota(jnp.int32, (slab, slab), 0)
    col = lax.broadcasted_iota(jnp.int32, (slab, slab), 1)
    same_chunk = (row // CHUNK) == (col // CHUNK)
    tri = [jnp.where(same_chunk & (col <= row), 1.0, 0.0).astype(BF16),
           jnp.where(same_chunk & (col >= row), 1.0, 0.0).astype(BF16)]
    strict = [jnp.where(s_idx < t_idx, 1.0, 0.0).astype(F32),
              jnp.where(s_idx > t_idx, 1.0, 0.0).astype(F32)]
    incl = [jnp.where(s_idx <= t_idx, 1.0, 0.0).astype(F32),
            jnp.where(s_idx >= t_idx, 1.0, 0.0).astype(F32)]

    chains = [(d, g, u) for u in range(WKV_UNROLL) for d in range(2) for g in range(N_GROUPS)]

    def loop_step(it, carry):
        slab0 = [pl.multiple_of(it * slab, slab), pl.multiple_of(tile - (it + 1) * slab, slab)]

        def chunk_lo(d, u):
            return (u if d == 0 else WKV_UNROLL - 1 - u) * CHUNK

        rows = {(d, u): pl.ds(pl.multiple_of(slab0[d] + chunk_lo(d, u), CHUNK), CHUNK)
                for d in range(2) for u in range(WKV_UNROLL)}

        kt, q, kh, bh, khp, bhp, v, epc = ({} for _ in range(8))
        for d in range(2):
            src = rf_ref if d == 0 else rb_ref
            srows = pl.ds(slab0[d], slab)
            r_t = src[0, srows, 0:D_RWKV]
            k_t = src[0, srows, D_RWKV:2 * D_RWKV]
            v_t = src[0, srows, 2 * D_RWKV:3 * D_RWKV]
            wlo = src[0, srows, 3 * D_RWKV:3 * D_RWKV + 2 * R_LO]
            alo = src[0, srows, 3 * D_RWKV + 2 * R_LO:3 * D_RWKV + 4 * R_LO]
            kk = k_t * kk_ref[...]
            ss = _head_sum(kk * kk, ones_bd)
            kk = kk * lax.rsqrt(jnp.maximum(ss, L2_EPS))
            w_raw = w0_ref[d:d + 1, :] + _dot(jnp.tanh(wlo).astype(BF16), w2_ref[d])
            logw = -_sigmoid(w_raw) * EXP_M05
            a = _sigmoid(a0_ref[d:d + 1, :] + _dot(alo.astype(BF16), a2_ref[d]))
            kdir = k_t * (1.0 + (a - 1.0) * ka_ref[...])
            bvec = kk * a
            lw_hi, lw_lo = _split2(logw)
            cum = _dot(tri[d], lw_hi) + _dot(tri[d], lw_lo)
            edge = CHUNK - 1 if d == 0 else 0
            tot = jnp.concatenate(
                [jnp.broadcast_to(cum[c * CHUNK + edge:c * CHUNK + edge + 1, :], (CHUNK, D_RWKV))
                 for c in range(WKV_UNROLL)], axis=0)
            e_in = jnp.exp(cum)
            e_neg = jnp.exp(-cum)
            e_tot = jnp.exp(tot)
            e_rem = e_tot * e_neg
            full = {'kt': kk * jnp.exp(cum - logw), 'q': r_t * e_in, 'kh': kdir * e_neg,
                    'bh': bvec * e_neg, 'khp': kdir * e_rem, 'bhp': bvec * e_rem, 'v': v_t,
                    'epc': e_tot}
            for g in range(N_GROUPS):
                lanes = slice(g * GROUP, (g + 1) * GROUP)
                for u in range(WKV_UNROLL):
                    lo = chunk_lo(d, u)
                    ch = (d, g, u)
                    for dst, name in ((kt, 'kt'), (q, 'q'), (kh, 'kh'), (bh, 'bh'), (khp, 'khp'),
                                      (bhp, 'bhp'), (v, 'v')):
                        dst[ch] = full[name][lo:lo + CHUNK, lanes]
                    epc[ch] = full['epc'][lo:lo + 1, lanes]

        def stack(*xs):
            return jnp.concatenate([x.astype(BF16) for x in xs], axis=0)

        lhs2 = {ch: stack(kt[ch], q[ch]) for ch in chains}
        a1 = {ch: _dot(lhs2[ch], bd(kh[ch]), NT) for ch in chains}
        a2 = {ch: _dot(lhs2[ch], bd(bh[ch]), NT) for ch in chains}
        a_ak = {ch: a1[ch][0:CHUNK] * strict[ch[0]] for ch in chains}
        a_qk = {ch: a1[ch][CHUNK:] * incl[ch[0]] for ch in chains}
        a_ab = {ch: a2[ch][0:CHUNK] * strict[ch[0]] for ch in chains}
        a_qb = {ch: a2[ch][CHUNK:] * incl[ch[0]] for ch in chains}

        khp_t = {ch: block_transpose(khp[ch]) for ch in chains}
        bhp_t = {ch: block_transpose(bhp[ch]) for ch in chains}

        t_inv = {ch: eye_a - a_ab[ch] for ch in chains}
        n_pow = {ch: _bdot(a_ab[ch], bd(a_ab[ch])) for ch in chains}
        for _ in range(4):
            both = {ch: _dot(stack(t_inv[ch], n_pow[ch]), bd(n_pow[ch])) for ch in chains}
            t_inv = {ch: t_inv[ch] + both[ch][0:CHUNK] for ch in chains}
            n_pow = {ch: both[ch][CHUNK:] for ch in chains}
        t_inv = {ch: t_inv[ch] + _bdot(t_inv[ch], bd(n_pow[ch])) for ch in chains}

        on_v = {ch: _dot(stack(a_ak[ch], a_qk[ch], khp_t[ch]), bd(v[ch])) for ch in chains}
        av = {ch: on_v[ch][0:CHUNK] for ch in chains}
        lhs_qb = {ch: _dot(stack(a_qb[ch], bhp_t[ch]), bd(t_inv[ch])).astype(BF16) for ch in chains}
        on_ktp = {ch: _dot(lhs_qb[ch], bd(kt[ch])) for ch in chains}
        on_vp = {ch: _dot(lhs_qb[ch], bd(av[ch])) for ch in chains}
        qp = {ch: q[ch] - on_ktp[ch][0:CHUNK] for ch in chains}
        yp = {ch: on_v[ch][CHUNK:2 * CHUNK] - on_vp[ch][0:CHUNK] for ch in chains}
        gp = {ch: eye_a * epc[ch] - on_ktp[ch][CHUNK:] for ch in chains}
        hp = {ch: on_v[ch][2 * CHUNK:] - on_vp[ch][CHUNK:] for ch in chains}

        for d in range(2):
            y_ref = yf_ref if d == 0 else yb_ref
            for g in range(N_GROUPS):
                m = mst[d, g]
                for u in range(WKV_UNROLL):
                    ch = (d, g, u)
                    on_m = _dot(stack(qp[ch], gp[ch]), bd(m))
                    y = on_m[0:CHUNK] + yp[ch]
                    lanes = slice(g * GROUP, (g + 1) * GROUP)
                    if merged:
                        y_ref[0, rows[(d, u)], lanes] += y
                    else:
                        y_ref[0, rows[(d, u)], lanes] = y
                    m = on_m[CHUNK:] + hp[ch]
                mst[d, g] = m
        return carry

    lax.fori_loop(0, n_chunks // WKV_UNROLL, loop_step, 0)

    if has_sout:
        @pl.when(i == n_tiles - 1)
        def _fin():
            for d in range(2):
                for g in range(N_GROUPS):
                    sout_ref[0, d, g] = block_transpose(mst[d, g])


def _wkv(rws, s0_bd, p, tile, has_sout):
    bsz, seq, _ = rws.shape
    n_tiles = seq // tile
    has_s0 = s0_bd is not None
    merged = n_tiles == 1
    in_specs = [pl.BlockSpec((1, tile, RW_COLS), lambda b, i: (b, i, 0))]
    args = [rws]
    if not merged:
        in_specs.append(pl.BlockSpec((1, tile, RW_COLS), lambda b, i: (b, n_tiles - 1 - i, 0)))
        args.append(rws)
    if has_s0:
        in_specs.append(pl.BlockSpec((1, 2, N_GROUPS, HEAD, GROUP), lambda b, i: (b, 0, 0, 0, 0)))
        args.append(s0_bd)
    in_specs += [
        _const_spec((2, D_RWKV)),
        _const_spec((2, 2 * R_LO, D_RWKV)),
        _const_spec((2, D_RWKV)),
        _const_spec((2, 2 * R_LO, D_RWKV)),
        _const_spec((1, D_RWKV)),
        _const_spec((1, D_RWKV)),
    ]
    args += [p['w0'], p['w2p'], p['a0'], p['a2p'], p['k_k'], p['k_a']]
    out_specs = [pl.BlockSpec((1, tile, D_RWKV), lambda b, i: (b, i, 0))]
    out_shape = [jax.ShapeDtypeStruct((bsz, seq, D_RWKV), F32)]
    if not merged:
        out_specs.append(pl.BlockSpec((1, tile, D_RWKV), lambda b, i: (b, n_tiles - 1 - i, 0)))
        out_shape.append(jax.ShapeDtypeStruct((bsz, seq, D_RWKV), F32))
    if has_sout:
        out_specs.append(pl.BlockSpec((1, 2, N_GROUPS, HEAD, GROUP), lambda b, i: (b, 0, 0, 0, 0)))
        out_shape.append(jax.ShapeDtypeStruct((bsz, 2, N_GROUPS, HEAD, GROUP), F32))
    return pl.pallas_call(
        functools.partial(_wkv_kernel, tile=tile, n_tiles=n_tiles, has_s0=has_s0,
                          has_sout=has_sout),
        grid=(bsz, n_tiles),
        in_specs=in_specs,
        out_specs=out_specs,
        out_shape=out_shape,
        scratch_shapes=[pltpu.VMEM((2, N_GROUPS, HEAD, GROUP), F32)],
        compiler_params=pltpu.CompilerParams(
            dimension_semantics=("arbitrary", "arbitrary"), vmem_limit_bytes=VMEM_LIMIT),
        name="wkv_s0" if has_s0 else "wkv",
    )(*args)


def _mix_kernel(*refs, n_y):
    x_ref = refs[0]
    y_refs = refs[1:1 + n_y]
    (rws_ref, u_ref, mods_ref, rk_ref, lxg_ref, lxb_ref, g2_ref, wout_ref,
     x1_ref) = refs[1 + n_y:]
    ones_bd = _head_ones()
    g1 = mods_ref[0, 2:3, :]
    o = y_refs[0][0]
    for y_ref in y_refs[1:]:
        o = o + y_ref[0]
    mu = _head_sum(o, ones_bd) * (1.0 / HEAD)
    dv = o - mu
    var = _head_sum(dv * dv, ones_bd) * (1.0 / HEAD)
    on = dv * lax.rsqrt(var + EPS_GN) * lxg_ref[...] + lxb_ref[...]
    r = rws_ref[0, :, 0:D_RWKV]
    k = rws_ref[0, :, D_RWKV:2 * D_RWKV]
    v = rws_ref[0, :, 2 * D_RWKV:3 * D_RWKV]
    glo = rws_ref[0, :, 3 * D_RWKV + 4 * R_LO:]
    bonus = _head_sum(r * k * rk_ref[...], ones_bd) * v
    gate = _dot(_sigmoid(glo).astype(BF16), g2_ref[...])
    o2 = (on + bonus) * gate
    mixed = _dot(u_ref[0].astype(BF16), wout_ref[0:D_CONV, :])
    mixed = mixed + _dot(o2.astype(BF16), wout_ref[D_CONV:, :])
    x1_ref[0] = x_ref[0] + g1 * mixed


def _mix(x, ys, rws, u, mods, cond_of_b, p, tile):
    bsz, seq, _ = x.shape
    n_tiles = seq // tile

    def tok(width):
        return pl.BlockSpec((1, tile, width), lambda b, i: (b, i, 0))

    return pl.pallas_call(
        functools.partial(_mix_kernel, n_y=len(ys)),
        grid=(bsz, n_tiles),
        in_specs=[tok(D_MODEL)] + [tok(D_RWKV) for _ in ys] + [
            tok(RW_COLS), tok(D_CONV),
            pl.BlockSpec((1, 6, D_MODEL), lambda b, i: (cond_of_b(b), 0, 0)),
            _const_spec((1, D_RWKV)),
            _const_spec((1, D_RWKV)),
            _const_spec((1, D_RWKV)),
            _const_spec((R_G, D_RWKV)),
            _const_spec((D_MODEL, D_MODEL)),
        ],
        out_specs=tok(D_MODEL),
        out_shape=jax.ShapeDtypeStruct((bsz, seq, D_MODEL), F32),
        compiler_params=pltpu.CompilerParams(
            dimension_semantics=("parallel", "parallel"), vmem_limit_bytes=VMEM_LIMIT),
        name="mix",
    )(x, *ys, rws, u, mods, p['r_k'], p['ln_x_g'], p['ln_x_b'], p['g2'], p['w_out'])


FFN_COLS = 512
FFN_PAD = 8


def _ffn_kernel(*refs, tile, n_tiles, grid_conv, period):
    if grid_conv:
        (x_ref, xp_ref, xn_ref, mods_ref, n2g_ref, up_ref, dw_ref, dwb_ref, down_ref, fg_ref,
         o_ref, vbuf, gbuf, vl, vr, gl, gr, act_s) = refs
    else:
        (x_ref, mods_ref, n2g_ref, up_ref, dw_ref, dwb_ref, down_ref, fg_ref,
         o_ref, vbuf, gbuf, vl, vr, gl, gr, act_s) = refs
    i = pl.program_id(1)
    sh2 = mods_ref[0, 3:4, :]
    sc2 = mods_ref[0, 4:5, :]
    g2 = mods_ref[0, 5:6, :]
    x1 = x_ref[0]
    halo = FFN_HALO if grid_conv else 0
    if grid_conv:
        xe = jnp.concatenate([xp_ref[0], x1, xn_ref[0]], axis=0)
    else:
        xe = x1
    rows_e = tile + 2 * halo
    h2 = _rmsnorm(xe, n2g_ref[...]) * (1.0 + sc2) + sh2

    if grid_conv:
        e_idx = lax.broadcasted_iota(jnp.int32, (rows_e, 1), 0)
        pv = jnp.where(i > 0, 1.0, 0.0).astype(F32)
        nv = jnp.where(i < n_tiles - 1, 1.0, 0.0).astype(F32)
        h2 = h2 * (jnp.where(e_idx >= halo, 1.0, pv) * jnp.where(e_idx < halo + tile, 1.0, nv))
        taps = [(dr, dc) for dr in (-1, 0, 1) for dc in (-1, 0, 1)]
    else:
        taps = [(0, -1), (0, 0), (0, 1)]
    masked = period < tile
    if masked:
        colpos = lax.broadcasted_iota(jnp.int32, (tile, 1), 0) % period
        m_left = jnp.where(colpos >= 1, 1.0, 0.0)
        m_right = jnp.where(colpos <= period - 2, 1.0, 0.0)

    zpad = jnp.zeros((FFN_PAD, FFN_COLS), F32)
    vbuf[0:FFN_PAD, :] = zpad
    vbuf[FFN_PAD + rows_e:, :] = zpad
    gbuf[0:FFN_PAD, :] = zpad
    gbuf[FFN_PAD + rows_e:, :] = zpad

    def conv(buf, side_l, side_r, col0):
        ext = tile + 2 * FFN_PAD
        sums = {}
        for ti, (dr, dc) in enumerate(taps):
            win = buf[pl.ds(halo + GRID_W * dr, ext), :]
            term = win * dw_ref[ti:ti + 1, col0:col0 + FFN_COLS]
            sums[dc] = term if dc not in sums else sums[dc] + term
        side_l[...] = sums[-1]
        side_r[...] = sums[1]
        left = side_l[pl.ds(FFN_PAD - 1, tile), :]
        right = side_r[pl.ds(FFN_PAD + 1, tile), :]
        if masked:
            left = left * m_left
            right = right * m_right
        centre = sums[0][FFN_PAD:FFN_PAD + tile]
        return centre + left + right + dwb_ref[:, col0:col0 + FFN_COLS]

    hb = h2.astype(BF16)
    for cc in range(D_FF // FFN_COLS):
        c0 = cc * FFN_COLS
        vbuf[FFN_PAD:FFN_PAD + rows_e, :] = _dot(hb, up_ref[:, c0:c0 + FFN_COLS])
        gbuf[FFN_PAD:FFN_PAD + rows_e, :] = _dot(hb, up_ref[:, D_FF + c0:D_FF + c0 + FFN_COLS])
        val = conv(vbuf, vl, vr, c0)
        gate = conv(gbuf, gl, gr, D_FF + c0)
        act_s[:, c0:c0 + FFN_COLS] = (val * (gate * _sigmoid(gate))).astype(BF16)

    x2 = x1 + g2 * _dot(act_s[...], down_ref[...])
    o_ref[0] = _rmsnorm(x2, fg_ref[...])


def _ffn(x1, mods, cond_of_b, p, tile, grid_conv, period):
    bsz, seq, _ = x1.shape
    n_tiles = seq // tile
    in_specs = [pl.BlockSpec((1, tile, D_MODEL), lambda b, i: (b, i, 0))]
    args = [x1]
    if grid_conv:
        hb = tile // FFN_HALO
        n_hblk = seq // FFN_HALO
        in_specs += [
            pl.BlockSpec((1, FFN_HALO, D_MODEL), lambda b, i: (b, jnp.maximum(i * hb - 1, 0), 0)),
            pl.BlockSpec((1, FFN_HALO, D_MODEL),
                         lambda b, i: (b, jnp.minimum((i + 1) * hb, n_hblk - 1), 0)),
        ]
        args += [x1, x1]
        dw = p['ffn_dw9']
    else:
        dw = p['ffn_dw3']
    n_taps = dw.shape[0]
    in_specs += [
        pl.BlockSpec((1, 6, D_MODEL), lambda b, i: (cond_of_b(b), 0, 0)),
        _const_spec((1, D_MODEL)),
        _const_spec((D_MODEL, 2 * D_FF)),
        _const_spec((n_taps, 2 * D_FF)),
        _const_spec((1, 2 * D_FF)),
        _const_spec((D_FF, D_MODEL)),
        _const_spec((1, D_MODEL)),
    ]
    args += [mods, p['norm2_g'], p['ffn_up'], dw, p['ffn_dw_b'], p['ffn_down'], p['final_g']]
    halo = FFN_HALO if grid_conv else 0
    buf_rows = tile + 2 * halo + 2 * FFN_PAD
    return pl.pallas_call(
        functools.partial(_ffn_kernel, tile=tile, n_tiles=n_tiles, grid_conv=grid_conv,
                          period=period),
        grid=(bsz, n_tiles),
        in_specs=in_specs,
        out_specs=pl.BlockSpec((1, tile, D_MODEL), lambda b, i: (b, i, 0)),
        out_shape=jax.ShapeDtypeStruct((bsz, seq, D_MODEL), F32),
        scratch_shapes=[pltpu.VMEM((buf_rows, FFN_COLS), F32),
                        pltpu.VMEM((buf_rows, FFN_COLS), F32)]
        + [pltpu.VMEM((tile + 2 * FFN_PAD, FFN_COLS), F32) for _ in range(4)]
        + [pltpu.VMEM((tile, D_FF), BF16)],
        compiler_params=pltpu.CompilerParams(
            dimension_semantics=("parallel", "parallel"), vmem_limit_bytes=VMEM_LIMIT),
        name="ffn_grid" if grid_conv else "ffn",
    )(*args)


def _pad_lowrank(w):
    z = jnp.zeros_like(w[0])
    return jnp.stack([jnp.concatenate([w[0], z], axis=0),
                      jnp.concatenate([z, w[1]], axis=0)]).astype(BF16)


def _pack_state(s):
    bsz = s.shape[0]
    x = s.reshape(bsz, 2, N_GROUPS, HEADS_PER_GROUP, HEAD, HEAD)
    return jnp.swapaxes(x, 3, 4).reshape(bsz, 2, N_GROUPS, HEAD, GROUP)


def _unpack_state(x):
    bsz = x.shape[0]
    s = x.reshape(bsz, 2, N_GROUPS, HEAD, HEADS_PER_GROUP, HEAD)
    return jnp.swapaxes(s, 3, 4).reshape(bsz, 2, N_RHEADS, HEAD, HEAD)


def kernel(x_prompt, x_sample, state_wkv, c, c_ctx, ada_w, ada_b, norm1_g, w_in, shift_k, conv_dw, conv_dw_b, conv_ln_g, conv_ln_b, w0, w2, a0, a2, g2, k_k, k_a, r_k, ln_x_g, ln_x_b, w_out, norm2_g, ffn_up, ffn_dw, ffn_dw_b, ffn_down, final_g):
    assert ada_w.shape[0] == 1, "single layer"
    dec_b = x_sample.shape[0]
    cond8 = jnp.zeros((8, D_MODEL), F32).at[0].set(c_ctx).at[1:1 + dec_b].set(c)
    mods = _adaln(cond8, ada_w[0], ada_b).reshape(8, 6, D_MODEL)

    p = {
        'norm1_g': norm1_g, 'w_in': w_in[0].astype(BF16), 'shift_k': shift_k[0],
        'conv_dw': jnp.concatenate([conv_dw[0], jnp.zeros((1, D_CONV), F32)], axis=0),
        'conv_dw_b': conv_dw_b, 'conv_ln_g': conv_ln_g, 'conv_ln_b': conv_ln_b,
        'w0': w0[0], 'w2p': _pad_lowrank(w2[0]), 'a0': a0[0], 'a2p': _pad_lowrank(a2[0]),
        'k_k': k_k, 'k_a': k_a, 'r_k': r_k.reshape(1, D_RWKV),
        'ln_x_g': ln_x_g, 'ln_x_b': ln_x_b, 'g2': g2[0].astype(BF16),
        'w_out': w_out[0].astype(BF16), 'norm2_g': norm2_g,
        'ffn_up': ffn_up[0].astype(BF16), 'ffn_dw9': ffn_dw[0].reshape(9, 2 * D_FF),
        'ffn_dw3': ffn_dw[0, 1], 'ffn_dw_b': ffn_dw_b, 'ffn_down': ffn_down[0].astype(BF16),
        'final_g': final_g.reshape(1, D_MODEL),
    }

    def ctx_cond(b):
        return 0

    def lat_cond(b):
        return b + 1

    seq = x_prompt.shape[1]
    u, rws = _pre(x_prompt, mods, ctx_cond, p, tile=seq, has_halo=False)
    y, sfin = _wkv(rws, None, p, tile=seq, has_sout=True)
    bsz = x_prompt.shape[0]

    def pair(t):
        return t.reshape(bsz // 2, 2 * seq, t.shape[-1])

    x1 = _mix(pair(x_prompt), [pair(y)], pair(rws), pair(u), mods, ctx_cond, p, tile=2 * seq)
    y_prompt = _ffn(x1.reshape(x_prompt.shape), mods, ctx_cond, p, tile=seq, grid_conv=False,
                    period=seq)
    new_state = _unpack_state(sfin)[:, None].astype(state_wkv.dtype)

    s0_bd = _pack_state(state_wkv[:, 0].astype(F32))
    u, rws = _pre(x_sample, mods, lat_cond, p, tile=512, has_halo=True)
    yf, yb = _wkv(rws, s0_bd, p, tile=256, has_sout=False)
    x1 = _mix(x_sample, [yf, yb], rws, u, mods, lat_cond, p, tile=512)
    y_sample = _ffn(x1, mods, lat_cond, p, tile=512, grid_conv=True, period=GRID_W)

    return (y_prompt, y_sample, new_state)
```

```python
import functools
import math

import jax
import jax.numpy as jnp
from jax import lax
from jax.experimental import pallas as pl
from jax.experimental.pallas import tpu as pltpu

F32 = jnp.float32
BF16 = jnp.bfloat16

D_MODEL = 1024
D_CONV = 512
D_RWKV = 512
HEAD = 64
N_RHEADS = 8
CONV_W = 31
R_LO = 64
R_G = 128
D_FF = 2560
RW_COLS = 3 * D_RWKV + 4 * R_LO + R_G
P_IN = 2 * D_CONV + RW_COLS
GRID_W = 64
EPS_RMS = 1e-6
EPS_LN = 1e-5
EPS_GN = 64e-5
L2_EPS = 1e-24
EXP_M05 = math.exp(-0.5)

CHUNK = 64
GROUP = 256
HEADS_PER_GROUP = GROUP // HEAD
N_GROUPS = D_RWKV // GROUP
CONV_HALO = 16
assert CONV_HALO - CONV_W // 2 == 1
FFN_HALO = GRID_W
VMEM_LIMIT = 56 * 1024 * 1024

WKV_UNROLL = 2

NN = (((1,), (0,)), ((), ()))
NT = (((1,), (1,)), ((), ()))


def _dot(a, b, dims=NN):
    return lax.dot_general(a, b, dims, preferred_element_type=F32)


def _split2(x):
    hi = x.astype(BF16)
    lo = (x - hi.astype(F32)).astype(BF16)
    return hi, lo


def _bdot(a, b, dims=NN):
    return _dot(a.astype(BF16), b.astype(BF16), dims)


def _sigmoid(x):
    return jax.nn.sigmoid(x)


def _head_ones():
    r = lax.broadcasted_iota(jnp.int32, (GROUP, GROUP), 0) // HEAD
    c = lax.broadcasted_iota(jnp.int32, (GROUP, GROUP), 1) // HEAD
    return jnp.where(r == c, 1.0, 0.0).astype(BF16)


def _head_sum(x, ones_bd):
    t = x.shape[0]
    pieces = []
    for g in range(N_GROUPS):
        pieces.extend(_split2(x[:, g * GROUP:(g + 1) * GROUP]))
    s = _dot(jnp.concatenate(pieces, axis=0), ones_bd)
    return jnp.concatenate([s[(2 * g) * t:(2 * g + 1) * t] + s[(2 * g + 1) * t:(2 * g + 2) * t]
                            for g in range(N_GROUPS)], axis=1)


def _rmsnorm(x, g):
    ms = jnp.mean(x * x, axis=-1, keepdims=True)
    return x * lax.rsqrt(ms + EPS_RMS) * g


def _adaln_kernel(c_ref, w_ref, b_ref, o_ref):
    c = c_ref[...]
    s = (c * _sigmoid(c)).astype(BF16)
    o_ref[...] = _dot(s, w_ref[...].astype(BF16)) + b_ref[...]


def _adaln(cond8, ada_w, ada_b):
    n_out = ada_w.shape[1]
    bn = 1024
    return pl.pallas_call(
        _adaln_kernel,
        grid=(n_out // bn,),
        in_specs=[
            pl.BlockSpec((8, D_MODEL), lambda j: (0, 0)),
            pl.BlockSpec((D_MODEL, bn), lambda j: (0, j)),
            pl.BlockSpec((1, bn), lambda j: (0, j)),
        ],
        out_specs=pl.BlockSpec((8, bn), lambda j: (0, j)),
        out_shape=jax.ShapeDtypeStruct((8, n_out), F32),
        compiler_params=pltpu.CompilerParams(dimension_semantics=("parallel",)),
        name="adaln",
    )(cond8, ada_w, ada_b)


def _pre_kernel(*refs, tile, n_tiles, has_halo):
    if has_halo:
        (x_ref, xp_ref, xn_ref, mods_ref, n1g_ref, win_ref, sk_ref, cdw_ref, cdb_ref,
         lng_ref, lnb_ref, u_ref, rws_ref, ubuf, rwbuf, pbuf, wbf) = refs
    else:
        (x_ref, mods_ref, n1g_ref, win_ref, sk_ref, cdw_ref, cdb_ref,
         lng_ref, lnb_ref, u_ref, rws_ref, ubuf, rwbuf, pbuf, wbf) = refs
    i = pl.program_id(1)
    sh1 = mods_ref[0, 0:1, :]
    sc1 = mods_ref[0, 1:2, :]

    if has_halo:
        x = jnp.concatenate([xp_ref[0], x_ref[0], xn_ref[0]], axis=0)
    else:
        x = x_ref[0]
    h = _rmsnorm(x, n1g_ref[...]) * (1.0 + sc1) + sh1

    @pl.when((pl.program_id(0) == 0) & (i == 0))
    def _cast_weights():
        wbf[...] = win_ref[...].astype(BF16)

    proj = _dot(h.astype(BF16), wbf[...])
    cv = proj[:, 0:D_CONV]
    cg = proj[:, D_CONV:2 * D_CONV]
    rw = proj[:, 2 * D_CONV:]
    u0 = cv * _sigmoid(cg)

    if has_halo:
        pv = jnp.where(i > 0, 1.0, 0.0).astype(F32)
        nv = jnp.where(i < n_tiles - 1, 1.0, 0.0).astype(F32)
        ubuf[0:CONV_HALO, :] = u0[0:CONV_HALO] * pv
        ubuf[CONV_HALO:CONV_HALO + tile, :] = u0[CONV_HALO:CONV_HALO + tile]
        ubuf[CONV_HALO + tile:, :] = u0[CONV_HALO + tile:] * nv
        rwbuf[0:8, :] = rw[CONV_HALO - 8:CONV_HALO] * pv
        rwbuf[8:8 + tile, :] = rw[CONV_HALO:CONV_HALO + tile]
        rwbuf[8 + tile:, :] = rw[CONV_HALO + tile:CONV_HALO + tile + 8] * nv
    else:
        ubuf[0:CONV_HALO, :] = jnp.zeros((CONV_HALO, D_CONV), F32)
        ubuf[CONV_HALO:CONV_HALO + tile, :] = u0
        ubuf[CONV_HALO + tile:, :] = jnp.zeros((CONV_HALO, D_CONV), F32)
        rwbuf[0:8, :] = jnp.zeros((8, RW_COLS), F32)
        rwbuf[8:8 + tile, :] = rw
        rwbuf[8 + tile:, :] = jnp.zeros((8, RW_COLS), F32)

    ext = tile + 8
    for s in range(8):
        ps = None
        for m in range(4):
            j = 8 * m + s - 1
            if 0 <= j < CONV_W:
                term = ubuf[pl.ds(8 * m, ext), :] * cdw_ref[j:j + 1, :]
                ps = term if ps is None else ps + term
        pbuf[s] = ps
    acc = pbuf[0, 0:tile, :] + cdb_ref[...]
    for s in range(1, 8):
        acc = acc + pbuf[s, pl.ds(s, tile), :]
    mu = jnp.mean(acc, axis=-1, keepdims=True)
    dv = acc - mu
    var = jnp.mean(dv * dv, axis=-1, keepdims=True)
    ln = dv * lax.rsqrt(var + EPS_LN) * lng_ref[...] + lnb_ref[...]
    u_ref[0] = (ln * _sigmoid(ln)).astype(BF16)

    rws = rwbuf[pl.ds(7, tile), :] * sk_ref[0:1, :]
    rws = rws + rwbuf[pl.ds(8, tile), :] * sk_ref[1:2, :]
    rws = rws + rwbuf[pl.ds(9, tile), :] * sk_ref[2:3, :]
    rws_ref[0] = rws


def _const_spec(shape):
    nd = len(shape)
    return pl.BlockSpec(shape, lambda b, i: (0,) * nd)


def _pre(x, mods, cond_of_b, p, tile, has_halo):
    bsz, seq, _ = x.shape
    n_tiles = seq // tile
    hb = tile // CONV_HALO
    n_hblk = seq // CONV_HALO
    in_specs = [pl.BlockSpec((1, tile, D_MODEL), lambda b, i: (b, i, 0))]
    args = [x]
    if has_halo:
        in_specs += [
            pl.BlockSpec((1, CONV_HALO, D_MODEL), lambda b, i: (b, jnp.maximum(i * hb - 1, 0), 0)),
            pl.BlockSpec((1, CONV_HALO, D_MODEL),
                         lambda b, i: (b, jnp.minimum((i + 1) * hb, n_hblk - 1), 0)),
        ]
        args += [x, x]
    in_specs += [
        pl.BlockSpec((1, 6, D_MODEL), lambda b, i: (cond_of_b(b), 0, 0)),
        _const_spec((1, D_MODEL)),
        _const_spec((D_MODEL, P_IN)),
        _const_spec((3, RW_COLS)),
        _const_spec((32, D_CONV)),
        _const_spec((1, D_CONV)),
        _const_spec((1, D_CONV)),
        _const_spec((1, D_CONV)),
    ]
    args += [mods, p['norm1_g'], p['w_in'], p['shift_k'], p['conv_dw'], p['conv_dw_b'],
             p['conv_ln_g'], p['conv_ln_b']]
    return pl.pallas_call(
        functools.partial(_pre_kernel, tile=tile, n_tiles=n_tiles, has_halo=has_halo),
        grid=(bsz, n_tiles),
        in_specs=in_specs,
        out_specs=[
            pl.BlockSpec((1, tile, D_CONV), lambda b, i: (b, i, 0)),
            pl.BlockSpec((1, tile, RW_COLS), lambda b, i: (b, i, 0)),
        ],
        out_shape=[
            jax.ShapeDtypeStruct((bsz, seq, D_CONV), BF16),
            jax.ShapeDtypeStruct((bsz, seq, RW_COLS), F32),
        ],
        scratch_shapes=[
            pltpu.VMEM((tile + 2 * CONV_HALO, D_CONV), F32),
            pltpu.VMEM((tile + 16, RW_COLS), F32),
            pltpu.VMEM((8, tile + 8, D_CONV), F32),
            pltpu.VMEM((D_MODEL, P_IN), BF16),
        ],
        compiler_params=pltpu.CompilerParams(
            dimension_semantics=("arbitrary", "arbitrary"), vmem_limit_bytes=VMEM_LIMIT),
        name="pre_halo" if has_halo else "pre",
    )(*args)


def _wkv_kernel(*refs, tile, n_tiles, has_s0, has_sout):
    merged = n_tiles == 1
    refs = list(refs)
    if merged:
        rf_ref = rb_ref = refs[0]
        pos = 1
    else:
        rf_ref, rb_ref = refs[0], refs[1]
        pos = 2
    s0_ref = None
    if has_s0:
        s0_ref = refs[pos]
        pos += 1
    w0_ref, w2_ref, a0_ref, a2_ref, kk_ref, ka_ref = refs[pos:pos + 6]
    pos += 6
    if merged:
        yf_ref = yb_ref = refs[pos]
        pos += 1
    else:
        yf_ref, yb_ref = refs[pos], refs[pos + 1]
        pos += 2
    sout_ref = None
    if has_sout:
        sout_ref = refs[pos]
        pos += 1
    (mst,) = refs[pos:]

    i = pl.program_id(1)
    n_chunks = tile // CHUNK

    lane_head = lax.broadcasted_iota(jnp.int32, (1, GROUP), 1) // HEAD
    head_masks = [jnp.where(lane_head == hh, 1.0, 0.0).astype(BF16)
                  for hh in range(HEADS_PER_GROUP)]
    t_idx = lax.broadcasted_iota(jnp.int32, (CHUNK, GROUP), 0)
    s_idx = lax.broadcasted_iota(jnp.int32, (CHUNK, GROUP), 1) % CHUNK
    eye_a = jnp.where(s_idx == t_idx, 1.0, 0.0).astype(F32)

    def bd(x):
        xb = x.astype(BF16)
        return jnp.concatenate([xb * m for m in head_masks], axis=0)

    def block_transpose(x):
        xt = x.T
        return jnp.concatenate([xt[hh * HEAD:(hh + 1) * HEAD, :] for hh in range(HEADS_PER_GROUP)],
                               axis=1)

    @pl.when(i == 0)
    def _init():
        if has_s0:
            for d in range(2):
                for g in range(N_GROUPS):
                    mst[d, g] = block_transpose(s0_ref[0, d, g])
        else:
            mst[...] = jnp.zeros(mst.shape, F32)
    if merged:
        yf_ref[...] = jnp.zeros(yf_ref.shape, F32)

    ones_bd = _head_ones()
    slab = WKV_UNROLL * CHUNK
    row = lax.broadcasted_iota(jnp.int32, (slab, slab), 0)
    col = lax.broadcasted_iota(jnp.int32, (slab, slab), 1)
    same_chunk = (row // CHUNK) == (col // CHUNK)
    tri = [jnp.where(same_chunk & (col <= row), 1.0, 0.0).astype(BF16),
           jnp.where(same_chunk & (col >= row), 1.0, 0.0).astype(BF16)]
    strict = [jnp.where(s_idx < t_idx, 1.0, 0.0).astype(F32),
              jnp.where(s_idx > t_idx, 1.0, 0.0).astype(F32)]
    incl = [jnp.where(s_idx <= t_idx, 1.0, 0.0).astype(F32),
            jnp.where(s_idx >= t_idx, 1.0, 0.0).astype(F32)]

    chains = [(d, g, u) for u in range(WKV_UNROLL) for d in range(2) for g in range(N_GROUPS)]

    def loop_step(it, carry):
        slab0 = [pl.multiple_of(it * slab, slab), pl.multiple_of(tile - (it + 1) * slab, slab)]

        def chunk_lo(d, u):
            return (u if d == 0 else WKV_UNROLL - 1 - u) * CHUNK

        rows = {(d, u): pl.ds(pl.multiple_of(slab0[d] + chunk_lo(d, u), CHUNK), CHUNK)
                for d in range(2) for u in range(WKV_UNROLL)}

        kt, q, kh, bh, khp, bhp, v, epc = ({} for _ in range(8))
        for d in range(2):
            src = rf_ref if d == 0 else rb_ref
            srows = pl.ds(slab0[d], slab)
            r_t = src[0, srows, 0:D_RWKV]
            k_t = src[0, srows, D_RWKV:2 * D_RWKV]
            v_t = src[0, srows, 2 * D_RWKV:3 * D_RWKV]
            wlo = src[0, srows, 3 * D_RWKV:3 * D_RWKV + 2 * R_LO]
            alo = src[0, srows, 3 * D_RWKV + 2 * R_LO:3 * D_RWKV + 4 * R_LO]
            kk = k_t * kk_ref[...]
            ss = _head_sum(kk * kk, ones_bd)
            kk = kk * lax.rsqrt(jnp.maximum(ss, L2_EPS))
            w_raw = w0_ref[d:d + 1, :] + _dot(jnp.tanh(wlo).astype(BF16), w2_ref[d])
            logw = -_sigmoid(w_raw) * EXP_M05
            a = _sigmoid(a0_ref[d:d + 1, :] + _dot(alo.astype(BF16), a2_ref[d]))
            kdir = k_t * (1.0 + (a - 1.0) * ka_ref[...])
            bvec = kk * a
            lw_hi, lw_lo = _split2(logw)
            cum = _dot(tri[d], lw_hi) + _dot(tri[d], lw_lo)
            edge = CHUNK - 1 if d == 0 else 0
            tot = jnp.concatenate(
                [jnp.broadcast_to(cum[c * CHUNK + edge:c * CHUNK + edge + 1, :], (CHUNK, D_RWKV))
                 for c in range(WKV_UNROLL)], axis=0)
            e_in = jnp.exp(cum)
            e_neg = jnp.exp(-cum)
            e_tot = jnp.exp(tot)
            e_rem = e_tot * e_neg
            full = {'kt': kk * jnp.exp(cum - logw), 'q': r_t * e_in, 'kh': kdir * e_neg,
                    'bh': bvec * e_neg, 'khp': kdir * e_rem, 'bhp': bvec * e_rem, 'v': v_t,
                    'epc': e_tot}
            for g in range(N_GROUPS):
                lanes = slice(g * GROUP, (g + 1) * GROUP)
                for u in range(WKV_UNROLL):
                    lo = chunk_lo(d, u)
                    ch = (d, g, u)
                    for dst, name in ((kt, 'kt'), (q, 'q'), (kh, 'kh'), (bh, 'bh'), (khp, 'khp'),
                                      (bhp, 'bhp'), (v, 'v')):
                        dst[ch] = full[name][lo:lo + CHUNK, lanes]
                    epc[ch] = full['epc'][lo:lo + 1, lanes]

        def stack(*xs):
            return jnp.concatenate([x.astype(BF16) for x in xs], axis=0)

        lhs2 = {ch: stack(kt[ch], q[ch]) for ch in chains}
        a1 = {ch: _dot(lhs2[ch], bd(kh[ch]), NT) for ch in chains}
        a2 = {ch: _dot(lhs2[ch], bd(bh[ch]), NT) for ch in chains}
        a_ak = {ch: a1[ch][0:CHUNK] * strict[ch[0]] for ch in chains}
        a_qk = {ch: a1[ch][CHUNK:] * incl[ch[0]] for ch in chains}
        a_ab = {ch: a2[ch][0:CHUNK] * strict[ch[0]] for ch in chains}
        a_qb = {ch: a2[ch][CHUNK:] * incl[ch[0]] for ch in chains}

        khp_t = {ch: block_transpose(khp[ch]) for ch in chains}
        bhp_t = {ch: block_transpose(bhp[ch]) for ch in chains}

        t_inv = {ch: eye_a - a_ab[ch] for ch in chains}
        n_pow = {ch: _bdot(a_ab[ch], bd(a_ab[ch])) for ch in chains}
        for _ in range(4):
            both = {ch: _dot(stack(t_inv[ch], n_pow[ch]), bd(n_pow[ch])) for ch in chains}
            t_inv = {ch: t_inv[ch] + both[ch][0:CHUNK] for ch in chains}
            n_pow = {ch: both[ch][CHUNK:] for ch in chains}
        t_inv = {ch: t_inv[ch] + _bdot(t_inv[ch], bd(n_pow[ch])) for ch in chains}

        on_v = {ch: _dot(stack(a_ak[ch], a_qk[ch], khp_t[ch]), bd(v[ch])) for ch in chains}
        av = {ch: on_v[ch][0:CHUNK] for ch in chains}
        lhs_qb = {ch: _dot(stack(a_qb[ch], bhp_t[ch]), bd(t_inv[ch])).astype(BF16) for ch in chains}
        on_ktp = {ch: _dot(lhs_qb[ch], bd(kt[ch])) for ch in chains}
        on_vp = {ch: _dot(lhs_qb[ch], bd(av[ch])) for ch in chains}
        qp = {ch: q[ch] - on_ktp[ch][0:CHUNK] for ch in chains}
        yp = {ch: on_v[ch][CHUNK:2 * CHUNK] - on_vp[ch][0:CHUNK] for ch in chains}
        gp = {ch: eye_a * epc[ch] - on_ktp[ch][CHUNK:] for ch in chains}
        hp = {ch: on_v[ch][2 * CHUNK:] - on_vp[ch][CHUNK:] for ch in chains}

        for d in range(2):
            y_ref = yf_ref if d == 0 else yb_ref
            for g in range(N_GROUPS):
                m = mst[d, g]
                for u in range(WKV_UNROLL):
                    ch = (d, g, u)
                    on_m = _dot(stack(qp[ch], gp[ch]), bd(m))
                    y = on_m[0:CHUNK] + yp[ch]
                    lanes = slice(g * GROUP, (g + 1) * GROUP)
                    if merged:
                        y_ref[0, rows[(d, u)], lanes] += y
                    else:
                        y_ref[0, rows[(d, u)], lanes] = y
                    m = on_m[CHUNK:] + hp[ch]
                mst[d, g] = m
        return carry

    lax.fori_loop(0, n_chunks // WKV_UNROLL, loop_step, 0)

    if has_sout:
        @pl.when(i == n_tiles - 1)
        def _fin():
            for d in range(2):
                for g in range(N_GROUPS):
                    sout_ref[0, d, g] = block_transpose(mst[d, g])


def _wkv(rws, s0_bd, p, tile, has_sout):
    bsz, seq, _ = rws.shape
    n_tiles = seq // tile
    has_s0 = s0_bd is not None
    merged = n_tiles == 1
    in_specs = [pl.BlockSpec((1, tile, RW_COLS), lambda b, i: (b, i, 0))]
    args = [rws]
    if not merged:
        in_specs.append(pl.BlockSpec((1, tile, RW_COLS), lambda b, i: (b, n_tiles - 1 - i, 0)))
        args.append(rws)
    if has_s0:
        in_specs.append(pl.BlockSpec((1, 2, N_GROUPS, HEAD, GROUP), lambda b, i: (b, 0, 0, 0, 0)))
        args.append(s0_bd)
    in_specs += [
        _const_spec((2, D_RWKV)),
        _const_spec((2, 2 * R_LO, D_RWKV)),
        _const_spec((2, D_RWKV)),
        _const_spec((2, 2 * R_LO, D_RWKV)),
        _const_spec((1, D_RWKV)),
        _const_spec((1, D_RWKV)),
    ]
    args += [p['w0'], p['w2p'], p['a0'], p['a2p'], p['k_k'], p['k_a']]
    out_specs = [pl.BlockSpec((1, tile, D_RWKV), lambda b, i: (b, i, 0))]
    out_shape = [jax.ShapeDtypeStruct((bsz, seq, D_RWKV), F32)]
    if not merged:
        out_specs.append(pl.BlockSpec((1, tile, D_RWKV), lambda b, i: (b, n_tiles - 1 - i, 0)))
        out_shape.append(jax.ShapeDtypeStruct((bsz, seq, D_RWKV), F32))
    if has_sout:
        out_specs.append(pl.BlockSpec((1, 2, N_GROUPS, HEAD, GROUP), lambda b, i: (b, 0, 0, 0, 0)))
        out_shape.append(jax.ShapeDtypeStruct((bsz, 2, N_GROUPS, HEAD, GROUP), F32))
    return pl.pallas_call(
        functools.partial(_wkv_kernel, tile=tile, n_tiles=n_tiles, has_s0=has_s0,
                          has_sout=has_sout),
        grid=(bsz, n_tiles),
        in_specs=in_specs,
        out_specs=out_specs,
        out_shape=out_shape,
        scratch_shapes=[pltpu.VMEM((2, N_GROUPS, HEAD, GROUP), F32)],
        compiler_params=pltpu.CompilerParams(
            dimension_semantics=("arbitrary", "arbitrary"), vmem_limit_bytes=VMEM_LIMIT),
        name="wkv_s0" if has_s0 else "wkv",
    )(*args)


def _mix_kernel(*refs, n_y):
    x_ref = refs[0]
    y_refs = refs[1:1 + n_y]
    (rws_ref, u_ref, mods_ref, rk_ref, lxg_ref, lxb_ref, g2_ref, wout_ref,
     x1_ref) = refs[1 + n_y:]
    ones_bd = _head_ones()
    g1 = mods_ref[0, 2:3, :]
    o = y_refs[0][0]
    for y_ref in y_refs[1:]:
        o = o + y_ref[0]
    mu = _head_sum(o, ones_bd) * (1.0 / HEAD)
    dv = o - mu
    var = _head_sum(dv * dv, ones_bd) * (1.0 / HEAD)
    on = dv * lax.rsqrt(var + EPS_GN) * lxg_ref[...] + lxb_ref[...]
    r = rws_ref[0, :, 0:D_RWKV]
    k = rws_ref[0, :, D_RWKV:2 * D_RWKV]
    v = rws_ref[0, :, 2 * D_RWKV:3 * D_RWKV]
    glo = rws_ref[0, :, 3 * D_RWKV + 4 * R_LO:]
    bonus = _head_sum(r * k * rk_ref[...], ones_bd) * v
    gate = _dot(_sigmoid(glo).astype(BF16), g2_ref[...])
    o2 = (on + bonus) * gate
    mixed = _dot(u_ref[0].astype(BF16), wout_ref[0:D_CONV, :])
    mixed = mixed + _dot(o2.astype(BF16), wout_ref[D_CONV:, :])
    x1_ref[0] = x_ref[0] + g1 * mixed


def _mix(x, ys, rws, u, mods, cond_of_b, p, tile):
    bsz, seq, _ = x.shape
    n_tiles = seq // tile

    def tok(width):
        return pl.BlockSpec((1, tile, width), lambda b, i: (b, i, 0))

    return pl.pallas_call(
        functools.partial(_mix_kernel, n_y=len(ys)),
        grid=(bsz, n_tiles),
        in_specs=[tok(D_MODEL)] + [tok(D_RWKV) for _ in ys] + [
            tok(RW_COLS), tok(D_CONV),
            pl.BlockSpec((1, 6, D_MODEL), lambda b, i: (cond_of_b(b), 0, 0)),
            _const_spec((1, D_RWKV)),
            _const_spec((1, D_RWKV)),
            _const_spec((1, D_RWKV)),
            _const_spec((R_G, D_RWKV)),
            _const_spec((D_MODEL, D_MODEL)),
        ],
        out_specs=tok(D_MODEL),
        out_shape=jax.ShapeDtypeStruct((bsz, seq, D_MODEL), F32),
        compiler_params=pltpu.CompilerParams(
            dimension_semantics=("parallel", "parallel"), vmem_limit_bytes=VMEM_LIMIT),
        name="mix",
    )(x, *ys, rws, u, mods, p['r_k'], p['ln_x_g'], p['ln_x_b'], p['g2'], p['w_out'])


FFN_COLS = 512
FFN_PAD = 8


def _ffn_kernel(*refs, tile, n_tiles, grid_conv, period):
    if grid_conv:
        (x_ref, xp_ref, xn_ref, mods_ref, n2g_ref, up_ref, dw_ref, dwb_ref, down_ref, fg_ref,
         o_ref, vbuf, gbuf, vl, vr, gl, gr, act_s) = refs
    else:
        (x_ref, mods_ref, n2g_ref, up_ref, dw_ref, dwb_ref, down_ref, fg_ref,
         o_ref, vbuf, gbuf, vl, vr, gl, gr, act_s) = refs
    i = pl.program_id(1)
    sh2 = mods_ref[0, 3:4, :]
    sc2 = mods_ref[0, 4:5, :]
    g2 = mods_ref[0, 5:6, :]
    halo = FFN_HALO if grid_conv else 0
    if grid_conv:
        xe = jnp.concatenate([xp_ref[0], x_ref[0], xn_ref[0]], axis=0)
    else:
        xe = x_ref[0]
    rows_e = tile + 2 * halo
    h2 = _rmsnorm(xe, n2g_ref[...]) * (1.0 + sc2) + sh2

    if grid_conv:
        e_idx = lax.broadcasted_iota(jnp.int32, (rows_e, 1), 0)
        pv = jnp.where(i > 0, 1.0, 0.0).astype(F32)
        nv = jnp.where(i < n_tiles - 1, 1.0, 0.0).astype(F32)
        h2 = h2 * (jnp.where(e_idx >= halo, 1.0, pv) * jnp.where(e_idx < halo + tile, 1.0, nv))
        taps = [(dr, dc) for dr in (-1, 0, 1) for dc in (-1, 0, 1)]
    else:
        taps = [(0, -1), (0, 0), (0, 1)]
    masked = period < tile
    if masked:
        colpos = lax.broadcasted_iota(jnp.int32, (tile, 1), 0) % period
        m_left = jnp.where(colpos >= 1, 1.0, 0.0)
        m_right = jnp.where(colpos <= period - 2, 1.0, 0.0)

    zpad = jnp.zeros((FFN_PAD, FFN_COLS), F32)
    vbuf[0:FFN_PAD, :] = zpad
    vbuf[FFN_PAD + rows_e:, :] = zpad
    gbuf[0:FFN_PAD, :] = zpad
    gbuf[FFN_PAD + rows_e:, :] = zpad

    def conv(buf, side_l, side_r, col0):
        ext = tile + 2 * FFN_PAD
        sums = {}
        for ti, (dr, dc) in enumerate(taps):
            win = buf[pl.ds(halo + GRID_W * dr, ext), :]
            term = win * dw_ref[ti:ti + 1, col0:col0 + FFN_COLS]
            sums[dc] = term if dc not in sums else sums[dc] + term
        side_l[...] = sums[-1]
        side_r[...] = sums[1]
        left = side_l[pl.ds(FFN_PAD - 1, tile), :]
        right = side_r[pl.ds(FFN_PAD + 1, tile), :]
        if masked:
            left = left * m_left
            right = right * m_right
        centre = sums[0][FFN_PAD:FFN_PAD + tile]
        return centre + left + right + dwb_ref[:, col0:col0 + FFN_COLS]

    hb = h2.astype(BF16)
    for cc in range(D_FF // FFN_COLS):
        c0 = cc * FFN_COLS
        vbuf[FFN_PAD:FFN_PAD + rows_e, :] = _dot(hb, up_ref[:, c0:c0 + FFN_COLS])
        gbuf[FFN_PAD:FFN_PAD + rows_e, :] = _dot(hb, up_ref[:, D_FF + c0:D_FF + c0 + FFN_COLS])
        val = conv(vbuf, vl, vr, c0)
        gate = conv(gbuf, gl, gr, D_FF + c0)
        act_s[:, c0:c0 + FFN_COLS] = (val * (gate * _sigmoid(gate))).astype(BF16)

    x2 = x_ref[0] + g2 * _dot(act_s[...], down_ref[...])
    o_ref[0] = _rmsnorm(x2, fg_ref[...])


def _ffn(x1, mods, cond_of_b, p, tile, grid_conv, period):
    bsz, seq, _ = x1.shape
    n_tiles = seq // tile
    in_specs = [pl.BlockSpec((1, tile, D_MODEL), lambda b, i: (b, i, 0))]
    args = [x1]
    if grid_conv:
        hb = tile // FFN_HALO
        n_hblk = seq // FFN_HALO
        in_specs += [
            pl.BlockSpec((1, FFN_HALO, D_MODEL), lambda b, i: (b, jnp.maximum(i * hb - 1, 0), 0)),
            pl.BlockSpec((1, FFN_HALO, D_MODEL),
                         lambda b, i: (b, jnp.minimum((i + 1) * hb, n_hblk - 1), 0)),
        ]
        args += [x1, x1]
        dw = p['ffn_dw9']
    else:
        dw = p['ffn_dw3']
    n_taps = dw.shape[0]
    in_specs += [
        pl.BlockSpec((1, 6, D_MODEL), lambda b, i: (cond_of_b(b), 0, 0)),
        _const_spec((1, D_MODEL)),
        _const_spec((D_MODEL, 2 * D_FF)),
        _const_spec((n_taps, 2 * D_FF)),
        _const_spec((1, 2 * D_FF)),
        _const_spec((D_FF, D_MODEL)),
        _const_spec((1, D_MODEL)),
    ]
    args += [mods, p['norm2_g'], p['ffn_up'], dw, p['ffn_dw_b'], p['ffn_down'], p['final_g']]
    halo = FFN_HALO if grid_conv else 0
    buf_rows = tile + 2 * halo + 2 * FFN_PAD
    return pl.pallas_call(
        functools.partial(_ffn_kernel, tile=tile, n_tiles=n_tiles, grid_conv=grid_conv,
                          period=period),
        grid=(bsz, n_tiles),
        in_specs=in_specs,
        out_specs=pl.BlockSpec((1, tile, D_MODEL), lambda b, i: (b, i, 0)),
        out_shape=jax.ShapeDtypeStruct((bsz, seq, D_MODEL), F32),
        scratch_shapes=[pltpu.VMEM((buf_rows, FFN_COLS), F32),
                        pltpu.VMEM((buf_rows, FFN_COLS), F32)]
        + [pltpu.VMEM((tile + 2 * FFN_PAD, FFN_COLS), F32) for _ in range(4)]
        + [pltpu.VMEM((tile, D_FF), BF16)],
        compiler_params=pltpu.CompilerParams(
            dimension_semantics=("parallel", "parallel"), vmem_limit_bytes=VMEM_LIMIT),
        name="ffn_grid" if grid_conv else "ffn",
    )(*args)


def _pad_lowrank(w):
    z = jnp.zeros_like(w[0])
    return jnp.stack([jnp.concatenate([w[0], z], axis=0),
                      jnp.concatenate([z, w[1]], axis=0)]).astype(BF16)


def _pack_state(s):
    bsz = s.shape[0]
    x = s.reshape(bsz, 2, N_GROUPS, HEADS_PER_GROUP, HEAD, HEAD)
    return jnp.swapaxes(x, 3, 4).reshape(bsz, 2, N_GROUPS, HEAD, GROUP)


def _unpack_state(x):
    bsz = x.shape[0]
    s = x.reshape(bsz, 2, N_GROUPS, HEAD, HEADS_PER_GROUP, HEAD)
    return jnp.swapaxes(s, 3, 4).reshape(bsz, 2, N_RHEADS, HEAD, HEAD)


def kernel(x_prompt, x_sample, state_wkv, c, c_ctx, ada_w, ada_b, norm1_g, w_in, shift_k, conv_dw, conv_dw_b, conv_ln_g, conv_ln_b, w0, w2, a0, a2, g2, k_k, k_a, r_k, ln_x_g, ln_x_b, w_out, norm2_g, ffn_up, ffn_dw, ffn_dw_b, ffn_down, final_g):
    assert ada_w.shape[0] == 1, "single layer"
    dec_b = x_sample.shape[0]
    cond8 = jnp.zeros((8, D_MODEL), F32).at[0].set(c_ctx).at[1:1 + dec_b].set(c)
    mods = _adaln(cond8, ada_w[0], ada_b).reshape(8, 6, D_MODEL)

    p = {
        'norm1_g': norm1_g, 'w_in': w_in[0], 'shift_k': shift_k[0],
        'conv_dw': jnp.concatenate([conv_dw[0], jnp.zeros((1, D_CONV), F32)], axis=0),
        'conv_dw_b': conv_dw_b, 'conv_ln_g': conv_ln_g, 'conv_ln_b': conv_ln_b,
        'w0': w0[0], 'w2p': _pad_lowrank(w2[0]), 'a0': a0[0], 'a2p': _pad_lowrank(a2[0]),
        'k_k': k_k, 'k_a': k_a, 'r_k': r_k.reshape(1, D_RWKV),
        'ln_x_g': ln_x_g, 'ln_x_b': ln_x_b, 'g2': g2[0].astype(BF16),
        'w_out': w_out[0].astype(BF16), 'norm2_g': norm2_g,
        'ffn_up': ffn_up[0].astype(BF16), 'ffn_dw9': ffn_dw[0].reshape(9, 2 * D_FF),
        'ffn_dw3': ffn_dw[0, 1], 'ffn_dw_b': ffn_dw_b, 'ffn_down': ffn_down[0].astype(BF16),
        'final_g': final_g.reshape(1, D_MODEL),
    }

    def ctx_cond(b):
        return 0

    def lat_cond(b):
        return b + 1

    seq = x_prompt.shape[1]
    u, rws = _pre(x_prompt, mods, ctx_cond, p, tile=seq, has_halo=False)
    y, sfin = _wkv(rws, None, p, tile=seq, has_sout=True)
    bsz = x_prompt.shape[0]

    def pair(t):
        return t.reshape(bsz // 2, 2 * seq, t.shape[-1])

    x1 = _mix(pair(x_prompt), [pair(y)], pair(rws), pair(u), mods, ctx_cond, p, tile=2 * seq)
    y_prompt = _ffn(x1.reshape(x_prompt.shape), mods, ctx_cond, p, tile=seq, grid_conv=False,
                    period=seq)
    new_state = _unpack_state(sfin)[:, None].astype(state_wkv.dtype)

    s0_bd = _pack_state(state_wkv[:, 0].astype(F32))
    u, rws = _pre(x_sample, mods, lat_cond, p, tile=512, has_halo=True)
    yf, yb = _wkv(rws, s0_bd, p, tile=256, has_sout=False)
    x1 = _mix(x_sample, [yf, yb], rws, u, mods, lat_cond, p, tile=512)
    y_sample = _ffn(x1, mods, lat_cond, p, tile=512, grid_conv=True, period=GRID_W)

    return (y_prompt, y_sample, new_state)
```

```python
import functools
import math

import jax
import jax.numpy as jnp
from jax import lax
from jax.experimental import pallas as pl
from jax.experimental.pallas import tpu as pltpu

F32 = jnp.float32
BF16 = jnp.bfloat16

D_MODEL = 1024
D_CONV = 512
D_RWKV = 512
HEAD = 64
N_RHEADS = 8
CONV_W = 31
R_LO = 64
R_G = 128
D_FF = 2560
RW_COLS = 3 * D_RWKV + 4 * R_LO + R_G
P_IN = 2 * D_CONV + RW_COLS
GRID_W = 64
EPS_RMS = 1e-6
EPS_LN = 1e-5
EPS_GN = 64e-5
L2_EPS = 1e-24
EXP_M05 = math.exp(-0.5)

CHUNK = 64
GROUP = 256
HEADS_PER_GROUP = GROUP // HEAD
N_GROUPS = D_RWKV // GROUP
CONV_HALO = 16
assert CONV_HALO - CONV_W // 2 == 1
FFN_HALO = GRID_W
VMEM_LIMIT = 56 * 1024 * 1024

WKV_UNROLL = 2

NN = (((1,), (0,)), ((), ()))
NT = (((1,), (1,)), ((), ()))


def _dot(a, b, dims=NN):
    return lax.dot_general(a, b, dims, preferred_element_type=F32)


def _split2(x):
    hi = x.astype(BF16)
    lo = (x - hi.astype(F32)).astype(BF16)
    return hi, lo


def _bdot(a, b, dims=NN):
    return _dot(a.astype(BF16), b.astype(BF16), dims)


def _sigmoid(x):
    return jax.nn.sigmoid(x)


def _head_ones():
    r = lax.broadcasted_iota(jnp.int32, (GROUP, GROUP), 0) // HEAD
    c = lax.broadcasted_iota(jnp.int32, (GROUP, GROUP), 1) // HEAD
    return jnp.where(r == c, 1.0, 0.0).astype(BF16)


def _head_sum(x, ones_bd):
    t = x.shape[0]
    pieces = []
    for g in range(N_GROUPS):
        pieces.extend(_split2(x[:, g * GROUP:(g + 1) * GROUP]))
    s = _dot(jnp.concatenate(pieces, axis=0), ones_bd)
    return jnp.concatenate([s[(2 * g) * t:(2 * g + 1) * t] + s[(2 * g + 1) * t:(2 * g + 2) * t]
                            for g in range(N_GROUPS)], axis=1)


def _rmsnorm(x, g):
    ms = jnp.mean(x * x, axis=-1, keepdims=True)
    return x * lax.rsqrt(ms + EPS_RMS) * g


def _adaln_kernel(c_ref, w_ref, b_ref, o_ref):
    c = c_ref[...]
    s = (c * _sigmoid(c)).astype(BF16)
    o_ref[...] = _dot(s, w_ref[...].astype(BF16)) + b_ref[...]


def _adaln(cond8, ada_w, ada_b):
    n_out = ada_w.shape[1]
    bn = 1024
    return pl.pallas_call(
        _adaln_kernel,
        grid=(n_out // bn,),
        in_specs=[
            pl.BlockSpec((8, D_MODEL), lambda j: (0, 0)),
            pl.BlockSpec((D_MODEL, bn), lambda j: (0, j)),
            pl.BlockSpec((1, bn), lambda j: (0, j)),
        ],
        out_specs=pl.BlockSpec((8, bn), lambda j: (0, j)),
        out_shape=jax.ShapeDtypeStruct((8, n_out), F32),
        compiler_params=pltpu.CompilerParams(dimension_semantics=("parallel",)),
        name="adaln",
    )(cond8, ada_w, ada_b)


def _pre_kernel(*refs, tile, n_tiles, has_halo):
    if has_halo:
        (x_ref, xp_ref, xn_ref, mods_ref, n1g_ref, win_ref, sk_ref, cdw_ref, cdb_ref,
         lng_ref, lnb_ref, u_ref, rws_ref, ubuf, rwbuf, pbuf) = refs
    else:
        (x_ref, mods_ref, n1g_ref, win_ref, sk_ref, cdw_ref, cdb_ref,
         lng_ref, lnb_ref, u_ref, rws_ref, ubuf, rwbuf, pbuf) = refs
    i = pl.program_id(1)
    sh1 = mods_ref[0, 0:1, :]
    sc1 = mods_ref[0, 1:2, :]

    if has_halo:
        x = jnp.concatenate([xp_ref[0], x_ref[0], xn_ref[0]], axis=0)
    else:
        x = x_ref[0]
    h = _rmsnorm(x, n1g_ref[...]) * (1.0 + sc1) + sh1
    proj = _dot(h.astype(BF16), win_ref[...])
    cv = proj[:, 0:D_CONV]
    cg = proj[:, D_CONV:2 * D_CONV]
    rw = proj[:, 2 * D_CONV:]
    u0 = cv * _sigmoid(cg)

    if has_halo:
        pv = jnp.where(i > 0, 1.0, 0.0).astype(F32)
        nv = jnp.where(i < n_tiles - 1, 1.0, 0.0).astype(F32)
        ubuf[0:CONV_HALO, :] = u0[0:CONV_HALO] * pv
        ubuf[CONV_HALO:CONV_HALO + tile, :] = u0[CONV_HALO:CONV_HALO + tile]
        ubuf[CONV_HALO + tile:, :] = u0[CONV_HALO + tile:] * nv
        rwbuf[0:8, :] = rw[CONV_HALO - 8:CONV_HALO] * pv
        rwbuf[8:8 + tile, :] = rw[CONV_HALO:CONV_HALO + tile]
        rwbuf[8 + tile:, :] = rw[CONV_HALO + tile:CONV_HALO + tile + 8] * nv
    else:
        ubuf[0:CONV_HALO, :] = jnp.zeros((CONV_HALO, D_CONV), F32)
        ubuf[CONV_HALO:CONV_HALO + tile, :] = u0
        ubuf[CONV_HALO + tile:, :] = jnp.zeros((CONV_HALO, D_CONV), F32)
        rwbuf[0:8, :] = jnp.zeros((8, RW_COLS), F32)
        rwbuf[8:8 + tile, :] = rw
        rwbuf[8 + tile:, :] = jnp.zeros((8, RW_COLS), F32)

    ext = tile + 8
    for s in range(8):
        ps = None
        for m in range(4):
            j = 8 * m + s - 1
            if 0 <= j < CONV_W:
                term = ubuf[pl.ds(8 * m, ext), :] * cdw_ref[j:j + 1, :]
                ps = term if ps is None else ps + term
        pbuf[s] = ps
    acc = pbuf[0, 0:tile, :] + cdb_ref[...]
    for s in range(1, 8):
        acc = acc + pbuf[s, pl.ds(s, tile), :]
    mu = jnp.mean(acc, axis=-1, keepdims=True)
    dv = acc - mu
    var = jnp.mean(dv * dv, axis=-1, keepdims=True)
    ln = dv * lax.rsqrt(var + EPS_LN) * lng_ref[...] + lnb_ref[...]
    u_ref[0] = (ln * _sigmoid(ln)).astype(BF16)

    rws = rwbuf[pl.ds(7, tile), :] * sk_ref[0:1, :]
    rws = rws + rwbuf[pl.ds(8, tile), :] * sk_ref[1:2, :]
    rws = rws + rwbuf[pl.ds(9, tile), :] * sk_ref[2:3, :]
    rws_ref[0] = rws


def _const_spec(shape):
    nd = len(shape)
    return pl.BlockSpec(shape, lambda b, i: (0,) * nd)


def _pre(x, mods, cond_of_b, p, tile, has_halo):
    bsz, seq, _ = x.shape
    n_tiles = seq // tile
    hb = tile // CONV_HALO
    n_hblk = seq // CONV_HALO
    in_specs = [pl.BlockSpec((1, tile, D_MODEL), lambda b, i: (b, i, 0))]
    args = [x]
    if has_halo:
        in_specs += [
            pl.BlockSpec((1, CONV_HALO, D_MODEL), lambda b, i: (b, jnp.maximum(i * hb - 1, 0), 0)),
            pl.BlockSpec((1, CONV_HALO, D_MODEL),
                         lambda b, i: (b, jnp.minimum((i + 1) * hb, n_hblk - 1), 0)),
        ]
        args += [x, x]
    in_specs += [
        pl.BlockSpec((1, 6, D_MODEL), lambda b, i: (cond_of_b(b), 0, 0)),
        _const_spec((1, D_MODEL)),
        _const_spec((D_MODEL, P_IN)),
        _const_spec((3, RW_COLS)),
        _const_spec((32, D_CONV)),
        _const_spec((1, D_CONV)),
        _const_spec((1, D_CONV)),
        _const_spec((1, D_CONV)),
    ]
    args += [mods, p['norm1_g'], p['w_in'], p['shift_k'], p['conv_dw'], p['conv_dw_b'],
             p['conv_ln_g'], p['conv_ln_b']]
    return pl.pallas_call(
        functools.partial(_pre_kernel, tile=tile, n_tiles=n_tiles, has_halo=has_halo),
        grid=(bsz, n_tiles),
        in_specs=in_specs,
        out_specs=[
            pl.BlockSpec((1, tile, D_CONV), lambda b, i: (b, i, 0)),
            pl.BlockSpec((1, tile, RW_COLS), lambda b, i: (b, i, 0)),
        ],
        out_shape=[
            jax.ShapeDtypeStruct((bsz, seq, D_CONV), BF16),
            jax.ShapeDtypeStruct((bsz, seq, RW_COLS), F32),
        ],
        scratch_shapes=[
            pltpu.VMEM((tile + 2 * CONV_HALO, D_CONV), F32),
            pltpu.VMEM((tile + 16, RW_COLS), F32),
            pltpu.VMEM((8, tile + 8, D_CONV), F32),
        ],
        compiler_params=pltpu.CompilerParams(
            dimension_semantics=("parallel", "parallel"), vmem_limit_bytes=VMEM_LIMIT),
        name="pre_halo" if has_halo else "pre",
    )(*args)


def _wkv_kernel(*refs, tile, n_tiles, has_s0, has_sout):
    merged = n_tiles == 1
    refs = list(refs)
    if merged:
        rf_ref = rb_ref = refs[0]
        pos = 1
    else:
        rf_ref, rb_ref = refs[0], refs[1]
        pos = 2
    s0_ref = None
    if has_s0:
        s0_ref = refs[pos]
        pos += 1
    w0_ref, w2_ref, a0_ref, a2_ref, kk_ref, ka_ref = refs[pos:pos + 6]
    pos += 6
    if merged:
        yf_ref = yb_ref = refs[pos]
        pos += 1
    else:
        yf_ref, yb_ref = refs[pos], refs[pos + 1]
        pos += 2
    sout_ref = None
    if has_sout:
        sout_ref = refs[pos]
        pos += 1
    (mst,) = refs[pos:]

    i = pl.program_id(1)
    n_chunks = tile // CHUNK

    lane_head = lax.broadcasted_iota(jnp.int32, (1, GROUP), 1) // HEAD
    head_masks = [jnp.where(lane_head == hh, 1.0, 0.0).astype(BF16)
                  for hh in range(HEADS_PER_GROUP)]
    t_idx = lax.broadcasted_iota(jnp.int32, (CHUNK, GROUP), 0)
    s_idx = lax.broadcasted_iota(jnp.int32, (CHUNK, GROUP), 1) % CHUNK
    eye_a = jnp.where(s_idx == t_idx, 1.0, 0.0).astype(F32)

    def bd(x):
        xb = x.astype(BF16)
        return jnp.concatenate([xb * m for m in head_masks], axis=0)

    def block_transpose(x):
        xt = x.T
        return jnp.concatenate([xt[hh * HEAD:(hh + 1) * HEAD, :] for hh in range(HEADS_PER_GROUP)],
                               axis=1)

    @pl.when(i == 0)
    def _init():
        if has_s0:
            for d in range(2):
                for g in range(N_GROUPS):
                    heads = [s0_ref[0, d, g * HEADS_PER_GROUP + hh] for hh in range(HEADS_PER_GROUP)]
                    mst[d, g] = jnp.concatenate(heads, axis=0).T
        else:
            mst[...] = jnp.zeros(mst.shape, F32)
    if merged:
        yf_ref[...] = jnp.zeros(yf_ref.shape, F32)

    ones_bd = _head_ones()
    slab = WKV_UNROLL * CHUNK
    row = lax.broadcasted_iota(jnp.int32, (slab, slab), 0)
    col = lax.broadcasted_iota(jnp.int32, (slab, slab), 1)
    same_chunk = (row // CHUNK) == (col // CHUNK)
    tri = [jnp.where(same_chunk & (col <= row), 1.0, 0.0).astype(BF16),
           jnp.where(same_chunk & (col >= row), 1.0, 0.0).astype(BF16)]
    strict = [jnp.where(s_idx < t_idx, 1.0, 0.0).astype(F32),
              jnp.where(s_idx > t_idx, 1.0, 0.0).astype(F32)]
    incl = [jnp.where(s_idx <= t_idx, 1.0, 0.0).astype(F32),
            jnp.where(s_idx >= t_idx, 1.0, 0.0).astype(F32)]

    chains = [(d, g, u) for u in range(WKV_UNROLL) for d in range(2) for g in range(N_GROUPS)]

    def loop_step(it, carry):
        slab0 = [pl.multiple_of(it * slab, slab), pl.multiple_of(tile - (it + 1) * slab, slab)]

        def chunk_lo(d, u):
            return (u if d == 0 else WKV_UNROLL - 1 - u) * CHUNK

        rows = {(d, u): pl.ds(pl.multiple_of(slab0[d] + chunk_lo(d, u), CHUNK), CHUNK)
                for d in range(2) for u in range(WKV_UNROLL)}

        kt, q, kh, bh, khp, bhp, v, epc = ({} for _ in range(8))
        for d in range(2):
            src = rf_ref if d == 0 else rb_ref
            srows = pl.ds(slab0[d], slab)
            r_t = src[0, srows, 0:D_RWKV]
            k_t = src[0, srows, D_RWKV:2 * D_RWKV]
            v_t = src[0, srows, 2 * D_RWKV:3 * D_RWKV]
            wlo = src[0, srows, 3 * D_RWKV:3 * D_RWKV + 2 * R_LO]
            alo = src[0, srows, 3 * D_RWKV + 2 * R_LO:3 * D_RWKV + 4 * R_LO]
            kk = k_t * kk_ref[...]
            ss = _head_sum(kk * kk, ones_bd)
            kk = kk * lax.rsqrt(jnp.maximum(ss, L2_EPS))
            w_raw = w0_ref[d:d + 1, :] + _dot(jnp.tanh(wlo).astype(BF16), w2_ref[d])
            logw = -_sigmoid(w_raw) * EXP_M05
            a = _sigmoid(a0_ref[d:d + 1, :] + _dot(alo.astype(BF16), a2_ref[d]))
            kdir = k_t * (1.0 + (a - 1.0) * ka_ref[...])
            bvec = kk * a
            lw_hi, lw_lo = _split2(logw)
            cum = _dot(tri[d], lw_hi) + _dot(tri[d], lw_lo)
            edge = CHUNK - 1 if d == 0 else 0
            tot = jnp.concatenate(
                [jnp.broadcast_to(cum[c * CHUNK + edge:c * CHUNK + edge + 1, :], (CHUNK, D_RWKV))
                 for c in range(WKV_UNROLL)], axis=0)
            e_in = jnp.exp(cum)
            e_neg = jnp.exp(-cum)
            e_tot = jnp.exp(tot)
            e_rem = e_tot * e_neg
            full = {'kt': kk * jnp.exp(cum - logw), 'q': r_t * e_in, 'kh': kdir * e_neg,
                    'bh': bvec * e_neg, 'khp': kdir * e_rem, 'bhp': bvec * e_rem, 'v': v_t,
                    'epc': e_tot}
            for g in range(N_GROUPS):
                lanes = slice(g * GROUP, (g + 1) * GROUP)
                for u in range(WKV_UNROLL):
                    lo = chunk_lo(d, u)
                    ch = (d, g, u)
                    for dst, name in ((kt, 'kt'), (q, 'q'), (kh, 'kh'), (bh, 'bh'), (khp, 'khp'),
                                      (bhp, 'bhp'), (v, 'v')):
                        dst[ch] = full[name][lo:lo + CHUNK, lanes]
                    epc[ch] = full['epc'][lo:lo + 1, lanes]

        def stack(*xs):
            return jnp.concatenate([x.astype(BF16) for x in xs], axis=0)

        lhs2 = {ch: stack(kt[ch], q[ch]) for ch in chains}
        a1 = {ch: _dot(lhs2[ch], bd(kh[ch]), NT) for ch in chains}
        a2 = {ch: _dot(lhs2[ch], bd(bh[ch]), NT) for ch in chains}
        a_ak = {ch: a1[ch][0:CHUNK] * strict[ch[0]] for ch in chains}
        a_qk = {ch: a1[ch][CHUNK:] * incl[ch[0]] for ch in chains}
        a_ab = {ch: a2[ch][0:CHUNK] * strict[ch[0]] for ch in chains}
        a_qb = {ch: a2[ch][CHUNK:] * incl[ch[0]] for ch in chains}

        khp_t = {ch: block_transpose(khp[ch]) for ch in chains}
        bhp_t = {ch: block_transpose(bhp[ch]) for ch in chains}

        t_inv = {ch: eye_a - a_ab[ch] for ch in chains}
        n_pow = {ch: _bdot(a_ab[ch], bd(a_ab[ch])) for ch in chains}
        for _ in range(4):
            both = {ch: _dot(stack(t_inv[ch], n_pow[ch]), bd(n_pow[ch])) for ch in chains}
            t_inv = {ch: t_inv[ch] + both[ch][0:CHUNK] for ch in chains}
            n_pow = {ch: both[ch][CHUNK:] for ch in chains}
        t_inv = {ch: t_inv[ch] + _bdot(t_inv[ch], bd(n_pow[ch])) for ch in chains}

        on_v = {ch: _dot(stack(a_ak[ch], a_qk[ch], khp_t[ch]), bd(v[ch])) for ch in chains}
        av = {ch: on_v[ch][0:CHUNK] for ch in chains}
        lhs_qb = {ch: _dot(stack(a_qb[ch], bhp_t[ch]), bd(t_inv[ch])).astype(BF16) for ch in chains}
        on_ktp = {ch: _dot(lhs_qb[ch], bd(kt[ch])) for ch in chains}
        on_vp = {ch: _dot(lhs_qb[ch], bd(av[ch])) for ch in chains}
        qp = {ch: q[ch] - on_ktp[ch][0:CHUNK] for ch in chains}
        yp = {ch: on_v[ch][CHUNK:2 * CHUNK] - on_vp[ch][0:CHUNK] for ch in chains}
        gp = {ch: eye_a * epc[ch] - on_ktp[ch][CHUNK:] for ch in chains}
        hp = {ch: on_v[ch][2 * CHUNK:] - on_vp[ch][CHUNK:] for ch in chains}

        for d in range(2):
            y_ref = yf_ref if d == 0 else yb_ref
            for g in range(N_GROUPS):
                m = mst[d, g]
                for u in range(WKV_UNROLL):
                    ch = (d, g, u)
                    on_m = _dot(stack(qp[ch], gp[ch]), bd(m))
                    y = on_m[0:CHUNK] + yp[ch]
                    lanes = slice(g * GROUP, (g + 1) * GROUP)
                    if merged:
                        y_ref[0, rows[(d, u)], lanes] += y
                    else:
                        y_ref[0, rows[(d, u)], lanes] = y
                    m = on_m[CHUNK:] + hp[ch]
                mst[d, g] = m
        return carry

    lax.fori_loop(0, n_chunks // WKV_UNROLL, loop_step, 0)

    if has_sout:
        @pl.when(i == n_tiles - 1)
        def _fin():
            for d in range(2):
                for g in range(N_GROUPS):
                    st = mst[d, g].T
                    for hh in range(HEADS_PER_GROUP):
                        sout_ref[0, d, g * HEADS_PER_GROUP + hh] = st[hh * HEAD:(hh + 1) * HEAD, :]


def _wkv(rws, s0_bd, p, tile, has_sout):
    bsz, seq, _ = rws.shape
    n_tiles = seq // tile
    has_s0 = s0_bd is not None
    merged = n_tiles == 1
    in_specs = [pl.BlockSpec((1, tile, RW_COLS), lambda b, i: (b, i, 0))]
    args = [rws]
    if not merged:
        in_specs.append(pl.BlockSpec((1, tile, RW_COLS), lambda b, i: (b, n_tiles - 1 - i, 0)))
        args.append(rws)
    if has_s0:
        in_specs.append(pl.BlockSpec((1, 2, N_RHEADS, HEAD, HEAD), lambda b, i: (b, 0, 0, 0, 0)))
        args.append(s0_bd)
    in_specs += [
        _const_spec((2, D_RWKV)),
        _const_spec((2, 2 * R_LO, D_RWKV)),
        _const_spec((2, D_RWKV)),
        _const_spec((2, 2 * R_LO, D_RWKV)),
        _const_spec((1, D_RWKV)),
        _const_spec((1, D_RWKV)),
    ]
    args += [p['w0'], p['w2p'], p['a0'], p['a2p'], p['k_k'], p['k_a']]
    out_specs = [pl.BlockSpec((1, tile, D_RWKV), lambda b, i: (b, i, 0))]
    out_shape = [jax.ShapeDtypeStruct((bsz, seq, D_RWKV), F32)]
    if not merged:
        out_specs.append(pl.BlockSpec((1, tile, D_RWKV), lambda b, i: (b, n_tiles - 1 - i, 0)))
        out_shape.append(jax.ShapeDtypeStruct((bsz, seq, D_RWKV), F32))
    if has_sout:
        out_specs.append(pl.BlockSpec((1, 2, N_RHEADS, HEAD, HEAD), lambda b, i: (b, 0, 0, 0, 0)))
        out_shape.append(jax.ShapeDtypeStruct((bsz, 2, N_RHEADS, HEAD, HEAD), F32))
    return pl.pallas_call(
        functools.partial(_wkv_kernel, tile=tile, n_tiles=n_tiles, has_s0=has_s0,
                          has_sout=has_sout),
        grid=(bsz, n_tiles),
        in_specs=in_specs,
        out_specs=out_specs,
        out_shape=out_shape,
        scratch_shapes=[pltpu.VMEM((2, N_GROUPS, HEAD, GROUP), F32)],
        compiler_params=pltpu.CompilerParams(
            dimension_semantics=("arbitrary", "arbitrary"), vmem_limit_bytes=VMEM_LIMIT),
        name="wkv_s0" if has_s0 else "wkv",
    )(*args)


def _mix_kernel(*refs, n_y):
    x_ref = refs[0]
    y_refs = refs[1:1 + n_y]
    (rws_ref, u_ref, mods_ref, rk_ref, lxg_ref, lxb_ref, g2_ref, wout_ref,
     x1_ref) = refs[1 + n_y:]
    ones_bd = _head_ones()
    g1 = mods_ref[0, 2:3, :]
    o = y_refs[0][0]
    for y_ref in y_refs[1:]:
        o = o + y_ref[0]
    mu = _head_sum(o, ones_bd) * (1.0 / HEAD)
    dv = o - mu
    var = _head_sum(dv * dv, ones_bd) * (1.0 / HEAD)
    on = dv * lax.rsqrt(var + EPS_GN) * lxg_ref[...] + lxb_ref[...]
    r = rws_ref[0, :, 0:D_RWKV]
    k = rws_ref[0, :, D_RWKV:2 * D_RWKV]
    v = rws_ref[0, :, 2 * D_RWKV:3 * D_RWKV]
    glo = rws_ref[0, :, 3 * D_RWKV + 4 * R_LO:]
    bonus = _head_sum(r * k * rk_ref[...], ones_bd) * v
    gate = _dot(_sigmoid(glo).astype(BF16), g2_ref[...])
    o2 = (on + bonus) * gate
    mixed = _dot(u_ref[0].astype(BF16), wout_ref[0:D_CONV, :])
    mixed = mixed + _dot(o2.astype(BF16), wout_ref[D_CONV:, :])
    x1_ref[0] = x_ref[0] + g1 * mixed


def _mix(x, ys, rws, u, mods, cond_of_b, p, tile):
    bsz, seq, _ = x.shape
    n_tiles = seq // tile

    def tok(width):
        return pl.BlockSpec((1, tile, width), lambda b, i: (b, i, 0))

    return pl.pallas_call(
        functools.partial(_mix_kernel, n_y=len(ys)),
        grid=(bsz, n_tiles),
        in_specs=[tok(D_MODEL)] + [tok(D_RWKV) for _ in ys] + [
            tok(RW_COLS), tok(D_CONV),
            pl.BlockSpec((1, 6, D_MODEL), lambda b, i: (cond_of_b(b), 0, 0)),
            _const_spec((1, D_RWKV)),
            _const_spec((1, D_RWKV)),
            _const_spec((1, D_RWKV)),
            _const_spec((R_G, D_RWKV)),
            _const_spec((D_MODEL, D_MODEL)),
        ],
        out_specs=tok(D_MODEL),
        out_shape=jax.ShapeDtypeStruct((bsz, seq, D_MODEL), F32),
        compiler_params=pltpu.CompilerParams(
            dimension_semantics=("parallel", "parallel"), vmem_limit_bytes=VMEM_LIMIT),
        name="mix",
    )(x, *ys, rws, u, mods, p['r_k'], p['ln_x_g'], p['ln_x_b'], p['g2'], p['w_out'])


FFN_COLS = 256
FFN_PAD = 8


def _ffn_kernel(*refs, tile, n_tiles, grid_conv, period):
    if grid_conv:
        (x_ref, xp_ref, xn_ref, mods_ref, n2g_ref, up_ref, dw_ref, dwb_ref, down_ref, fg_ref,
         o_ref, vbuf, gbuf, vl, vr, gl, gr, act_s) = refs
    else:
        (x_ref, mods_ref, n2g_ref, up_ref, dw_ref, dwb_ref, down_ref, fg_ref,
         o_ref, vbuf, gbuf, vl, vr, gl, gr, act_s) = refs
    i = pl.program_id(1)
    sh2 = mods_ref[0, 3:4, :]
    sc2 = mods_ref[0, 4:5, :]
    g2 = mods_ref[0, 5:6, :]
    halo = FFN_HALO if grid_conv else 0
    if grid_conv:
        xe = jnp.concatenate([xp_ref[0], x_ref[0], xn_ref[0]], axis=0)
    else:
        xe = x_ref[0]
    rows_e = tile + 2 * halo
    h2 = _rmsnorm(xe, n2g_ref[...]) * (1.0 + sc2) + sh2

    if grid_conv:
        e_idx = lax.broadcasted_iota(jnp.int32, (rows_e, 1), 0)
        pv = jnp.where(i > 0, 1.0, 0.0).astype(F32)
        nv = jnp.where(i < n_tiles - 1, 1.0, 0.0).astype(F32)
        h2 = h2 * (jnp.where(e_idx >= halo, 1.0, pv) * jnp.where(e_idx < halo + tile, 1.0, nv))
        taps = [(dr, dc) for dr in (-1, 0, 1) for dc in (-1, 0, 1)]
    else:
        taps = [(0, -1), (0, 0), (0, 1)]
    masked = period < tile
    if masked:
        colpos = lax.broadcasted_iota(jnp.int32, (tile, 1), 0) % period
        m_left = jnp.where(colpos >= 1, 1.0, 0.0)
        m_right = jnp.where(colpos <= period - 2, 1.0, 0.0)

    zpad = jnp.zeros((FFN_PAD, FFN_COLS), F32)
    vbuf[0:FFN_PAD, :] = zpad
    vbuf[FFN_PAD + rows_e:, :] = zpad
    gbuf[0:FFN_PAD, :] = zpad
    gbuf[FFN_PAD + rows_e:, :] = zpad

    def conv(buf, side_l, side_r, col0):
        ext = tile + 2 * FFN_PAD
        sums = {}
        for ti, (dr, dc) in enumerate(taps):
            win = buf[pl.ds(halo + GRID_W * dr, ext), :]
            term = win * dw_ref[ti:ti + 1, col0:col0 + FFN_COLS]
            sums[dc] = term if dc not in sums else sums[dc] + term
        side_l[...] = sums[-1]
        side_r[...] = sums[1]
        left = side_l[pl.ds(FFN_PAD - 1, tile), :]
        right = side_r[pl.ds(FFN_PAD + 1, tile), :]
        if masked:
            left = left * m_left
            right = right * m_right
        centre = sums[0][FFN_PAD:FFN_PAD + tile]
        return centre + left + right + dwb_ref[:, col0:col0 + FFN_COLS]

    hb = h2.astype(BF16)
    for cc in range(D_FF // FFN_COLS):
        c0 = cc * FFN_COLS
        vbuf[FFN_PAD:FFN_PAD + rows_e, :] = _dot(hb, up_ref[:, c0:c0 + FFN_COLS])
        gbuf[FFN_PAD:FFN_PAD + rows_e, :] = _dot(hb, up_ref[:, D_FF + c0:D_FF + c0 + FFN_COLS])
        val = conv(vbuf, vl, vr, c0)
        gate = conv(gbuf, gl, gr, D_FF + c0)
        act_s[:, c0:c0 + FFN_COLS] = (val * (gate * _sigmoid(gate))).astype(BF16)

    x2 = x_ref[0] + g2 * _dot(act_s[...], down_ref[...])
    o_ref[0] = _rmsnorm(x2, fg_ref[...])


def _ffn(x1, mods, cond_of_b, p, tile, grid_conv, period):
    bsz, seq, _ = x1.shape
    n_tiles = seq // tile
    in_specs = [pl.BlockSpec((1, tile, D_MODEL), lambda b, i: (b, i, 0))]
    args = [x1]
    if grid_conv:
        hb = tile // FFN_HALO
        n_hblk = seq // FFN_HALO
        in_specs += [
            pl.BlockSpec((1, FFN_HALO, D_MODEL), lambda b, i: (b, jnp.maximum(i * hb - 1, 0), 0)),
            pl.BlockSpec((1, FFN_HALO, D_MODEL),
                         lambda b, i: (b, jnp.minimum((i + 1) * hb, n_hblk - 1), 0)),
        ]
        args += [x1, x1]
        dw = p['ffn_dw9']
    else:
        dw = p['ffn_dw3']
    n_taps = dw.shape[0]
    in_specs += [
        pl.BlockSpec((1, 6, D_MODEL), lambda b, i: (cond_of_b(b), 0, 0)),
        _const_spec((1, D_MODEL)),
        _const_spec((D_MODEL, 2 * D_FF)),
        _const_spec((n_taps, 2 * D_FF)),
        _const_spec((1, 2 * D_FF)),
        _const_spec((D_FF, D_MODEL)),
        _const_spec((1, D_MODEL)),
    ]
    args += [mods, p['norm2_g'], p['ffn_up'], dw, p['ffn_dw_b'], p['ffn_down'], p['final_g']]
    halo = FFN_HALO if grid_conv else 0
    buf_rows = tile + 2 * halo + 2 * FFN_PAD
    return pl.pallas_call(
        functools.partial(_ffn_kernel, tile=tile, n_tiles=n_tiles, grid_conv=grid_conv,
                          period=period),
        grid=(bsz, n_tiles),
        in_specs=in_specs,
        out_specs=pl.BlockSpec((1, tile, D_MODEL), lambda b, i: (b, i, 0)),
        out_shape=jax.ShapeDtypeStruct((bsz, seq, D_MODEL), F32),
        scratch_shapes=[pltpu.VMEM((buf_rows, FFN_COLS), F32),
                        pltpu.VMEM((buf_rows, FFN_COLS), F32)]
        + [pltpu.VMEM((tile + 2 * FFN_PAD, FFN_COLS), F32) for _ in range(4)]
        + [pltpu.VMEM((tile, D_FF), BF16)],
        compiler_params=pltpu.CompilerParams(
            dimension_semantics=("parallel", "parallel"), vmem_limit_bytes=VMEM_LIMIT),
        name="ffn_grid" if grid_conv else "ffn",
    )(*args)


def _pad_lowrank(w):
    z = jnp.zeros_like(w[0])
    return jnp.stack([jnp.concatenate([w[0], z], axis=0),
                      jnp.concatenate([z, w[1]], axis=0)]).astype(BF16)


def kernel(x_prompt, x_sample, state_wkv, c, c_ctx, ada_w, ada_b, norm1_g, w_in, shift_k, conv_dw, conv_dw_b, conv_ln_g, conv_ln_b, w0, w2, a0, a2, g2, k_k, k_a, r_k, ln_x_g, ln_x_b, w_out, norm2_g, ffn_up, ffn_dw, ffn_dw_b, ffn_down, final_g):
    assert ada_w.shape[0] == 1, "single layer"
    dec_b = x_sample.shape[0]
    cond8 = jnp.zeros((8, D_MODEL), F32).at[0].set(c_ctx).at[1:1 + dec_b].set(c)
    mods = _adaln(cond8, ada_w[0], ada_b).reshape(8, 6, D_MODEL)

    p = {
        'norm1_g': norm1_g, 'w_in': w_in[0].astype(BF16), 'shift_k': shift_k[0],
        'conv_dw': jnp.concatenate([conv_dw[0], jnp.zeros((1, D_CONV), F32)], axis=0),
        'conv_dw_b': conv_dw_b, 'conv_ln_g': conv_ln_g, 'conv_ln_b': conv_ln_b,
        'w0': w0[0], 'w2p': _pad_lowrank(w2[0]), 'a0': a0[0], 'a2p': _pad_lowrank(a2[0]),
        'k_k': k_k, 'k_a': k_a, 'r_k': r_k.reshape(1, D_RWKV),
        'ln_x_g': ln_x_g, 'ln_x_b': ln_x_b, 'g2': g2[0].astype(BF16),
        'w_out': w_out[0].astype(BF16), 'norm2_g': norm2_g,
        'ffn_up': ffn_up[0].astype(BF16), 'ffn_dw9': ffn_dw[0].reshape(9, 2 * D_FF),
        'ffn_dw3': ffn_dw[0, 1], 'ffn_dw_b': ffn_dw_b, 'ffn_down': ffn_down[0].astype(BF16),
        'final_g': final_g.reshape(1, D_MODEL),
    }

    def ctx_cond(b):
        return 0

    def lat_cond(b):
        return b + 1

    seq = x_prompt.shape[1]
    u, rws = _pre(x_prompt, mods, ctx_cond, p, tile=seq, has_halo=False)
    y, sfin = _wkv(rws, None, p, tile=seq, has_sout=True)
    bsz = x_prompt.shape[0]

    def pair(t):
        return t.reshape(bsz // 2, 2 * seq, t.shape[-1])

    x1 = _mix(pair(x_prompt), [pair(y)], pair(rws), pair(u), mods, ctx_cond, p, tile=2 * seq)
    y_prompt = _ffn(x1.reshape(x_prompt.shape), mods, ctx_cond, p, tile=seq, grid_conv=False,
                    period=seq)
    new_state = sfin[:, None].astype(state_wkv.dtype)

    s0_bd = state_wkv[:, 0].astype(F32)
    u, rws = _pre(x_sample, mods, lat_cond, p, tile=512, has_halo=True)
    yf, yb = _wkv(rws, s0_bd, p, tile=256, has_sout=False)
    x1 = _mix(x_sample, [yf, yb], rws, u, mods, lat_cond, p, tile=512)
    y_sample = _ffn(x1, mods, lat_cond, p, tile=512, grid_conv=True, period=GRID_W)

    return (y_prompt, y_sample, new_state)
```

```python
import functools
import math

import jax
import jax.numpy as jnp
from jax import lax
from jax.experimental import pallas as pl
from jax.experimental.pallas import tpu as pltpu

F32 = jnp.float32
BF16 = jnp.bfloat16

D_MODEL = 1024
D_CONV = 512
D_RWKV = 512
HEAD = 64
N_RHEADS = 8
CONV_W = 31
R_LO = 64
R_G = 128
D_FF = 2560
RW_COLS = 3 * D_RWKV + 4 * R_LO + R_G
P_IN = 2 * D_CONV + RW_COLS
GRID_W = 64
EPS_RMS = 1e-6
EPS_LN = 1e-5
EPS_GN = 64e-5
L2_EPS = 1e-24
EXP_M05 = math.exp(-0.5)

CHUNK = 64
GROUP = 256
HEADS_PER_GROUP = GROUP // HEAD
N_GROUPS = D_RWKV // GROUP
CONV_HALO = 16
assert CONV_HALO - CONV_W // 2 == 1
FFN_HALO = GRID_W
VMEM_LIMIT = 56 * 1024 * 1024

WKV_UNROLL = 2

NN = (((1,), (0,)), ((), ()))
NT = (((1,), (1,)), ((), ()))


def _dot(a, b, dims=NN):
    return lax.dot_general(a, b, dims, preferred_element_type=F32)


def _split2(x):
    hi = x.astype(BF16)
    lo = (x - hi.astype(F32)).astype(BF16)
    return hi, lo


def _bdot(a, b, dims=NN):
    return _dot(a.astype(BF16), b.astype(BF16), dims)


def _sigmoid(x):
    return jax.nn.sigmoid(x)


def _head_ones():
    r = lax.broadcasted_iota(jnp.int32, (GROUP, GROUP), 0) // HEAD
    c = lax.broadcasted_iota(jnp.int32, (GROUP, GROUP), 1) // HEAD
    return jnp.where(r == c, 1.0, 0.0).astype(BF16)


def _head_sum(x, ones_bd):
    t = x.shape[0]
    pieces = []
    for g in range(N_GROUPS):
        pieces.extend(_split2(x[:, g * GROUP:(g + 1) * GROUP]))
    s = _dot(jnp.concatenate(pieces, axis=0), ones_bd)
    return jnp.concatenate([s[(2 * g) * t:(2 * g + 1) * t] + s[(2 * g + 1) * t:(2 * g + 2) * t]
                            for g in range(N_GROUPS)], axis=1)


def _rmsnorm(x, g):
    ms = jnp.mean(x * x, axis=-1, keepdims=True)
    return x * lax.rsqrt(ms + EPS_RMS) * g


def _adaln_kernel(c_ref, w_ref, b_ref, o_ref):
    c = c_ref[...]
    s = (c * _sigmoid(c)).astype(BF16)
    o_ref[...] = _dot(s, w_ref[...].astype(BF16)) + b_ref[...]


def _adaln(cond8, ada_w, ada_b):
    n_out = ada_w.shape[1]
    bn = 1024
    return pl.pallas_call(
        _adaln_kernel,
        grid=(n_out // bn,),
        in_specs=[
            pl.BlockSpec((8, D_MODEL), lambda j: (0, 0)),
            pl.BlockSpec((D_MODEL, bn), lambda j: (0, j)),
            pl.BlockSpec((1, bn), lambda j: (0, j)),
        ],
        out_specs=pl.BlockSpec((8, bn), lambda j: (0, j)),
        out_shape=jax.ShapeDtypeStruct((8, n_out), F32),
        compiler_params=pltpu.CompilerParams(dimension_semantics=("parallel",)),
        name="adaln",
    )(cond8, ada_w, ada_b)


def _pre_kernel(*refs, tile, n_tiles, has_halo):
    if has_halo:
        (x_ref, xp_ref, xn_ref, mods_ref, n1g_ref, win_ref, sk_ref, cdw_ref, cdb_ref,
         lng_ref, lnb_ref, u_ref, rws_ref, ubuf, rwbuf, pbuf) = refs
    else:
        (x_ref, mods_ref, n1g_ref, win_ref, sk_ref, cdw_ref, cdb_ref,
         lng_ref, lnb_ref, u_ref, rws_ref, ubuf, rwbuf, pbuf) = refs
    i = pl.program_id(1)
    sh1 = mods_ref[0, 0:1, :]
    sc1 = mods_ref[0, 1:2, :]

    if has_halo:
        x = jnp.concatenate([xp_ref[0], x_ref[0], xn_ref[0]], axis=0)
    else:
        x = x_ref[0]
    h = _rmsnorm(x, n1g_ref[...]) * (1.0 + sc1) + sh1
    proj = _dot(h.astype(BF16), win_ref[...])
    cv = proj[:, 0:D_CONV]
    cg = proj[:, D_CONV:2 * D_CONV]
    rw = proj[:, 2 * D_CONV:]
    u0 = cv * _sigmoid(cg)

    if has_halo:
        pv = jnp.where(i > 0, 1.0, 0.0).astype(F32)
        nv = jnp.where(i < n_tiles - 1, 1.0, 0.0).astype(F32)
        ubuf[0:CONV_HALO, :] = u0[0:CONV_HALO] * pv
        ubuf[CONV_HALO:CONV_HALO + tile, :] = u0[CONV_HALO:CONV_HALO + tile]
        ubuf[CONV_HALO + tile:, :] = u0[CONV_HALO + tile:] * nv
        rwbuf[0:8, :] = rw[CONV_HALO - 8:CONV_HALO] * pv
        rwbuf[8:8 + tile, :] = rw[CONV_HALO:CONV_HALO + tile]
        rwbuf[8 + tile:, :] = rw[CONV_HALO + tile:CONV_HALO + tile + 8] * nv
    else:
        ubuf[0:CONV_HALO, :] = jnp.zeros((CONV_HALO, D_CONV), F32)
        ubuf[CONV_HALO:CONV_HALO + tile, :] = u0
        ubuf[CONV_HALO + tile:, :] = jnp.zeros((CONV_HALO, D_CONV), F32)
        rwbuf[0:8, :] = jnp.zeros((8, RW_COLS), F32)
        rwbuf[8:8 + tile, :] = rw
        rwbuf[8 + tile:, :] = jnp.zeros((8, RW_COLS), F32)

    ext = tile + 8
    for s in range(8):
        ps = None
        for m in range(4):
            j = 8 * m + s - 1
            if 0 <= j < CONV_W:
                term = ubuf[pl.ds(8 * m, ext), :] * cdw_ref[j:j + 1, :]
                ps = term if ps is None else ps + term
        pbuf[s] = ps
    acc = pbuf[0, 0:tile, :] + cdb_ref[...]
    for s in range(1, 8):
        acc = acc + pbuf[s, pl.ds(s, tile), :]
    mu = jnp.mean(acc, axis=-1, keepdims=True)
    dv = acc - mu
    var = jnp.mean(dv * dv, axis=-1, keepdims=True)
    ln = dv * lax.rsqrt(var + EPS_LN) * lng_ref[...] + lnb_ref[...]
    u_ref[0] = (ln * _sigmoid(ln)).astype(BF16)

    rws = rwbuf[pl.ds(7, tile), :] * sk_ref[0:1, :]
    rws = rws + rwbuf[pl.ds(8, tile), :] * sk_ref[1:2, :]
    rws = rws + rwbuf[pl.ds(9, tile), :] * sk_ref[2:3, :]
    rws_ref[0] = rws


def _const_spec(shape):
    nd = len(shape)
    return pl.BlockSpec(shape, lambda b, i: (0,) * nd)


def _pre(x, mods, cond_of_b, p, tile, has_halo):
    bsz, seq, _ = x.shape
    n_tiles = seq // tile
    hb = tile // CONV_HALO
    n_hblk = seq // CONV_HALO
    in_specs = [pl.BlockSpec((1, tile, D_MODEL), lambda b, i: (b, i, 0))]
    args = [x]
    if has_halo:
        in_specs += [
            pl.BlockSpec((1, CONV_HALO, D_MODEL), lambda b, i: (b, jnp.maximum(i * hb - 1, 0), 0)),
            pl.BlockSpec((1, CONV_HALO, D_MODEL),
                         lambda b, i: (b, jnp.minimum((i + 1) * hb, n_hblk - 1), 0)),
        ]
        args += [x, x]
    in_specs += [
        pl.BlockSpec((1, 6, D_MODEL), lambda b, i: (cond_of_b(b), 0, 0)),
        _const_spec((1, D_MODEL)),
        _const_spec((D_MODEL, P_IN)),
        _const_spec((3, RW_COLS)),
        _const_spec((32, D_CONV)),
        _const_spec((1, D_CONV)),
        _const_spec((1, D_CONV)),
        _const_spec((1, D_CONV)),
    ]
    args += [mods, p['norm1_g'], p['w_in'], p['shift_k'], p['conv_dw'], p['conv_dw_b'],
             p['conv_ln_g'], p['conv_ln_b']]
    return pl.pallas_call(
        functools.partial(_pre_kernel, tile=tile, n_tiles=n_tiles, has_halo=has_halo),
        grid=(bsz, n_tiles),
        in_specs=in_specs,
        out_specs=[
            pl.BlockSpec((1, tile, D_CONV), lambda b, i: (b, i, 0)),
            pl.BlockSpec((1, tile, RW_COLS), lambda b, i: (b, i, 0)),
        ],
        out_shape=[
            jax.ShapeDtypeStruct((bsz, seq, D_CONV), BF16),
            jax.ShapeDtypeStruct((bsz, seq, RW_COLS), F32),
        ],
        scratch_shapes=[
            pltpu.VMEM((tile + 2 * CONV_HALO, D_CONV), F32),
            pltpu.VMEM((tile + 16, RW_COLS), F32),
            pltpu.VMEM((8, tile + 8, D_CONV), F32),
        ],
        compiler_params=pltpu.CompilerParams(
            dimension_semantics=("parallel", "parallel"), vmem_limit_bytes=VMEM_LIMIT),
        name="pre_halo" if has_halo else "pre",
    )(*args)


def _wkv_kernel(*refs, tile, n_tiles, has_s0, has_sout):
    merged = n_tiles == 1
    refs = list(refs)
    if merged:
        rf_ref = rb_ref = refs[0]
        pos = 1
    else:
        rf_ref, rb_ref = refs[0], refs[1]
        pos = 2
    s0_ref = None
    if has_s0:
        s0_ref = refs[pos]
        pos += 1
    w0_ref, w2_ref, a0_ref, a2_ref, kk_ref, ka_ref = refs[pos:pos + 6]
    pos += 6
    y_ref = refs[pos]
    pos += 1
    sout_ref = None
    if has_sout:
        sout_ref = refs[pos]
        pos += 1
    (mst,) = refs[pos:]

    i = pl.program_id(1)
    n_chunks = tile // CHUNK

    lane_head = lax.broadcasted_iota(jnp.int32, (1, GROUP), 1) // HEAD
    head_masks = [jnp.where(lane_head == hh, 1.0, 0.0).astype(BF16)
                  for hh in range(HEADS_PER_GROUP)]
    t_idx = lax.broadcasted_iota(jnp.int32, (CHUNK, GROUP), 0)
    s_idx = lax.broadcasted_iota(jnp.int32, (CHUNK, GROUP), 1) % CHUNK
    eye_a = jnp.where(s_idx == t_idx, 1.0, 0.0).astype(F32)

    def bd(x):
        xb = x.astype(BF16)
        return jnp.concatenate([xb * m for m in head_masks], axis=0)

    def block_transpose(x):
        xt = x.T
        return jnp.concatenate([xt[hh * HEAD:(hh + 1) * HEAD, :] for hh in range(HEADS_PER_GROUP)],
                               axis=1)

    @pl.when(i == 0)
    def _init():
        if has_s0:
            for d in range(2):
                for g in range(N_GROUPS):
                    heads = [s0_ref[0, d, g * HEADS_PER_GROUP + hh] for hh in range(HEADS_PER_GROUP)]
                    mst[d, g] = jnp.concatenate(heads, axis=0).T
        else:
            mst[...] = jnp.zeros(mst.shape, F32)
        y_ref[...] = jnp.zeros(y_ref.shape, F32)

    ones_bd = _head_ones()
    slab = WKV_UNROLL * CHUNK
    row = lax.broadcasted_iota(jnp.int32, (slab, slab), 0)
    col = lax.broadcasted_iota(jnp.int32, (slab, slab), 1)
    same_chunk = (row // CHUNK) == (col // CHUNK)
    tri = [jnp.where(same_chunk & (col <= row), 1.0, 0.0).astype(BF16),
           jnp.where(same_chunk & (col >= row), 1.0, 0.0).astype(BF16)]
    strict = [jnp.where(s_idx < t_idx, 1.0, 0.0).astype(F32),
              jnp.where(s_idx > t_idx, 1.0, 0.0).astype(F32)]
    incl = [jnp.where(s_idx <= t_idx, 1.0, 0.0).astype(F32),
            jnp.where(s_idx >= t_idx, 1.0, 0.0).astype(F32)]

    chains = [(d, g, u) for u in range(WKV_UNROLL) for d in range(2) for g in range(N_GROUPS)]

    def loop_step(it, carry):
        slab0 = [pl.multiple_of(it * slab, slab), pl.multiple_of(tile - (it + 1) * slab, slab)]

        def chunk_lo(d, u):
            return (u if d == 0 else WKV_UNROLL - 1 - u) * CHUNK

        kt, q, kh, bh, khp, bhp, v, epc = ({} for _ in range(8))
        for d in range(2):
            src = rf_ref if d == 0 else rb_ref
            srows = pl.ds(slab0[d], slab)
            r_t = src[0, srows, 0:D_RWKV]
            k_t = src[0, srows, D_RWKV:2 * D_RWKV]
            v_t = src[0, srows, 2 * D_RWKV:3 * D_RWKV]
            wlo = src[0, srows, 3 * D_RWKV:3 * D_RWKV + 2 * R_LO]
            alo = src[0, srows, 3 * D_RWKV + 2 * R_LO:3 * D_RWKV + 4 * R_LO]
            kk = k_t * kk_ref[...]
            ss = _head_sum(kk * kk, ones_bd)
            kk = kk * lax.rsqrt(jnp.maximum(ss, L2_EPS))
            w_raw = w0_ref[d:d + 1, :] + _dot(jnp.tanh(wlo).astype(BF16), w2_ref[d])
            logw = -_sigmoid(w_raw) * EXP_M05
            a = _sigmoid(a0_ref[d:d + 1, :] + _dot(alo.astype(BF16), a2_ref[d]))
            kdir = k_t * (1.0 + (a - 1.0) * ka_ref[...])
            bvec = kk * a
            lw_hi, lw_lo = _split2(logw)
            cum = _dot(tri[d], lw_hi) + _dot(tri[d], lw_lo)
            edge = CHUNK - 1 if d == 0 else 0
            tot = jnp.concatenate(
                [jnp.broadcast_to(cum[c * CHUNK + edge:c * CHUNK + edge + 1, :], (CHUNK, D_RWKV))
                 for c in range(WKV_UNROLL)], axis=0)
            e_in = jnp.exp(cum)
            e_neg = jnp.exp(-cum)
            e_tot = jnp.exp(tot)
            e_rem = e_tot * e_neg
            full = {'kt': kk * jnp.exp(cum - logw), 'q': r_t * e_in, 'kh': kdir * e_neg,
                    'bh': bvec * e_neg, 'khp': kdir * e_rem, 'bhp': bvec * e_rem, 'v': v_t,
                    'epc': e_tot}
            for g in range(N_GROUPS):
                lanes = slice(g * GROUP, (g + 1) * GROUP)
                for u in range(WKV_UNROLL):
                    lo = chunk_lo(d, u)
                    ch = (d, g, u)
                    for dst, name in ((kt, 'kt'), (q, 'q'), (kh, 'kh'), (bh, 'bh'), (khp, 'khp'),
                                      (bhp, 'bhp'), (v, 'v')):
                        dst[ch] = full[name][lo:lo + CHUNK, lanes]
                    epc[ch] = full['epc'][lo:lo + 1, lanes]

        def stack(*xs):
            return jnp.concatenate([x.astype(BF16) for x in xs], axis=0)

        lhs2 = {ch: stack(kt[ch], q[ch]) for ch in chains}
        a1 = {ch: _dot(lhs2[ch], bd(kh[ch]), NT) for ch in chains}
        a2 = {ch: _dot(lhs2[ch], bd(bh[ch]), NT) for ch in chains}
        a_ak = {ch: a1[ch][0:CHUNK] * strict[ch[0]] for ch in chains}
        a_qk = {ch: a1[ch][CHUNK:] * incl[ch[0]] for ch in chains}
        a_ab = {ch: a2[ch][0:CHUNK] * strict[ch[0]] for ch in chains}
        a_qb = {ch: a2[ch][CHUNK:] * incl[ch[0]] for ch in chains}

        khp_t = {ch: block_transpose(khp[ch]) for ch in chains}
        bhp_t = {ch: block_transpose(bhp[ch]) for ch in chains}

        t_inv = {ch: eye_a - a_ab[ch] for ch in chains}
        n_pow = {ch: _bdot(a_ab[ch], bd(a_ab[ch])) for ch in chains}
        for _ in range(4):
            both = {ch: _dot(stack(t_inv[ch], n_pow[ch]), bd(n_pow[ch])) for ch in chains}
            t_inv = {ch: t_inv[ch] + both[ch][0:CHUNK] for ch in chains}
            n_pow = {ch: both[ch][CHUNK:] for ch in chains}
        t_inv = {ch: t_inv[ch] + _bdot(t_inv[ch], bd(n_pow[ch])) for ch in chains}

        on_v = {ch: _dot(stack(a_ak[ch], a_qk[ch], khp_t[ch]), bd(v[ch])) for ch in chains}
        av = {ch: on_v[ch][0:CHUNK] for ch in chains}
        lhs_qb = {ch: _dot(stack(a_qb[ch], bhp_t[ch]), bd(t_inv[ch])).astype(BF16) for ch in chains}
        on_ktp = {ch: _dot(lhs_qb[ch], bd(kt[ch])) for ch in chains}
        on_vp = {ch: _dot(lhs_qb[ch], bd(av[ch])) for ch in chains}
        qp = {ch: q[ch] - on_ktp[ch][0:CHUNK] for ch in chains}
        yp = {ch: on_v[ch][CHUNK:2 * CHUNK] - on_vp[ch][0:CHUNK] for ch in chains}
        gp = {ch: eye_a * epc[ch] - on_ktp[ch][CHUNK:] for ch in chains}
        hp = {ch: on_v[ch][2 * CHUNK:] - on_vp[ch][CHUNK:] for ch in chains}

        tile0 = [i * tile, (n_tiles - 1 - i) * tile]
        for d in range(2):
            for g in range(N_GROUPS):
                m = mst[d, g]
                for u in range(WKV_UNROLL):
                    ch = (d, g, u)
                    on_m = _dot(stack(qp[ch], gp[ch]), bd(m))
                    out_rows = pl.ds(pl.multiple_of(tile0[d] + slab0[d] + chunk_lo(d, u), CHUNK), CHUNK)
                    y_ref[0, out_rows, g * GROUP:(g + 1) * GROUP] += on_m[0:CHUNK] + yp[ch]
                    m = on_m[CHUNK:] + hp[ch]
                mst[d, g] = m
        return carry

    lax.fori_loop(0, n_chunks // WKV_UNROLL, loop_step, 0)

    if has_sout:
        @pl.when(i == n_tiles - 1)
        def _fin():
            for d in range(2):
                for g in range(N_GROUPS):
                    st = mst[d, g].T
                    for hh in range(HEADS_PER_GROUP):
                        sout_ref[0, d, g * HEADS_PER_GROUP + hh] = st[hh * HEAD:(hh + 1) * HEAD, :]


def _wkv(rws, s0_bd, p, tile, has_sout):
    bsz, seq, _ = rws.shape
    n_tiles = seq // tile
    has_s0 = s0_bd is not None
    merged = n_tiles == 1
    in_specs = [pl.BlockSpec((1, tile, RW_COLS), lambda b, i: (b, i, 0))]
    args = [rws]
    if not merged:
        in_specs.append(pl.BlockSpec((1, tile, RW_COLS), lambda b, i: (b, n_tiles - 1 - i, 0)))
        args.append(rws)
    if has_s0:
        in_specs.append(pl.BlockSpec((1, 2, N_RHEADS, HEAD, HEAD), lambda b, i: (b, 0, 0, 0, 0)))
        args.append(s0_bd)
    in_specs += [
        _const_spec((2, D_RWKV)),
        _const_spec((2, 2 * R_LO, D_RWKV)),
        _const_spec((2, D_RWKV)),
        _const_spec((2, 2 * R_LO, D_RWKV)),
        _const_spec((1, D_RWKV)),
        _const_spec((1, D_RWKV)),
    ]
    args += [p['w0'], p['w2p'], p['a0'], p['a2p'], p['k_k'], p['k_a']]
    out_specs = [pl.BlockSpec((1, seq, D_RWKV), lambda b, i: (b, 0, 0))]
    out_shape = [jax.ShapeDtypeStruct((bsz, seq, D_RWKV), F32)]
    if has_sout:
        out_specs.append(pl.BlockSpec((1, 2, N_RHEADS, HEAD, HEAD), lambda b, i: (b, 0, 0, 0, 0)))
        out_shape.append(jax.ShapeDtypeStruct((bsz, 2, N_RHEADS, HEAD, HEAD), F32))
    return pl.pallas_call(
        functools.partial(_wkv_kernel, tile=tile, n_tiles=n_tiles, has_s0=has_s0,
                          has_sout=has_sout),
        grid=(bsz, n_tiles),
        in_specs=in_specs,
        out_specs=out_specs,
        out_shape=out_shape,
        scratch_shapes=[pltpu.VMEM((2, N_GROUPS, HEAD, GROUP), F32)],
        compiler_params=pltpu.CompilerParams(
            dimension_semantics=("arbitrary", "arbitrary"), vmem_limit_bytes=VMEM_LIMIT),
        name="wkv_s0" if has_s0 else "wkv",
    )(*args)


def _mix_kernel(x_ref, y_ref, rkv_ref, glo_ref, u_ref, mods_ref, rk_ref, lxg_ref, lxb_ref,
                g2_ref, wout_ref, x1_ref):
    ones_bd = _head_ones()
    g1 = mods_ref[0, 2:3, :]
    o = y_ref[0]
    mu = _head_sum(o, ones_bd) * (1.0 / HEAD)
    dv = o - mu
    var = _head_sum(dv * dv, ones_bd) * (1.0 / HEAD)
    on = dv * lax.rsqrt(var + EPS_GN) * lxg_ref[...] + lxb_ref[...]
    r = rkv_ref[0, :, 0:D_RWKV]
    k = rkv_ref[0, :, D_RWKV:2 * D_RWKV]
    v = rkv_ref[0, :, 2 * D_RWKV:3 * D_RWKV]
    glo = glo_ref[0]
    bonus = _head_sum(r * k * rk_ref[...], ones_bd) * v
    gate = _dot(_sigmoid(glo).astype(BF16), g2_ref[...])
    o2 = (on + bonus) * gate
    mixed = _dot(u_ref[0].astype(BF16), wout_ref[0:D_CONV, :])
    mixed = mixed + _dot(o2.astype(BF16), wout_ref[D_CONV:, :])
    x1_ref[0] = x_ref[0] + g1 * mixed


def _mix(x, y, rws, u, mods, cond_of_b, p, tile):
    bsz, seq, _ = x.shape
    n_tiles = seq // tile

    def tok(width, col_block=0):
        return pl.BlockSpec((1, tile, width), lambda b, i: (b, i, col_block))

    glo_block = (RW_COLS - R_G) // R_G
    return pl.pallas_call(
        _mix_kernel,
        grid=(bsz, n_tiles),
        in_specs=[
            tok(D_MODEL), tok(D_RWKV), tok(3 * D_RWKV), tok(R_G, glo_block), tok(D_CONV),
            pl.BlockSpec((1, 6, D_MODEL), lambda b, i: (cond_of_b(b), 0, 0)),
            _const_spec((1, D_RWKV)),
            _const_spec((1, D_RWKV)),
            _const_spec((1, D_RWKV)),
            _const_spec((R_G, D_RWKV)),
            _const_spec((D_MODEL, D_MODEL)),
        ],
        out_specs=tok(D_MODEL),
        out_shape=jax.ShapeDtypeStruct((bsz, seq, D_MODEL), F32),
        compiler_params=pltpu.CompilerParams(
            dimension_semantics=("parallel", "parallel"), vmem_limit_bytes=VMEM_LIMIT),
        name="mix",
    )(x, y, rws, rws, u, mods, p['r_k'], p['ln_x_g'], p['ln_x_b'], p['g2'], p['w_out'])


FFN_COLS = 256
FFN_PAD = 8


def _ffn_kernel(*refs, tile, n_tiles, grid_conv, period):
    if grid_conv:
        (x_ref, xp_ref, xn_ref, mods_ref, n2g_ref, up_ref, dw_ref, dwb_ref, down_ref, fg_ref,
         o_ref, vbuf, gbuf, vl, vr, gl, gr, act_s) = refs
    else:
        (x_ref, mods_ref, n2g_ref, up_ref, dw_ref, dwb_ref, down_ref, fg_ref,
         o_ref, vbuf, gbuf, vl, vr, gl, gr, act_s) = refs
    i = pl.program_id(1)
    sh2 = mods_ref[0, 3:4, :]
    sc2 = mods_ref[0, 4:5, :]
    g2 = mods_ref[0, 5:6, :]
    halo = FFN_HALO if grid_conv else 0
    if grid_conv:
        xe = jnp.concatenate([xp_ref[0], x_ref[0], xn_ref[0]], axis=0)
    else:
        xe = x_ref[0]
    rows_e = tile + 2 * halo
    h2 = _rmsnorm(xe, n2g_ref[...]) * (1.0 + sc2) + sh2

    if grid_conv:
        e_idx = lax.broadcasted_iota(jnp.int32, (rows_e, 1), 0)
        pv = jnp.where(i > 0, 1.0, 0.0).astype(F32)
        nv = jnp.where(i < n_tiles - 1, 1.0, 0.0).astype(F32)
        h2 = h2 * (jnp.where(e_idx >= halo, 1.0, pv) * jnp.where(e_idx < halo + tile, 1.0, nv))
        taps = [(dr, dc) for dr in (-1, 0, 1) for dc in (-1, 0, 1)]
    else:
        taps = [(0, -1), (0, 0), (0, 1)]
    masked = period < tile
    if masked:
        colpos = lax.broadcasted_iota(jnp.int32, (tile, 1), 0) % period
        m_left = jnp.where(colpos >= 1, 1.0, 0.0)
        m_right = jnp.where(colpos <= period - 2, 1.0, 0.0)

    zpad = jnp.zeros((FFN_PAD, FFN_COLS), F32)
    vbuf[0:FFN_PAD, :] = zpad
    vbuf[FFN_PAD + rows_e:, :] = zpad
    gbuf[0:FFN_PAD, :] = zpad
    gbuf[FFN_PAD + rows_e:, :] = zpad

    def conv(buf, side_l, side_r, col0):
        ext = tile + 2 * FFN_PAD
        sums = {}
        for ti, (dr, dc) in enumerate(taps):
            win = buf[pl.ds(halo + GRID_W * dr, ext), :]
            term = win * dw_ref[ti:ti + 1, col0:col0 + FFN_COLS]
            sums[dc] = term if dc not in sums else sums[dc] + term
        side_l[...] = sums[-1]
        side_r[...] = sums[1]
        left = side_l[pl.ds(FFN_PAD - 1, tile), :]
        right = side_r[pl.ds(FFN_PAD + 1, tile), :]
        if masked:
            left = left * m_left
            right = right * m_right
        centre = sums[0][FFN_PAD:FFN_PAD + tile]
        return centre + left + right + dwb_ref[:, col0:col0 + FFN_COLS]

    hb = h2.astype(BF16)
    for cc in range(D_FF // FFN_COLS):
        c0 = cc * FFN_COLS
        vbuf[FFN_PAD:FFN_PAD + rows_e, :] = _dot(hb, up_ref[:, c0:c0 + FFN_COLS])
        gbuf[FFN_PAD:FFN_PAD + rows_e, :] = _dot(hb, up_ref[:, D_FF + c0:D_FF + c0 + FFN_COLS])
        val = conv(vbuf, vl, vr, c0)
        gate = conv(gbuf, gl, gr, D_FF + c0)
        act_s[:, c0:c0 + FFN_COLS] = (val * (gate * _sigmoid(gate))).astype(BF16)

    x2 = x_ref[0] + g2 * _dot(act_s[...], down_ref[...])
    o_ref[0] = _rmsnorm(x2, fg_ref[...])


def _ffn(x1, mods, cond_of_b, p, tile, grid_conv, period):
    bsz, seq, _ = x1.shape
    n_tiles = seq // tile
    in_specs = [pl.BlockSpec((1, tile, D_MODEL), lambda b, i: (b, i, 0))]
    args = [x1]
    if grid_conv:
        hb = tile // FFN_HALO
        n_hblk = seq // FFN_HALO
        in_specs += [
            pl.BlockSpec((1, FFN_HALO, D_MODEL), lambda b, i: (b, jnp.maximum(i * hb - 1, 0), 0)),
            pl.BlockSpec((1, FFN_HALO, D_MODEL),
                         lambda b, i: (b, jnp.minimum((i + 1) * hb, n_hblk - 1), 0)),
        ]
        args += [x1, x1]
        dw = p['ffn_dw9']
    else:
        dw = p['ffn_dw3']
    n_taps = dw.shape[0]
    in_specs += [
        pl.BlockSpec((1, 6, D_MODEL), lambda b, i: (cond_of_b(b), 0, 0)),
        _const_spec((1, D_MODEL)),
        _const_spec((D_MODEL, 2 * D_FF)),
        _const_spec((n_taps, 2 * D_FF)),
        _const_spec((1, 2 * D_FF)),
        _const_spec((D_FF, D_MODEL)),
        _const_spec((1, D_MODEL)),
    ]
    args += [mods, p['norm2_g'], p['ffn_up'], dw, p['ffn_dw_b'], p['ffn_down'], p['final_g']]
    halo = FFN_HALO if grid_conv else 0
    buf_rows = tile + 2 * halo + 2 * FFN_PAD
    return pl.pallas_call(
        functools.partial(_ffn_kernel, tile=tile, n_tiles=n_tiles, grid_conv=grid_conv,
                          period=period),
        grid=(bsz, n_tiles),
        in_specs=in_specs,
        out_specs=pl.BlockSpec((1, tile, D_MODEL), lambda b, i: (b, i, 0)),
        out_shape=jax.ShapeDtypeStruct((bsz, seq, D_MODEL), F32),
        scratch_shapes=[pltpu.VMEM((buf_rows, FFN_COLS), F32),
                        pltpu.VMEM((buf_rows, FFN_COLS), F32)]
        + [pltpu.VMEM((tile + 2 * FFN_PAD, FFN_COLS), F32) for _ in range(4)]
        + [pltpu.VMEM((tile, D_FF), BF16)],
        compiler_params=pltpu.CompilerParams(
            dimension_semantics=("parallel", "parallel"), vmem_limit_bytes=VMEM_LIMIT),
        name="ffn_grid" if grid_conv else "ffn",
    )(*args)


def _pad_lowrank(w):
    z = jnp.zeros_like(w[0])
    return jnp.stack([jnp.concatenate([w[0], z], axis=0),
                      jnp.concatenate([z, w[1]], axis=0)]).astype(BF16)


def kernel(x_prompt, x_sample, state_wkv, c, c_ctx, ada_w, ada_b, norm1_g, w_in, shift_k, conv_dw, conv_dw_b, conv_ln_g, conv_ln_b, w0, w2, a0, a2, g2, k_k, k_a, r_k, ln_x_g, ln_x_b, w_out, norm2_g, ffn_up, ffn_dw, ffn_dw_b, ffn_down, final_g):
    assert ada_w.shape[0] == 1, "single layer"
    dec_b = x_sample.shape[0]
    cond8 = jnp.zeros((8, D_MODEL), F32).at[0].set(c_ctx).at[1:1 + dec_b].set(c)
    mods = _adaln(cond8, ada_w[0], ada_b).reshape(8, 6, D_MODEL)

    p = {
        'norm1_g': norm1_g, 'w_in': w_in[0].astype(BF16), 'shift_k': shift_k[0],
        'conv_dw': jnp.concatenate([conv_dw[0], jnp.zeros((1, D_CONV), F32)], axis=0),
        'conv_dw_b': conv_dw_b, 'conv_ln_g': conv_ln_g, 'conv_ln_b': conv_ln_b,
        'w0': w0[0], 'w2p': _pad_lowrank(w2[0]), 'a0': a0[0], 'a2p': _pad_lowrank(a2[0]),
        'k_k': k_k, 'k_a': k_a, 'r_k': r_k.reshape(1, D_RWKV),
        'ln_x_g': ln_x_g, 'ln_x_b': ln_x_b, 'g2': g2[0].astype(BF16),
        'w_out': w_out[0].astype(BF16), 'norm2_g': norm2_g,
        'ffn_up': ffn_up[0].astype(BF16), 'ffn_dw9': ffn_dw[0].reshape(9, 2 * D_FF),
        'ffn_dw3': ffn_dw[0, 1], 'ffn_dw_b': ffn_dw_b, 'ffn_down': ffn_down[0].astype(BF16),
        'final_g': final_g.reshape(1, D_MODEL),
    }

    def ctx_cond(b):
        return 0

    def lat_cond(b):
        return b + 1

    seq = x_prompt.shape[1]
    u, rws = _pre(x_prompt, mods, ctx_cond, p, tile=seq, has_halo=False)
    y, sfin = _wkv(rws, None, p, tile=seq, has_sout=True)
    bsz = x_prompt.shape[0]

    def pair(t):
        return t.reshape(bsz // 2, 2 * seq, t.shape[-1])

    x1 = _mix(pair(x_prompt), pair(y), pair(rws), pair(u), mods, ctx_cond, p, tile=2 * seq)
    y_prompt = _ffn(x1.reshape(x_prompt.shape), mods, ctx_cond, p, tile=seq, grid_conv=False,
                    period=seq)
    new_state = sfin[:, None].astype(state_wkv.dtype)

    s0_bd = state_wkv[:, 0].astype(F32)
    u, rws = _pre(x_sample, mods, lat_cond, p, tile=512, has_halo=True)
    (y,) = _wkv(rws, s0_bd, p, tile=256, has_sout=False)
    x1 = _mix(x_sample, y, rws, u, mods, lat_cond, p, tile=512)
    y_sample = _ffn(x1, mods, lat_cond, p, tile=512, grid_conv=True, period=GRID_W)

    return (y_prompt, y_sample, new_state)
```

```python
import functools
import math

import jax
import jax.numpy as jnp
from jax import lax
from jax.experimental import pallas as pl
from jax.experimental.pallas import tpu as pltpu

F32 = jnp.float32
BF16 = jnp.bfloat16

D_MODEL = 1024
D_CONV = 512
D_RWKV = 512
HEAD = 64
N_RHEADS = 8
CONV_W = 31
R_LO = 64
R_G = 128
D_FF = 2560
RW_COLS = 3 * D_RWKV + 4 * R_LO + R_G
P_IN = 2 * D_CONV + RW_COLS
GRID_W = 64
EPS_RMS = 1e-6
EPS_LN = 1e-5
EPS_GN = 64e-5
L2_EPS = 1e-24
EXP_M05 = math.exp(-0.5)

CHUNK = 64
GROUP = 256
HEADS_PER_GROUP = GROUP // HEAD
N_GROUPS = D_RWKV // GROUP
CONV_HALO = 16
assert CONV_HALO - CONV_W // 2 == 1
FFN_HALO = GRID_W
VMEM_LIMIT = 56 * 1024 * 1024

WKV_UNROLL = 2

NN = (((1,), (0,)), ((), ()))
NT = (((1,), (1,)), ((), ()))


def _dot(a, b, dims=NN):
    return lax.dot_general(a, b, dims, preferred_element_type=F32)


def _split2(x):
    hi = x.astype(BF16)
    lo = (x - hi.astype(F32)).astype(BF16)
    return hi, lo


def _bdot(a, b, dims=NN):
    return _dot(a.astype(BF16), b.astype(BF16), dims)


def _sigmoid(x):
    return jax.nn.sigmoid(x)


def _head_ones():
    r = lax.broadcasted_iota(jnp.int32, (GROUP, GROUP), 0) // HEAD
    c = lax.broadcasted_iota(jnp.int32, (GROUP, GROUP), 1) // HEAD
    return jnp.where(r == c, 1.0, 0.0).astype(BF16)


def _head_sum(x, ones_bd):
    t = x.shape[0]
    pieces = []
    for g in range(N_GROUPS):
        pieces.extend(_split2(x[:, g * GROUP:(g + 1) * GROUP]))
    s = _dot(jnp.concatenate(pieces, axis=0), ones_bd)
    return jnp.concatenate([s[(2 * g) * t:(2 * g + 1) * t] + s[(2 * g + 1) * t:(2 * g + 2) * t]
                            for g in range(N_GROUPS)], axis=1)


def _rmsnorm(x, g):
    ms = jnp.mean(x * x, axis=-1, keepdims=True)
    return x * lax.rsqrt(ms + EPS_RMS) * g


def _adaln_kernel(c_ref, w_ref, b_ref, o_ref):
    c = c_ref[...]
    s = (c * _sigmoid(c)).astype(BF16)
    o_ref[...] = _dot(s, w_ref[...].astype(BF16)) + b_ref[...]


def _adaln(cond8, ada_w, ada_b):
    n_out = ada_w.shape[1]
    bn = 1024
    return pl.pallas_call(
        _adaln_kernel,
        grid=(n_out // bn,),
        in_specs=[
            pl.BlockSpec((8, D_MODEL), lambda j: (0, 0)),
            pl.BlockSpec((D_MODEL, bn), lambda j: (0, j)),
            pl.BlockSpec((1, bn), lambda j: (0, j)),
        ],
        out_specs=pl.BlockSpec((8, bn), lambda j: (0, j)),
        out_shape=jax.ShapeDtypeStruct((8, n_out), F32),
        compiler_params=pltpu.CompilerParams(dimension_semantics=("parallel",)),
        name="adaln",
    )(cond8, ada_w, ada_b)


def _pre_kernel(*refs, tile, n_tiles, has_halo):
    if has_halo:
        (x_ref, xp_ref, xn_ref, mods_ref, n1g_ref, win_ref, sk_ref, cdw_ref, cdb_ref,
         lng_ref, lnb_ref, u_ref, rws_ref, ubuf, rwbuf, pbuf) = refs
    else:
        (x_ref, mods_ref, n1g_ref, win_ref, sk_ref, cdw_ref, cdb_ref,
         lng_ref, lnb_ref, u_ref, rws_ref, ubuf, rwbuf, pbuf) = refs
    i = pl.program_id(1)
    sh1 = mods_ref[0, 0:1, :]
    sc1 = mods_ref[0, 1:2, :]

    if has_halo:
        x = jnp.concatenate([xp_ref[0], x_ref[0], xn_ref[0]], axis=0)
    else:
        x = x_ref[0]
    h = _rmsnorm(x, n1g_ref[...]) * (1.0 + sc1) + sh1
    proj = _dot(h.astype(BF16), win_ref[...])
    cv = proj[:, 0:D_CONV]
    cg = proj[:, D_CONV:2 * D_CONV]
    rw = proj[:, 2 * D_CONV:]
    u0 = cv * _sigmoid(cg)

    if has_halo:
        pv = jnp.where(i > 0, 1.0, 0.0).astype(F32)
        nv = jnp.where(i < n_tiles - 1, 1.0, 0.0).astype(F32)
        ubuf[0:CONV_HALO, :] = u0[0:CONV_HALO] * pv
        ubuf[CONV_HALO:CONV_HALO + tile, :] = u0[CONV_HALO:CONV_HALO + tile]
        ubuf[CONV_HALO + tile:, :] = u0[CONV_HALO + tile:] * nv
        rwbuf[0:8, :] = rw[CONV_HALO - 8:CONV_HALO] * pv
        rwbuf[8:8 + tile, :] = rw[CONV_HALO:CONV_HALO + tile]
        rwbuf[8 + tile:, :] = rw[CONV_HALO + tile:CONV_HALO + tile + 8] * nv
    else:
        ubuf[0:CONV_HALO, :] = jnp.zeros((CONV_HALO, D_CONV), F32)
        ubuf[CONV_HALO:CONV_HALO + tile, :] = u0
        ubuf[CONV_HALO + tile:, :] = jnp.zeros((CONV_HALO, D_CONV), F32)
        rwbuf[0:8, :] = jnp.zeros((8, RW_COLS), F32)
        rwbuf[8:8 + tile, :] = rw
        rwbuf[8 + tile:, :] = jnp.zeros((8, RW_COLS), F32)

    ext = tile + 8
    for s in range(8):
        ps = None
        for m in range(4):
            j = 8 * m + s - 1
            if 0 <= j < CONV_W:
                term = ubuf[pl.ds(8 * m, ext), :] * cdw_ref[j:j + 1, :]
                ps = term if ps is None else ps + term
        pbuf[s] = ps
    acc = pbuf[0, 0:tile, :] + cdb_ref[...]
    for s in range(1, 8):
        acc = acc + pbuf[s, pl.ds(s, tile), :]
    mu = jnp.mean(acc, axis=-1, keepdims=True)
    dv = acc - mu
    var = jnp.mean(dv * dv, axis=-1, keepdims=True)
    ln = dv * lax.rsqrt(var + EPS_LN) * lng_ref[...] + lnb_ref[...]
    u_ref[0] = (ln * _sigmoid(ln)).astype(BF16)

    rws = rwbuf[pl.ds(7, tile), :] * sk_ref[0:1, :]
    rws = rws + rwbuf[pl.ds(8, tile), :] * sk_ref[1:2, :]
    rws = rws + rwbuf[pl.ds(9, tile), :] * sk_ref[2:3, :]
    rws_ref[0] = rws


def _const_spec(shape):
    nd = len(shape)
    return pl.BlockSpec(shape, lambda b, i: (0,) * nd)


def _pre(x, mods, cond_of_b, p, tile, has_halo):
    bsz, seq, _ = x.shape
    n_tiles = seq // tile
    hb = tile // CONV_HALO
    n_hblk = seq // CONV_HALO
    in_specs = [pl.BlockSpec((1, tile, D_MODEL), lambda b, i: (b, i, 0))]
    args = [x]
    if has_halo:
        in_specs += [
            pl.BlockSpec((1, CONV_HALO, D_MODEL), lambda b, i: (b, jnp.maximum(i * hb - 1, 0), 0)),
            pl.BlockSpec((1, CONV_HALO, D_MODEL),
                         lambda b, i: (b, jnp.minimum((i + 1) * hb, n_hblk - 1), 0)),
        ]
        args += [x, x]
    in_specs += [
        pl.BlockSpec((1, 6, D_MODEL), lambda b, i: (cond_of_b(b), 0, 0)),
        _const_spec((1, D_MODEL)),
        _const_spec((D_MODEL, P_IN)),
        _const_spec((3, RW_COLS)),
        _const_spec((32, D_CONV)),
        _const_spec((1, D_CONV)),
        _const_spec((1, D_CONV)),
        _const_spec((1, D_CONV)),
    ]
    args += [mods, p['norm1_g'], p['w_in'], p['shift_k'], p['conv_dw'], p['conv_dw_b'],
             p['conv_ln_g'], p['conv_ln_b']]
    return pl.pallas_call(
        functools.partial(_pre_kernel, tile=tile, n_tiles=n_tiles, has_halo=has_halo),
        grid=(bsz, n_tiles),
        in_specs=in_specs,
        out_specs=[
            pl.BlockSpec((1, tile, D_CONV), lambda b, i: (b, i, 0)),
            pl.BlockSpec((1, tile, RW_COLS), lambda b, i: (b, i, 0)),
        ],
        out_shape=[
            jax.ShapeDtypeStruct((bsz, seq, D_CONV), BF16),
            jax.ShapeDtypeStruct((bsz, seq, RW_COLS), F32),
        ],
        scratch_shapes=[
            pltpu.VMEM((tile + 2 * CONV_HALO, D_CONV), F32),
            pltpu.VMEM((tile + 16, RW_COLS), F32),
            pltpu.VMEM((8, tile + 8, D_CONV), F32),
        ],
        compiler_params=pltpu.CompilerParams(
            dimension_semantics=("parallel", "parallel"), vmem_limit_bytes=VMEM_LIMIT),
        name="pre_halo" if has_halo else "pre",
    )(*args)


def _wkv_kernel(*refs, tile, n_tiles, has_s0, has_sout, n_casts):
    merged = n_tiles == 1
    refs = list(refs)
    if merged:
        rf_ref = rb_ref = refs[0]
        pos = 1
    else:
        rf_ref, rb_ref = refs[0], refs[1]
        pos = 2
    s0_ref = None
    if has_s0:
        s0_ref = refs[pos]
        pos += 1
    w0_ref, w2_ref, a0_ref, a2_ref, kk_ref, ka_ref = refs[pos:pos + 6]
    pos += 6
    cast_in = refs[pos:pos + n_casts]
    pos += n_casts
    y_ref = refs[pos]
    pos += 1
    sout_ref = None
    if has_sout:
        sout_ref = refs[pos]
        pos += 1
    cast_out = refs[pos:pos + n_casts]
    pos += n_casts
    (mst,) = refs[pos:]

    for src, dst in zip(cast_in, cast_out):
        dst[...] = src[...].astype(BF16)

    i = pl.program_id(1)
    n_chunks = tile // CHUNK

    lane_head = lax.broadcasted_iota(jnp.int32, (1, GROUP), 1) // HEAD
    head_masks = [jnp.where(lane_head == hh, 1.0, 0.0).astype(BF16)
                  for hh in range(HEADS_PER_GROUP)]
    t_idx = lax.broadcasted_iota(jnp.int32, (CHUNK, GROUP), 0)
    s_idx = lax.broadcasted_iota(jnp.int32, (CHUNK, GROUP), 1) % CHUNK
    eye_a = jnp.where(s_idx == t_idx, 1.0, 0.0).astype(F32)

    def bd(x):
        xb = x.astype(BF16)
        return jnp.concatenate([xb * m for m in head_masks], axis=0)

    def block_transpose(x):
        xt = x.T
        return jnp.concatenate([xt[hh * HEAD:(hh + 1) * HEAD, :] for hh in range(HEADS_PER_GROUP)],
                               axis=1)

    @pl.when(i == 0)
    def _init():
        if has_s0:
            for d in range(2):
                for g in range(N_GROUPS):
                    heads = [s0_ref[0, d, g * HEADS_PER_GROUP + hh] for hh in range(HEADS_PER_GROUP)]
                    mst[d, g] = jnp.concatenate(heads, axis=0).T
        else:
            mst[...] = jnp.zeros(mst.shape, F32)
        y_ref[...] = jnp.zeros(y_ref.shape, F32)

    ones_bd = _head_ones()
    slab = WKV_UNROLL * CHUNK
    row = lax.broadcasted_iota(jnp.int32, (slab, slab), 0)
    col = lax.broadcasted_iota(jnp.int32, (slab, slab), 1)
    same_chunk = (row // CHUNK) == (col // CHUNK)
    tri = [jnp.where(same_chunk & (col <= row), 1.0, 0.0).astype(BF16),
           jnp.where(same_chunk & (col >= row), 1.0, 0.0).astype(BF16)]
    strict = [jnp.where(s_idx < t_idx, 1.0, 0.0).astype(F32),
              jnp.where(s_idx > t_idx, 1.0, 0.0).astype(F32)]
    incl = [jnp.where(s_idx <= t_idx, 1.0, 0.0).astype(F32),
            jnp.where(s_idx >= t_idx, 1.0, 0.0).astype(F32)]

    chains = [(d, g, u) for u in range(WKV_UNROLL) for d in range(2) for g in range(N_GROUPS)]

    def loop_step(it, carry):
        slab0 = [pl.multiple_of(it * slab, slab), pl.multiple_of(tile - (it + 1) * slab, slab)]

        def chunk_lo(d, u):
            return (u if d == 0 else WKV_UNROLL - 1 - u) * CHUNK

        kt, q, kh, bh, khp, bhp, v, epc = ({} for _ in range(8))
        for d in range(2):
            src = rf_ref if d == 0 else rb_ref
            srows = pl.ds(slab0[d], slab)
            r_t = src[0, srows, 0:D_RWKV]
            k_t = src[0, srows, D_RWKV:2 * D_RWKV]
            v_t = src[0, srows, 2 * D_RWKV:3 * D_RWKV]
            wlo = src[0, srows, 3 * D_RWKV:3 * D_RWKV + 2 * R_LO]
            alo = src[0, srows, 3 * D_RWKV + 2 * R_LO:3 * D_RWKV + 4 * R_LO]
            kk = k_t * kk_ref[...]
            ss = _head_sum(kk * kk, ones_bd)
            kk = kk * lax.rsqrt(jnp.maximum(ss, L2_EPS))
            w_raw = w0_ref[d:d + 1, :] + _dot(jnp.tanh(wlo).astype(BF16), w2_ref[d])
            logw = -_sigmoid(w_raw) * EXP_M05
            a = _sigmoid(a0_ref[d:d + 1, :] + _dot(alo.astype(BF16), a2_ref[d]))
            kdir = k_t * (1.0 + (a - 1.0) * ka_ref[...])
            bvec = kk * a
            lw_hi, lw_lo = _split2(logw)
            cum = _dot(tri[d], lw_hi) + _dot(tri[d], lw_lo)
            edge = CHUNK - 1 if d == 0 else 0
            tot = jnp.concatenate(
                [jnp.broadcast_to(cum[c * CHUNK + edge:c * CHUNK + edge + 1, :], (CHUNK, D_RWKV))
                 for c in range(WKV_UNROLL)], axis=0)
            e_in = jnp.exp(cum)
            e_neg = jnp.exp(-cum)
            e_tot = jnp.exp(tot)
            e_rem = e_tot * e_neg
            full = {'kt': kk * jnp.exp(cum - logw), 'q': r_t * e_in, 'kh': kdir * e_neg,
                    'bh': bvec * e_neg, 'khp': kdir * e_rem, 'bhp': bvec * e_rem, 'v': v_t,
                    'epc': e_tot}
            for g in range(N_GROUPS):
                lanes = slice(g * GROUP, (g + 1) * GROUP)
                for u in range(WKV_UNROLL):
                    lo = chunk_lo(d, u)
                    ch = (d, g, u)
                    for dst, name in ((kt, 'kt'), (q, 'q'), (kh, 'kh'), (bh, 'bh'), (khp, 'khp'),
                                      (bhp, 'bhp'), (v, 'v')):
                        dst[ch] = full[name][lo:lo + CHUNK, lanes]
                    epc[ch] = full['epc'][lo:lo + 1, lanes]

        def stack(*xs):
            return jnp.concatenate([x.astype(BF16) for x in xs], axis=0)

        lhs2 = {ch: stack(kt[ch], q[ch]) for ch in chains}
        a1 = {ch: _dot(lhs2[ch], bd(kh[ch]), NT) for ch in chains}
        a2 = {ch: _dot(lhs2[ch], bd(bh[ch]), NT) for ch in chains}
        a_ak = {ch: a1[ch][0:CHUNK] * strict[ch[0]] for ch in chains}
        a_qk = {ch: a1[ch][CHUNK:] * incl[ch[0]] for ch in chains}
        a_ab = {ch: a2[ch][0:CHUNK] * strict[ch[0]] for ch in chains}
        a_qb = {ch: a2[ch][CHUNK:] * incl[ch[0]] for ch in chains}

        khp_t = {ch: block_transpose(khp[ch]) for ch in chains}
        bhp_t = {ch: block_transpose(bhp[ch]) for ch in chains}

        t_inv = {ch: eye_a - a_ab[ch] for ch in chains}
        n_pow = {ch: _bdot(a_ab[ch], bd(a_ab[ch])) for ch in chains}
        for _ in range(4):
            both = {ch: _dot(stack(t_inv[ch], n_pow[ch]), bd(n_pow[ch])) for ch in chains}
            t_inv = {ch: t_inv[ch] + both[ch][0:CHUNK] for ch in chains}
            n_pow = {ch: both[ch][CHUNK:] for ch in chains}
        t_inv = {ch: t_inv[ch] + _bdot(t_inv[ch], bd(n_pow[ch])) for ch in chains}

        on_v = {ch: _dot(stack(a_ak[ch], a_qk[ch], khp_t[ch]), bd(v[ch])) for ch in chains}
        av = {ch: on_v[ch][0:CHUNK] for ch in chains}
        lhs_qb = {ch: _dot(stack(a_qb[ch], bhp_t[ch]), bd(t_inv[ch])).astype(BF16) for ch in chains}
        on_ktp = {ch: _dot(lhs_qb[ch], bd(kt[ch])) for ch in chains}
        on_vp = {ch: _dot(lhs_qb[ch], bd(av[ch])) for ch in chains}
        qp = {ch: q[ch] - on_ktp[ch][0:CHUNK] for ch in chains}
        yp = {ch: on_v[ch][CHUNK:2 * CHUNK] - on_vp[ch][0:CHUNK] for ch in chains}
        gp = {ch: eye_a * epc[ch] - on_ktp[ch][CHUNK:] for ch in chains}
        hp = {ch: on_v[ch][2 * CHUNK:] - on_vp[ch][CHUNK:] for ch in chains}

        tile0 = [i * tile, (n_tiles - 1 - i) * tile]
        for d in range(2):
            for g in range(N_GROUPS):
                m = mst[d, g]
                for u in range(WKV_UNROLL):
                    ch = (d, g, u)
                    on_m = _dot(stack(qp[ch], gp[ch]), bd(m))
                    out_rows = pl.ds(pl.multiple_of(tile0[d] + slab0[d] + chunk_lo(d, u), CHUNK), CHUNK)
                    y_ref[0, out_rows, g * GROUP:(g + 1) * GROUP] += on_m[0:CHUNK] + yp[ch]
                    m = on_m[CHUNK:] + hp[ch]
                mst[d, g] = m
        return carry

    lax.fori_loop(0, n_chunks // WKV_UNROLL, loop_step, 0)

    if has_sout:
        @pl.when(i == n_tiles - 1)
        def _fin():
            for d in range(2):
                for g in range(N_GROUPS):
                    st = mst[d, g].T
                    for hh in range(HEADS_PER_GROUP):
                        sout_ref[0, d, g * HEADS_PER_GROUP + hh] = st[hh * HEAD:(hh + 1) * HEAD, :]


def _wkv(rws, s0_bd, p, tile, has_sout, side_casts=()):
    bsz, seq, _ = rws.shape
    n_tiles = seq // tile
    has_s0 = s0_bd is not None
    merged = n_tiles == 1
    in_specs = [pl.BlockSpec((1, tile, RW_COLS), lambda b, i: (b, i, 0))]
    args = [rws]
    if not merged:
        in_specs.append(pl.BlockSpec((1, tile, RW_COLS), lambda b, i: (b, n_tiles - 1 - i, 0)))
        args.append(rws)
    if has_s0:
        in_specs.append(pl.BlockSpec((1, 2, N_RHEADS, HEAD, HEAD), lambda b, i: (b, 0, 0, 0, 0)))
        args.append(s0_bd)
    in_specs += [
        _const_spec((2, D_RWKV)),
        _const_spec((2, 2 * R_LO, D_RWKV)),
        _const_spec((2, D_RWKV)),
        _const_spec((2, 2 * R_LO, D_RWKV)),
        _const_spec((1, D_RWKV)),
        _const_spec((1, D_RWKV)),
    ]
    args += [p['w0'], p['w2p'], p['a0'], p['a2p'], p['k_k'], p['k_a']]
    out_specs = [pl.BlockSpec((1, seq, D_RWKV), lambda b, i: (b, 0, 0))]
    out_shape = [jax.ShapeDtypeStruct((bsz, seq, D_RWKV), F32)]
    if has_sout:
        out_specs.append(pl.BlockSpec((1, 2, N_RHEADS, HEAD, HEAD), lambda b, i: (b, 0, 0, 0, 0)))
        out_shape.append(jax.ShapeDtypeStruct((bsz, 2, N_RHEADS, HEAD, HEAD), F32))
    n_steps = bsz * n_tiles
    for w in side_casts:
        slab_rows = w.shape[0] // n_steps
        assert slab_rows * n_steps == w.shape[0] and slab_rows % 16 == 0
        spec = pl.BlockSpec((slab_rows, w.shape[1]), lambda b, i: (b * n_tiles + i, 0))
        in_specs.append(spec)
        args.append(w)
        out_specs.append(spec)
        out_shape.append(jax.ShapeDtypeStruct(w.shape, BF16))
    return pl.pallas_call(
        functools.partial(_wkv_kernel, tile=tile, n_tiles=n_tiles, has_s0=has_s0,
                          has_sout=has_sout, n_casts=len(side_casts)),
        grid=(bsz, n_tiles),
        in_specs=in_specs,
        out_specs=out_specs,
        out_shape=out_shape,
        scratch_shapes=[pltpu.VMEM((2, N_GROUPS, HEAD, GROUP), F32)],
        compiler_params=pltpu.CompilerParams(
            dimension_semantics=("arbitrary", "arbitrary"), vmem_limit_bytes=VMEM_LIMIT),
        name="wkv_s0" if has_s0 else "wkv",
    )(*args)


def _mix_kernel(x_ref, y_ref, rkv_ref, glo_ref, u_ref, mods_ref, rk_ref, lxg_ref, lxb_ref,
                g2_ref, wout_ref, x1_ref):
    ones_bd = _head_ones()
    g1 = mods_ref[0, 2:3, :]
    o = y_ref[0]
    mu = _head_sum(o, ones_bd) * (1.0 / HEAD)
    dv = o - mu
    var = _head_sum(dv * dv, ones_bd) * (1.0 / HEAD)
    on = dv * lax.rsqrt(var + EPS_GN) * lxg_ref[...] + lxb_ref[...]
    r = rkv_ref[0, :, 0:D_RWKV]
    k = rkv_ref[0, :, D_RWKV:2 * D_RWKV]
    v = rkv_ref[0, :, 2 * D_RWKV:3 * D_RWKV]
    glo = glo_ref[0]
    bonus = _head_sum(r * k * rk_ref[...], ones_bd) * v
    gate = _dot(_sigmoid(glo).astype(BF16), g2_ref[...])
    o2 = (on + bonus) * gate
    mixed = _dot(u_ref[0].astype(BF16), wout_ref[0:D_CONV, :])
    mixed = mixed + _dot(o2.astype(BF16), wout_ref[D_CONV:, :])
    x1_ref[0] = x_ref[0] + g1 * mixed


def _mix(x, y, rws, u, mods, cond_of_b, p, tile):
    bsz, seq, _ = x.shape
    n_tiles = seq // tile

    def tok(width, col_block=0):
        return pl.BlockSpec((1, tile, width), lambda b, i: (b, i, col_block))

    glo_block = (RW_COLS - R_G) // R_G
    return pl.pallas_call(
        _mix_kernel,
        grid=(bsz, n_tiles),
        in_specs=[
            tok(D_MODEL), tok(D_RWKV), tok(3 * D_RWKV), tok(R_G, glo_block), tok(D_CONV),
            pl.BlockSpec((1, 6, D_MODEL), lambda b, i: (cond_of_b(b), 0, 0)),
            _const_spec((1, D_RWKV)),
            _const_spec((1, D_RWKV)),
            _const_spec((1, D_RWKV)),
            _const_spec((R_G, D_RWKV)),
            _const_spec((D_MODEL, D_MODEL)),
        ],
        out_specs=tok(D_MODEL),
        out_shape=jax.ShapeDtypeStruct((bsz, seq, D_MODEL), F32),
        compiler_params=pltpu.CompilerParams(
            dimension_semantics=("parallel", "parallel"), vmem_limit_bytes=VMEM_LIMIT),
        name="mix",
    )(x, y, rws, rws, u, mods, p['r_k'], p['ln_x_g'], p['ln_x_b'], p['g2'], p['w_out'])


FFN_COLS = 256
FFN_PAD = 8


def _ffn_kernel(*refs, tile, n_tiles, grid_conv, period):
    if grid_conv:
        (x_ref, xp_ref, xn_ref, mods_ref, n2g_ref, up_ref, dw_ref, dwb_ref, down_ref, fg_ref,
         o_ref, vbuf, gbuf, vl, vr, gl, gr, act_s) = refs
    else:
        (x_ref, mods_ref, n2g_ref, up_ref, dw_ref, dwb_ref, down_ref, fg_ref,
         o_ref, vbuf, gbuf, vl, vr, gl, gr, act_s) = refs
    i = pl.program_id(1)
    sh2 = mods_ref[0, 3:4, :]
    sc2 = mods_ref[0, 4:5, :]
    g2 = mods_ref[0, 5:6, :]
    halo = FFN_HALO if grid_conv else 0
    if grid_conv:
        xe = jnp.concatenate([xp_ref[0], x_ref[0], xn_ref[0]], axis=0)
    else:
        xe = x_ref[0]
    rows_e = tile + 2 * halo
    h2 = _rmsnorm(xe, n2g_ref[...]) * (1.0 + sc2) + sh2

    if grid_conv:
        e_idx = lax.broadcasted_iota(jnp.int32, (rows_e, 1), 0)
        pv = jnp.where(i > 0, 1.0, 0.0).astype(F32)
        nv = jnp.where(i < n_tiles - 1, 1.0, 0.0).astype(F32)
        h2 = h2 * (jnp.where(e_idx >= halo, 1.0, pv) * jnp.where(e_idx < halo + tile, 1.0, nv))
        taps = [(dr, dc) for dr in (-1, 0, 1) for dc in (-1, 0, 1)]
    else:
        taps = [(0, -1), (0, 0), (0, 1)]
    masked = period < tile
    if masked:
        colpos = lax.broadcasted_iota(jnp.int32, (tile, 1), 0) % period
        m_left = jnp.where(colpos >= 1, 1.0, 0.0)
        m_right = jnp.where(colpos <= period - 2, 1.0, 0.0)

    zpad = jnp.zeros((FFN_PAD, FFN_COLS), F32)
    vbuf[0:FFN_PAD, :] = zpad
    vbuf[FFN_PAD + rows_e:, :] = zpad
    gbuf[0:FFN_PAD, :] = zpad
    gbuf[FFN_PAD + rows_e:, :] = zpad

    def conv(buf, side_l, side_r, col0):
        ext = tile + 2 * FFN_PAD
        sums = {}
        for ti, (dr, dc) in enumerate(taps):
            win = buf[pl.ds(halo + GRID_W * dr, ext), :]
            term = win * dw_ref[ti:ti + 1, col0:col0 + FFN_COLS]
            sums[dc] = term if dc not in sums else sums[dc] + term
        side_l[...] = sums[-1]
        side_r[...] = sums[1]
        left = side_l[pl.ds(FFN_PAD - 1, tile), :]
        right = side_r[pl.ds(FFN_PAD + 1, tile), :]
        if masked:
            left = left * m_left
            right = right * m_right
        centre = sums[0][FFN_PAD:FFN_PAD + tile]
        return centre + left + right + dwb_ref[:, col0:col0 + FFN_COLS]

    hb = h2.astype(BF16)
    for cc in range(D_FF // FFN_COLS):
        c0 = cc * FFN_COLS
        vbuf[FFN_PAD:FFN_PAD + rows_e, :] = _dot(hb, up_ref[:, c0:c0 + FFN_COLS])
        gbuf[FFN_PAD:FFN_PAD + rows_e, :] = _dot(hb, up_ref[:, D_FF + c0:D_FF + c0 + FFN_COLS])
        val = conv(vbuf, vl, vr, c0)
        gate = conv(gbuf, gl, gr, D_FF + c0)
        act_s[:, c0:c0 + FFN_COLS] = (val * (gate * _sigmoid(gate))).astype(BF16)

    x2 = x_ref[0] + g2 * _dot(act_s[...], down_ref[...])
    o_ref[0] = _rmsnorm(x2, fg_ref[...])


def _ffn(x1, mods, cond_of_b, p, tile, grid_conv, period):
    bsz, seq, _ = x1.shape
    n_tiles = seq // tile
    in_specs = [pl.BlockSpec((1, tile, D_MODEL), lambda b, i: (b, i, 0))]
    args = [x1]
    if grid_conv:
        hb = tile // FFN_HALO
        n_hblk = seq // FFN_HALO
        in_specs += [
            pl.BlockSpec((1, FFN_HALO, D_MODEL), lambda b, i: (b, jnp.maximum(i * hb - 1, 0), 0)),
            pl.BlockSpec((1, FFN_HALO, D_MODEL),
                         lambda b, i: (b, jnp.minimum((i + 1) * hb, n_hblk - 1), 0)),
        ]
        args += [x1, x1]
        dw = p['ffn_dw9']
    else:
        dw = p['ffn_dw3']
    n_taps = dw.shape[0]
    in_specs += [
        pl.BlockSpec((1, 6, D_MODEL), lambda b, i: (cond_of_b(b), 0, 0)),
        _const_spec((1, D_MODEL)),
        _const_spec((D_MODEL, 2 * D_FF)),
        _const_spec((n_taps, 2 * D_FF)),
        _const_spec((1, 2 * D_FF)),
        _const_spec((D_FF, D_MODEL)),
        _const_spec((1, D_MODEL)),
    ]
    args += [mods, p['norm2_g'], p['ffn_up'], dw, p['ffn_dw_b'], p['ffn_down'], p['final_g']]
    halo = FFN_HALO if grid_conv else 0
    buf_rows = tile + 2 * halo + 2 * FFN_PAD
    return pl.pallas_call(
        functools.partial(_ffn_kernel, tile=tile, n_tiles=n_tiles, grid_conv=grid_conv,
                          period=period),
        grid=(bsz, n_tiles),
        in_specs=in_specs,
        out_specs=pl.BlockSpec((1, tile, D_MODEL), lambda b, i: (b, i, 0)),
        out_shape=jax.ShapeDtypeStruct((bsz, seq, D_MODEL), F32),
        scratch_shapes=[pltpu.VMEM((buf_rows, FFN_COLS), F32),
                        pltpu.VMEM((buf_rows, FFN_COLS), F32)]
        + [pltpu.VMEM((tile + 2 * FFN_PAD, FFN_COLS), F32) for _ in range(4)]
        + [pltpu.VMEM((tile, D_FF), BF16)],
        compiler_params=pltpu.CompilerParams(
            dimension_semantics=("parallel", "parallel"), vmem_limit_bytes=VMEM_LIMIT),
        name="ffn_grid" if grid_conv else "ffn",
    )(*args)


def _pad_lowrank(w):
    z = jnp.zeros_like(w[0])
    return jnp.stack([jnp.concatenate([w[0], z], axis=0),
                      jnp.concatenate([z, w[1]], axis=0)]).astype(BF16)


def kernel(x_prompt, x_sample, state_wkv, c, c_ctx, ada_w, ada_b, norm1_g, w_in, shift_k, conv_dw, conv_dw_b, conv_ln_g, conv_ln_b, w0, w2, a0, a2, g2, k_k, k_a, r_k, ln_x_g, ln_x_b, w_out, norm2_g, ffn_up, ffn_dw, ffn_dw_b, ffn_down, final_g):
    assert ada_w.shape[0] == 1, "single layer"
    dec_b = x_sample.shape[0]
    cond8 = jnp.zeros((8, D_MODEL), F32).at[0].set(c_ctx).at[1:1 + dec_b].set(c)
    mods = _adaln(cond8, ada_w[0], ada_b).reshape(8, 6, D_MODEL)

    p = {
        'norm1_g': norm1_g, 'w_in': w_in[0].astype(BF16), 'shift_k': shift_k[0],
        'conv_dw': jnp.concatenate([conv_dw[0], jnp.zeros((1, D_CONV), F32)], axis=0),
        'conv_dw_b': conv_dw_b, 'conv_ln_g': conv_ln_g, 'conv_ln_b': conv_ln_b,
        'w0': w0[0], 'w2p': _pad_lowrank(w2[0]), 'a0': a0[0], 'a2p': _pad_lowrank(a2[0]),
        'k_k': k_k, 'k_a': k_a, 'r_k': r_k.reshape(1, D_RWKV),
        'ln_x_g': ln_x_g, 'ln_x_b': ln_x_b, 'g2': g2[0].astype(BF16),
        'norm2_g': norm2_g, 'ffn_dw9': ffn_dw[0].reshape(9, 2 * D_FF),
        'ffn_dw3': ffn_dw[0, 1], 'ffn_dw_b': ffn_dw_b,
        'final_g': final_g.reshape(1, D_MODEL),
    }

    def ctx_cond(b):
        return 0

    def lat_cond(b):
        return b + 1

    seq = x_prompt.shape[1]
    u, rws = _pre(x_prompt, mods, ctx_cond, p, tile=seq, has_halo=False)
    y, sfin, w_out_b, ffn_up_b, ffn_down_b = _wkv(
        rws, None, p, tile=seq, has_sout=True, side_casts=(w_out[0], ffn_up[0], ffn_down[0]))
    p = dict(p, w_out=w_out_b, ffn_up=ffn_up_b, ffn_down=ffn_down_b)
    bsz = x_prompt.shape[0]

    def pair(t):
        return t.reshape(bsz // 2, 2 * seq, t.shape[-1])

    x1 = _mix(pair(x_prompt), pair(y), pair(rws), pair(u), mods, ctx_cond, p, tile=2 * seq)
    y_prompt = _ffn(x1.reshape(x_prompt.shape), mods, ctx_cond, p, tile=seq, grid_conv=False,
                    period=seq)
    new_state = sfin[:, None].astype(state_wkv.dtype)

    s0_bd = state_wkv[:, 0].astype(F32)
    u, rws = _pre(x_sample, mods, lat_cond, p, tile=512, has_halo=True)
    (y,) = _wkv(rws, s0_bd, p, tile=256, has_sout=False)
    x1 = _mix(x_sample, y, rws, u, mods, lat_cond, p, tile=512)
    y_sample = _ffn(x1, mods, lat_cond, p, tile=512, grid_conv=True, period=GRID_W)

    return (y_prompt, y_sample, new_state)
```

```python
import functools
import math

import jax
import jax.numpy as jnp
from jax import lax
from jax.experimental import pallas as pl
from jax.experimental.pallas import tpu as pltpu

F32 = jnp.float32
BF16 = jnp.bfloat16

D_MODEL = 1024
D_CONV = 512
D_RWKV = 512
HEAD = 64
N_RHEADS = 8
CONV_W = 31
R_LO = 64
R_G = 128
D_FF = 2560
RW_COLS = 3 * D_RWKV + 4 * R_LO + R_G
P_IN = 2 * D_CONV + RW_COLS
GRID_W = 64
EPS_RMS = 1e-6
EPS_LN = 1e-5
EPS_GN = 64e-5
L2_EPS = 1e-24
EXP_M05 = math.exp(-0.5)

CHUNK = 64
GROUP = 256
HEADS_PER_GROUP = GROUP // HEAD
N_GROUPS = D_RWKV // GROUP
CONV_HALO = 16
assert CONV_HALO - CONV_W // 2 == 1
FFN_HALO = GRID_W
VMEM_LIMIT = 56 * 1024 * 1024

MIX_TILE = 1024
WKV_UNROLL = 2

NN = (((1,), (0,)), ((), ()))
NT = (((1,), (1,)), ((), ()))


def _dot(a, b, dims=NN):
    return lax.dot_general(a, b, dims, preferred_element_type=F32)


def _split2(x):
    hi = x.astype(BF16)
    lo = (x - hi.astype(F32)).astype(BF16)
    return hi, lo


def _bdot(a, b, dims=NN):
    return _dot(a.astype(BF16), b.astype(BF16), dims)


def _sigmoid(x):
    return jax.nn.sigmoid(x)


def _head_ones():
    r = lax.broadcasted_iota(jnp.int32, (GROUP, GROUP), 0) // HEAD
    c = lax.broadcasted_iota(jnp.int32, (GROUP, GROUP), 1) // HEAD
    return jnp.where(r == c, 1.0, 0.0).astype(BF16)


def _head_sum(x, ones_bd):
    t = x.shape[0]
    pieces = []
    for g in range(N_GROUPS):
        pieces.extend(_split2(x[:, g * GROUP:(g + 1) * GROUP]))
    s = _dot(jnp.concatenate(pieces, axis=0), ones_bd)
    return jnp.concatenate([s[(2 * g) * t:(2 * g + 1) * t] + s[(2 * g + 1) * t:(2 * g + 2) * t]
                            for g in range(N_GROUPS)], axis=1)


def _rmsnorm(x, g):
    ms = jnp.mean(x * x, axis=-1, keepdims=True)
    return x * lax.rsqrt(ms + EPS_RMS) * g


def _adaln_kernel(c_ref, w_ref, b_ref, o_ref):
    c = c_ref[...]
    s = (c * _sigmoid(c)).astype(BF16)
    o_ref[...] = _dot(s, w_ref[...].astype(BF16)) + b_ref[...]


def _adaln(cond8, ada_w, ada_b):
    n_out = ada_w.shape[1]
    bn = 1024
    return pl.pallas_call(
        _adaln_kernel,
        grid=(n_out // bn,),
        in_specs=[
            pl.BlockSpec((8, D_MODEL), lambda j: (0, 0)),
            pl.BlockSpec((D_MODEL, bn), lambda j: (0, j)),
            pl.BlockSpec((1, bn), lambda j: (0, j)),
        ],
        out_specs=pl.BlockSpec((8, bn), lambda j: (0, j)),
        out_shape=jax.ShapeDtypeStruct((8, n_out), F32),
        compiler_params=pltpu.CompilerParams(dimension_semantics=("parallel",)),
        name="adaln",
    )(cond8, ada_w, ada_b)


def _pre_kernel(*refs, tile, n_tiles, has_halo):
    if has_halo:
        (x_ref, xp_ref, xn_ref, mods_ref, n1g_ref, win_ref, sk_ref, cdw_ref, cdb_ref,
         lng_ref, lnb_ref, u_ref, rws_ref, ubuf, rwbuf, pbuf) = refs
    else:
        (x_ref, mods_ref, n1g_ref, win_ref, sk_ref, cdw_ref, cdb_ref,
         lng_ref, lnb_ref, u_ref, rws_ref, ubuf, rwbuf, pbuf) = refs
    i = pl.program_id(1)
    sh1 = mods_ref[0, 0:1, :]
    sc1 = mods_ref[0, 1:2, :]

    if has_halo:
        x = jnp.concatenate([xp_ref[0], x_ref[0], xn_ref[0]], axis=0)
    else:
        x = x_ref[0]
    h = _rmsnorm(x, n1g_ref[...]) * (1.0 + sc1) + sh1
    proj = _dot(h.astype(BF16), win_ref[...])
    cv = proj[:, 0:D_CONV]
    cg = proj[:, D_CONV:2 * D_CONV]
    rw = proj[:, 2 * D_CONV:]
    u0 = cv * _sigmoid(cg)

    if has_halo:
        pv = jnp.where(i > 0, 1.0, 0.0).astype(F32)
        nv = jnp.where(i < n_tiles - 1, 1.0, 0.0).astype(F32)
        ubuf[0:CONV_HALO, :] = u0[0:CONV_HALO] * pv
        ubuf[CONV_HALO:CONV_HALO + tile, :] = u0[CONV_HALO:CONV_HALO + tile]
        ubuf[CONV_HALO + tile:, :] = u0[CONV_HALO + tile:] * nv
        rwbuf[0:8, :] = rw[CONV_HALO - 8:CONV_HALO] * pv
        rwbuf[8:8 + tile, :] = rw[CONV_HALO:CONV_HALO + tile]
        rwbuf[8 + tile:, :] = rw[CONV_HALO + tile:CONV_HALO + tile + 8] * nv
    else:
        ubuf[0:CONV_HALO, :] = jnp.zeros((CONV_HALO, D_CONV), F32)
        ubuf[CONV_HALO:CONV_HALO + tile, :] = u0
        ubuf[CONV_HALO + tile:, :] = jnp.zeros((CONV_HALO, D_CONV), F32)
        rwbuf[0:8, :] = jnp.zeros((8, RW_COLS), F32)
        rwbuf[8:8 + tile, :] = rw
        rwbuf[8 + tile:, :] = jnp.zeros((8, RW_COLS), F32)

    ext = tile + 8
    for s in range(8):
        ps = None
        for m in range(4):
            j = 8 * m + s - 1
            if 0 <= j < CONV_W:
                term = ubuf[pl.ds(8 * m, ext), :] * cdw_ref[j:j + 1, :]
                ps = term if ps is None else ps + term
        pbuf[s] = ps
    acc = pbuf[0, 0:tile, :] + cdb_ref[...]
    for s in range(1, 8):
        acc = acc + pbuf[s, pl.ds(s, tile), :]
    mu = jnp.mean(acc, axis=-1, keepdims=True)
    dv = acc - mu
    var = jnp.mean(dv * dv, axis=-1, keepdims=True)
    ln = dv * lax.rsqrt(var + EPS_LN) * lng_ref[...] + lnb_ref[...]
    u_ref[0] = (ln * _sigmoid(ln)).astype(BF16)

    rws = rwbuf[pl.ds(7, tile), :] * sk_ref[0:1, :]
    rws = rws + rwbuf[pl.ds(8, tile), :] * sk_ref[1:2, :]
    rws = rws + rwbuf[pl.ds(9, tile), :] * sk_ref[2:3, :]
    rws_ref[0] = rws


def _const_spec(shape):
    nd = len(shape)
    return pl.BlockSpec(shape, lambda b, i: (0,) * nd)


def _pre(x, mods, cond_of_b, p, tile, has_halo):
    bsz, seq, _ = x.shape
    n_tiles = seq // tile
    hb = tile // CONV_HALO
    n_hblk = seq // CONV_HALO
    in_specs = [pl.BlockSpec((1, tile, D_MODEL), lambda b, i: (b, i, 0))]
    args = [x]
    if has_halo:
        in_specs += [
            pl.BlockSpec((1, CONV_HALO, D_MODEL), lambda b, i: (b, jnp.maximum(i * hb - 1, 0), 0)),
            pl.BlockSpec((1, CONV_HALO, D_MODEL),
                         lambda b, i: (b, jnp.minimum((i + 1) * hb, n_hblk - 1), 0)),
        ]
        args += [x, x]
    in_specs += [
        pl.BlockSpec((1, 6, D_MODEL), lambda b, i: (cond_of_b(b), 0, 0)),
        _const_spec((1, D_MODEL)),
        _const_spec((D_MODEL, P_IN)),
        _const_spec((3, RW_COLS)),
        _const_spec((32, D_CONV)),
        _const_spec((1, D_CONV)),
        _const_spec((1, D_CONV)),
        _const_spec((1, D_CONV)),
    ]
    args += [mods, p['norm1_g'], p['w_in'], p['shift_k'], p['conv_dw'], p['conv_dw_b'],
             p['conv_ln_g'], p['conv_ln_b']]
    return pl.pallas_call(
        functools.partial(_pre_kernel, tile=tile, n_tiles=n_tiles, has_halo=has_halo),
        grid=(bsz, n_tiles),
        in_specs=in_specs,
        out_specs=[
            pl.BlockSpec((1, tile, D_CONV), lambda b, i: (b, i, 0)),
            pl.BlockSpec((1, tile, RW_COLS), lambda b, i: (b, i, 0)),
        ],
        out_shape=[
            jax.ShapeDtypeStruct((bsz, seq, D_CONV), BF16),
            jax.ShapeDtypeStruct((bsz, seq, RW_COLS), F32),
        ],
        scratch_shapes=[
            pltpu.VMEM((tile + 2 * CONV_HALO, D_CONV), F32),
            pltpu.VMEM((tile + 16, RW_COLS), F32),
            pltpu.VMEM((8, tile + 8, D_CONV), F32),
        ],
        compiler_params=pltpu.CompilerParams(
            dimension_semantics=("parallel", "parallel"), vmem_limit_bytes=VMEM_LIMIT),
        name="pre_halo" if has_halo else "pre",
    )(*args)


def _wkv_kernel(*refs, tile, n_tiles, has_s0, has_sout, n_casts):
    merged = n_tiles == 1
    refs = list(refs)
    if merged:
        rf_ref = rb_ref = refs[0]
        pos = 1
    else:
        rf_ref, rb_ref = refs[0], refs[1]
        pos = 2
    s0_ref = None
    if has_s0:
        s0_ref = refs[pos]
        pos += 1
    w0_ref, w2_ref, a0_ref, a2_ref, kk_ref, ka_ref = refs[pos:pos + 6]
    pos += 6
    cast_in = refs[pos:pos + n_casts]
    pos += n_casts
    y_ref = refs[pos]
    pos += 1
    sout_ref = None
    if has_sout:
        sout_ref = refs[pos]
        pos += 1
    cast_out = refs[pos:pos + n_casts]
    pos += n_casts
    (mst,) = refs[pos:]

    for src, dst in zip(cast_in, cast_out):
        dst[...] = src[...].astype(BF16)

    i = pl.program_id(1)
    n_chunks = tile // CHUNK

    lane_head = lax.broadcasted_iota(jnp.int32, (1, GROUP), 1) // HEAD
    head_masks = [jnp.where(lane_head == hh, 1.0, 0.0).astype(BF16)
                  for hh in range(HEADS_PER_GROUP)]
    t_idx = lax.broadcasted_iota(jnp.int32, (CHUNK, GROUP), 0)
    s_idx = lax.broadcasted_iota(jnp.int32, (CHUNK, GROUP), 1) % CHUNK
    eye_a = jnp.where(s_idx == t_idx, 1.0, 0.0).astype(F32)

    def bd(x):
        xb = x.astype(BF16)
        return jnp.concatenate([xb * m for m in head_masks], axis=0)

    def block_transpose(x):
        xt = x.T
        return jnp.concatenate([xt[hh * HEAD:(hh + 1) * HEAD, :] for hh in range(HEADS_PER_GROUP)],
                               axis=1)

    @pl.when(i == 0)
    def _init():
        if has_s0:
            for d in range(2):
                for g in range(N_GROUPS):
                    heads = [s0_ref[0, d, g * HEADS_PER_GROUP + hh] for hh in range(HEADS_PER_GROUP)]
                    mst[d, g] = jnp.concatenate(heads, axis=0).T
        else:
            mst[...] = jnp.zeros(mst.shape, F32)
        y_ref[...] = jnp.zeros(y_ref.shape, F32)

    ones_bd = _head_ones()
    slab = WKV_UNROLL * CHUNK
    row = lax.broadcasted_iota(jnp.int32, (slab, slab), 0)
    col = lax.broadcasted_iota(jnp.int32, (slab, slab), 1)
    same_chunk = (row // CHUNK) == (col // CHUNK)
    tri = [jnp.where(same_chunk & (col <= row), 1.0, 0.0).astype(BF16),
           jnp.where(same_chunk & (col >= row), 1.0, 0.0).astype(BF16)]
    strict = [jnp.where(s_idx < t_idx, 1.0, 0.0).astype(F32),
              jnp.where(s_idx > t_idx, 1.0, 0.0).astype(F32)]
    incl = [jnp.where(s_idx <= t_idx, 1.0, 0.0).astype(F32),
            jnp.where(s_idx >= t_idx, 1.0, 0.0).astype(F32)]

    chains = [(d, g, u) for u in range(WKV_UNROLL) for d in range(2) for g in range(N_GROUPS)]

    def loop_step(it, carry):
        slab0 = [pl.multiple_of(it * slab, slab), pl.multiple_of(tile - (it + 1) * slab, slab)]

        def chunk_lo(d, u):
            return (u if d == 0 else WKV_UNROLL - 1 - u) * CHUNK

        kt, q, kh, bh, khp, bhp, v, epc = ({} for _ in range(8))
        for d in range(2):
            src = rf_ref if d == 0 else rb_ref
            srows = pl.ds(slab0[d], slab)
            r_t = src[0, srows, 0:D_RWKV]
            k_t = src[0, srows, D_RWKV:2 * D_RWKV]
            v_t = src[0, srows, 2 * D_RWKV:3 * D_RWKV]
            wlo = src[0, srows, 3 * D_RWKV:3 * D_RWKV + 2 * R_LO]
            alo = src[0, srows, 3 * D_RWKV + 2 * R_LO:3 * D_RWKV + 4 * R_LO]
            kk = k_t * kk_ref[...]
            ss = _head_sum(kk * kk, ones_bd)
            kk = kk * lax.rsqrt(jnp.maximum(ss, L2_EPS))
            w_raw = w0_ref[d:d + 1, :] + _dot(jnp.tanh(wlo).astype(BF16), w2_ref[d])
            logw = -_sigmoid(w_raw) * EXP_M05
            a = _sigmoid(a0_ref[d:d + 1, :] + _dot(alo.astype(BF16), a2_ref[d]))
            kdir = k_t * (1.0 + (a - 1.0) * ka_ref[...])
            bvec = kk * a
            lw_hi, lw_lo = _split2(logw)
            cum = _dot(tri[d], lw_hi) + _dot(tri[d], lw_lo)
            edge = CHUNK - 1 if d == 0 else 0
            tot = jnp.concatenate(
                [jnp.broadcast_to(cum[c * CHUNK + edge:c * CHUNK + edge + 1, :], (CHUNK, D_RWKV))
                 for c in range(WKV_UNROLL)], axis=0)
            e_in = jnp.exp(cum)
            e_neg = jnp.exp(-cum)
            e_tot = jnp.exp(tot)
            e_rem = e_tot * e_neg
            full = {'kt': kk * jnp.exp(cum - logw), 'q': r_t * e_in, 'kh': kdir * e_neg,
                    'bh': bvec * e_neg, 'khp': kdir * e_rem, 'bhp': bvec * e_rem, 'v': v_t,
                    'epc': e_tot}
            for g in range(N_GROUPS):
                lanes = slice(g * GROUP, (g + 1) * GROUP)
                for u in range(WKV_UNROLL):
                    lo = chunk_lo(d, u)
                    ch = (d, g, u)
                    for dst, name in ((kt, 'kt'), (q, 'q'), (kh, 'kh'), (bh, 'bh'), (khp, 'khp'),
                                      (bhp, 'bhp'), (v, 'v')):
                        dst[ch] = full[name][lo:lo + CHUNK, lanes]
                    epc[ch] = full['epc'][lo:lo + 1, lanes]

        def stack(*xs):
            return jnp.concatenate([x.astype(BF16) for x in xs], axis=0)

        lhs2 = {ch: stack(kt[ch], q[ch]) for ch in chains}
        a1 = {ch: _dot(lhs2[ch], bd(kh[ch]), NT) for ch in chains}
        a2 = {ch: _dot(lhs2[ch], bd(bh[ch]), NT) for ch in chains}
        a_ak = {ch: a1[ch][0:CHUNK] * strict[ch[0]] for ch in chains}
        a_qk = {ch: a1[ch][CHUNK:] * incl[ch[0]] for ch in chains}
        a_ab = {ch: a2[ch][0:CHUNK] * strict[ch[0]] for ch in chains}
        a_qb = {ch: a2[ch][CHUNK:] * incl[ch[0]] for ch in chains}

        khp_t = {ch: block_transpose(khp[ch]) for ch in chains}
        bhp_t = {ch: block_transpose(bhp[ch]) for ch in chains}

        t_inv = {ch: eye_a - a_ab[ch] for ch in chains}
        n_pow = {ch: _bdot(a_ab[ch], bd(a_ab[ch])) for ch in chains}
        for _ in range(4):
            both = {ch: _dot(stack(t_inv[ch], n_pow[ch]), bd(n_pow[ch])) for ch in chains}
            t_inv = {ch: t_inv[ch] + both[ch][0:CHUNK] for ch in chains}
            n_pow = {ch: both[ch][CHUNK:] for ch in chains}
        t_inv = {ch: t_inv[ch] + _bdot(t_inv[ch], bd(n_pow[ch])) for ch in chains}

        on_v = {ch: _dot(stack(a_ak[ch], a_qk[ch], khp_t[ch]), bd(v[ch])) for ch in chains}
        av = {ch: on_v[ch][0:CHUNK] for ch in chains}
        lhs_qb = {ch: _dot(stack(a_qb[ch], bhp_t[ch]), bd(t_inv[ch])).astype(BF16) for ch in chains}
        on_ktp = {ch: _dot(lhs_qb[ch], bd(kt[ch])) for ch in chains}
        on_vp = {ch: _dot(lhs_qb[ch], bd(av[ch])) for ch in chains}
        qp = {ch: q[ch] - on_ktp[ch][0:CHUNK] for ch in chains}
        yp = {ch: on_v[ch][CHUNK:2 * CHUNK] - on_vp[ch][0:CHUNK] for ch in chains}
        gp = {ch: eye_a * epc[ch] - on_ktp[ch][CHUNK:] for ch in chains}
        hp = {ch: on_v[ch][2 * CHUNK:] - on_vp[ch][CHUNK:] for ch in chains}

        tile0 = [i * tile, (n_tiles - 1 - i) * tile]
        for d in range(2):
            for g in range(N_GROUPS):
                m = mst[d, g]
                for u in range(WKV_UNROLL):
                    ch = (d, g, u)
                    on_m = _dot(stack(qp[ch], gp[ch]), bd(m))
                    out_rows = pl.ds(pl.multiple_of(tile0[d] + slab0[d] + chunk_lo(d, u), CHUNK), CHUNK)
                    y_ref[0, out_rows, g * GROUP:(g + 1) * GROUP] += on_m[0:CHUNK] + yp[ch]
                    m = on_m[CHUNK:] + hp[ch]
                mst[d, g] = m
        return carry

    lax.fori_loop(0, n_chunks // WKV_UNROLL, loop_step, 0)

    if has_sout:
        @pl.when(i == n_tiles - 1)
        def _fin():
            for d in range(2):
                for g in range(N_GROUPS):
                    st = mst[d, g].T
                    for hh in range(HEADS_PER_GROUP):
                        sout_ref[0, d, g * HEADS_PER_GROUP + hh] = st[hh * HEAD:(hh + 1) * HEAD, :]


def _wkv(rws, s0_bd, p, tile, has_sout, side_casts=()):
    bsz, seq, _ = rws.shape
    n_tiles = seq // tile
    has_s0 = s0_bd is not None
    merged = n_tiles == 1
    in_specs = [pl.BlockSpec((1, tile, RW_COLS), lambda b, i: (b, i, 0))]
    args = [rws]
    if not merged:
        in_specs.append(pl.BlockSpec((1, tile, RW_COLS), lambda b, i: (b, n_tiles - 1 - i, 0)))
        args.append(rws)
    if has_s0:
        in_specs.append(pl.BlockSpec((1, 2, N_RHEADS, HEAD, HEAD), lambda b, i: (b, 0, 0, 0, 0)))
        args.append(s0_bd)
    in_specs += [
        _const_spec((2, D_RWKV)),
        _const_spec((2, 2 * R_LO, D_RWKV)),
        _const_spec((2, D_RWKV)),
        _const_spec((2, 2 * R_LO, D_RWKV)),
        _const_spec((1, D_RWKV)),
        _const_spec((1, D_RWKV)),
    ]
    args += [p['w0'], p['w2p'], p['a0'], p['a2p'], p['k_k'], p['k_a']]
    out_specs = [pl.BlockSpec((1, seq, D_RWKV), lambda b, i: (b, 0, 0))]
    out_shape = [jax.ShapeDtypeStruct((bsz, seq, D_RWKV), F32)]
    if has_sout:
        out_specs.append(pl.BlockSpec((1, 2, N_RHEADS, HEAD, HEAD), lambda b, i: (b, 0, 0, 0, 0)))
        out_shape.append(jax.ShapeDtypeStruct((bsz, 2, N_RHEADS, HEAD, HEAD), F32))
    n_steps = bsz * n_tiles
    for w in side_casts:
        slab_rows = w.shape[0] // n_steps
        assert slab_rows * n_steps == w.shape[0] and slab_rows % 16 == 0
        spec = pl.BlockSpec((slab_rows, w.shape[1]), lambda b, i: (b * n_tiles + i, 0))
        in_specs.append(spec)
        args.append(w)
        out_specs.append(spec)
        out_shape.append(jax.ShapeDtypeStruct(w.shape, BF16))
    return pl.pallas_call(
        functools.partial(_wkv_kernel, tile=tile, n_tiles=n_tiles, has_s0=has_s0,
                          has_sout=has_sout, n_casts=len(side_casts)),
        grid=(bsz, n_tiles),
        in_specs=in_specs,
        out_specs=out_specs,
        out_shape=out_shape,
        scratch_shapes=[pltpu.VMEM((2, N_GROUPS, HEAD, GROUP), F32)],
        compiler_params=pltpu.CompilerParams(
            dimension_semantics=("arbitrary", "arbitrary"), vmem_limit_bytes=VMEM_LIMIT),
        name="wkv_s0" if has_s0 else "wkv",
    )(*args)


def _mix_kernel(x_ref, y_ref, rkv_ref, glo_ref, u_ref, mods_ref, rk_ref, lxg_ref, lxb_ref,
                g2_ref, wout_ref, x1_ref):
    ones_bd = _head_ones()
    g1 = mods_ref[0, 2:3, :]
    o = y_ref[0]
    mu = _head_sum(o, ones_bd) * (1.0 / HEAD)
    dv = o - mu
    var = _head_sum(dv * dv, ones_bd) * (1.0 / HEAD)
    on = dv * lax.rsqrt(var + EPS_GN) * lxg_ref[...] + lxb_ref[...]
    r = rkv_ref[0, :, 0:D_RWKV]
    k = rkv_ref[0, :, D_RWKV:2 * D_RWKV]
    v = rkv_ref[0, :, 2 * D_RWKV:3 * D_RWKV]
    glo = glo_ref[0]
    bonus = _head_sum(r * k * rk_ref[...], ones_bd) * v
    gate = _dot(_sigmoid(glo).astype(BF16), g2_ref[...])
    o2 = (on + bonus) * gate
    mixed = _dot(u_ref[0].astype(BF16), wout_ref[0:D_CONV, :])
    mixed = mixed + _dot(o2.astype(BF16), wout_ref[D_CONV:, :])
    x1_ref[0] = x_ref[0] + g1 * mixed


def _mix(x, y, rws, u, mods, cond_of_b, p, tile):
    bsz, seq, _ = x.shape
    n_tiles = seq // tile

    def tok(width, col_block=0):
        return pl.BlockSpec((1, tile, width), lambda b, i: (b, i, col_block))

    glo_block = (RW_COLS - R_G) // R_G
    return pl.pallas_call(
        _mix_kernel,
        grid=(bsz, n_tiles),
        in_specs=[
            tok(D_MODEL), tok(D_RWKV), tok(3 * D_RWKV), tok(R_G, glo_block), tok(D_CONV),
            pl.BlockSpec((1, 6, D_MODEL), lambda b, i: (cond_of_b(b), 0, 0)),
            _const_spec((1, D_RWKV)),
            _const_spec((1, D_RWKV)),
            _const_spec((1, D_RWKV)),
            _const_spec((R_G, D_RWKV)),
            _const_spec((D_MODEL, D_MODEL)),
        ],
        out_specs=tok(D_MODEL),
        out_shape=jax.ShapeDtypeStruct((bsz, seq, D_MODEL), F32),
        compiler_params=pltpu.CompilerParams(
            dimension_semantics=("parallel", "parallel"), vmem_limit_bytes=VMEM_LIMIT),
        name="mix",
    )(x, y, rws, rws, u, mods, p['r_k'], p['ln_x_g'], p['ln_x_b'], p['g2'], p['w_out'])


FFN_COLS = 256
FFN_PAD = 8


def _ffn_kernel(*refs, tile, n_tiles, grid_conv, period):
    if grid_conv:
        (x_ref, xp_ref, xn_ref, mods_ref, n2g_ref, up_ref, dw_ref, dwb_ref, down_ref, fg_ref,
         o_ref, vbuf, gbuf, vl, vr, gl, gr, act_s) = refs
    else:
        (x_ref, mods_ref, n2g_ref, up_ref, dw_ref, dwb_ref, down_ref, fg_ref,
         o_ref, vbuf, gbuf, vl, vr, gl, gr, act_s) = refs
    i = pl.program_id(1)
    sh2 = mods_ref[0, 3:4, :]
    sc2 = mods_ref[0, 4:5, :]
    g2 = mods_ref[0, 5:6, :]
    halo = FFN_HALO if grid_conv else 0
    if grid_conv:
        xe = jnp.concatenate([xp_ref[0], x_ref[0], xn_ref[0]], axis=0)
    else:
        xe = x_ref[0]
    rows_e = tile + 2 * halo
    h2 = _rmsnorm(xe, n2g_ref[...]) * (1.0 + sc2) + sh2

    if grid_conv:
        e_idx = lax.broadcasted_iota(jnp.int32, (rows_e, 1), 0)
        pv = jnp.where(i > 0, 1.0, 0.0).astype(F32)
        nv = jnp.where(i < n_tiles - 1, 1.0, 0.0).astype(F32)
        h2 = h2 * (jnp.where(e_idx >= halo, 1.0, pv) * jnp.where(e_idx < halo + tile, 1.0, nv))
        taps = [(dr, dc) for dr in (-1, 0, 1) for dc in (-1, 0, 1)]
    else:
        taps = [(0, -1), (0, 0), (0, 1)]
    masked = period < tile
    if masked:
        colpos = lax.broadcasted_iota(jnp.int32, (tile, 1), 0) % period
        m_left = jnp.where(colpos >= 1, 1.0, 0.0)
        m_right = jnp.where(colpos <= period - 2, 1.0, 0.0)

    zpad = jnp.zeros((FFN_PAD, FFN_COLS), F32)
    vbuf[0:FFN_PAD, :] = zpad
    vbuf[FFN_PAD + rows_e:, :] = zpad
    gbuf[0:FFN_PAD, :] = zpad
    gbuf[FFN_PAD + rows_e:, :] = zpad

    def conv(buf, side_l, side_r, col0):
        ext = tile + 2 * FFN_PAD
        sums = {}
        for ti, (dr, dc) in enumerate(taps):
            win = buf[pl.ds(halo + GRID_W * dr, ext), :]
            term = win * dw_ref[ti:ti + 1, col0:col0 + FFN_COLS]
            sums[dc] = term if dc not in sums else sums[dc] + term
        side_l[...] = sums[-1]
        side_r[...] = sums[1]
        left = side_l[pl.ds(FFN_PAD - 1, tile), :]
        right = side_r[pl.ds(FFN_PAD + 1, tile), :]
        if masked:
            left = left * m_left
            right = right * m_right
        centre = sums[0][FFN_PAD:FFN_PAD + tile]
        return centre + left + right + dwb_ref[:, col0:col0 + FFN_COLS]

    hb = h2.astype(BF16)
    for cc in range(D_FF // FFN_COLS):
        c0 = cc * FFN_COLS
        vbuf[FFN_PAD:FFN_PAD + rows_e, :] = _dot(hb, up_ref[:, c0:c0 + FFN_COLS])
        gbuf[FFN_PAD:FFN_PAD + rows_e, :] = _dot(hb, up_ref[:, D_FF + c0:D_FF + c0 + FFN_COLS])
        val = conv(vbuf, vl, vr, c0)
        gate = conv(gbuf, gl, gr, D_FF + c0)
        act_s[:, c0:c0 + FFN_COLS] = (val * (gate * _sigmoid(gate))).astype(BF16)

    x2 = x_ref[0] + g2 * _dot(act_s[...], down_ref[...])
    o_ref[0] = _rmsnorm(x2, fg_ref[...])


def _ffn(x1, mods, cond_of_b, p, tile, grid_conv, period):
    bsz, seq, _ = x1.shape
    n_tiles = seq // tile
    in_specs = [pl.BlockSpec((1, tile, D_MODEL), lambda b, i: (b, i, 0))]
    args = [x1]
    if grid_conv:
        hb = tile // FFN_HALO
        n_hblk = seq // FFN_HALO
        in_specs += [
            pl.BlockSpec((1, FFN_HALO, D_MODEL), lambda b, i: (b, jnp.maximum(i * hb - 1, 0), 0)),
            pl.BlockSpec((1, FFN_HALO, D_MODEL),
                         lambda b, i: (b, jnp.minimum((i + 1) * hb, n_hblk - 1), 0)),
        ]
        args += [x1, x1]
        dw = p['ffn_dw9']
    else:
        dw = p['ffn_dw3']
    n_taps = dw.shape[0]
    in_specs += [
        pl.BlockSpec((1, 6, D_MODEL), lambda b, i: (cond_of_b(b), 0, 0)),
        _const_spec((1, D_MODEL)),
        _const_spec((D_MODEL, 2 * D_FF)),
        _const_spec((n_taps, 2 * D_FF)),
        _const_spec((1, 2 * D_FF)),
        _const_spec((D_FF, D_MODEL)),
        _const_spec((1, D_MODEL)),
    ]
    args += [mods, p['norm2_g'], p['ffn_up'], dw, p['ffn_dw_b'], p['ffn_down'], p['final_g']]
    halo = FFN_HALO if grid_conv else 0
    buf_rows = tile + 2 * halo + 2 * FFN_PAD
    return pl.pallas_call(
        functools.partial(_ffn_kernel, tile=tile, n_tiles=n_tiles, grid_conv=grid_conv,
                          period=period),
        grid=(bsz, n_tiles),
        in_specs=in_specs,
        out_specs=pl.BlockSpec((1, tile, D_MODEL), lambda b, i: (b, i, 0)),
        out_shape=jax.ShapeDtypeStruct((bsz, seq, D_MODEL), F32),
        scratch_shapes=[pltpu.VMEM((buf_rows, FFN_COLS), F32),
                        pltpu.VMEM((buf_rows, FFN_COLS), F32)]
        + [pltpu.VMEM((tile + 2 * FFN_PAD, FFN_COLS), F32) for _ in range(4)]
        + [pltpu.VMEM((tile, D_FF), BF16)],
        compiler_params=pltpu.CompilerParams(
            dimension_semantics=("parallel", "parallel"), vmem_limit_bytes=VMEM_LIMIT),
        name="ffn_grid" if grid_conv else "ffn",
    )(*args)


def _pad_lowrank(w):
    z = jnp.zeros_like(w[0])
    return jnp.stack([jnp.concatenate([w[0], z], axis=0),
                      jnp.concatenate([z, w[1]], axis=0)]).astype(BF16)


def kernel(x_prompt, x_sample, state_wkv, c, c_ctx, ada_w, ada_b, norm1_g, w_in, shift_k, conv_dw, conv_dw_b, conv_ln_g, conv_ln_b, w0, w2, a0, a2, g2, k_k, k_a, r_k, ln_x_g, ln_x_b, w_out, norm2_g, ffn_up, ffn_dw, ffn_dw_b, ffn_down, final_g):
    assert ada_w.shape[0] == 1, "single layer"
    dec_b = x_sample.shape[0]
    cond8 = jnp.zeros((8, D_MODEL), F32).at[0].set(c_ctx).at[1:1 + dec_b].set(c)
    mods = _adaln(cond8, ada_w[0], ada_b).reshape(8, 6, D_MODEL)

    p = {
        'norm1_g': norm1_g, 'w_in': w_in[0].astype(BF16), 'shift_k': shift_k[0],
        'conv_dw': jnp.concatenate([conv_dw[0], jnp.zeros((1, D_CONV), F32)], axis=0),
        'conv_dw_b': conv_dw_b, 'conv_ln_g': conv_ln_g, 'conv_ln_b': conv_ln_b,
        'w0': w0[0], 'w2p': _pad_lowrank(w2[0]), 'a0': a0[0], 'a2p': _pad_lowrank(a2[0]),
        'k_k': k_k, 'k_a': k_a, 'r_k': r_k.reshape(1, D_RWKV),
        'ln_x_g': ln_x_g, 'ln_x_b': ln_x_b, 'g2': g2[0].astype(BF16),
        'norm2_g': norm2_g, 'ffn_dw9': ffn_dw[0].reshape(9, 2 * D_FF),
        'ffn_dw3': ffn_dw[0, 1], 'ffn_dw_b': ffn_dw_b,
        'final_g': final_g.reshape(1, D_MODEL),
    }

    def ctx_cond(b):
        return 0

    def lat_cond(b):
        return b + 1

    seq = x_prompt.shape[1]
    u, rws = _pre(x_prompt, mods, ctx_cond, p, tile=seq, has_halo=False)
    y, sfin, w_out_b, ffn_up_b, ffn_down_b = _wkv(
        rws, None, p, tile=seq, has_sout=True, side_casts=(w_out[0], ffn_up[0], ffn_down[0]))
    p = dict(p, w_out=w_out_b, ffn_up=ffn_up_b, ffn_down=ffn_down_b)
    bsz = x_prompt.shape[0]
    n_join = MIX_TILE // seq

    def pair(t):
        return t.reshape(bsz // n_join, MIX_TILE, t.shape[-1])

    x1 = _mix(pair(x_prompt), pair(y), pair(rws), pair(u), mods, ctx_cond, p, tile=MIX_TILE)
    y_prompt = _ffn(x1.reshape(x_prompt.shape), mods, ctx_cond, p, tile=seq, grid_conv=False,
                    period=seq)
    new_state = sfin[:, None].astype(state_wkv.dtype)

    s0_bd = state_wkv[:, 0].astype(F32)
    u, rws = _pre(x_sample, mods, lat_cond, p, tile=512, has_halo=True)
    (y,) = _wkv(rws, s0_bd, p, tile=256, has_sout=False)
    x1 = _mix(x_sample, y, rws, u, mods, lat_cond, p, tile=MIX_TILE)
    y_sample = _ffn(x1, mods, lat_cond, p, tile=512, grid_conv=True, period=GRID_W)

    return (y_prompt, y_sample, new_state)
```

```python
import functools
import math

import jax
import jax.numpy as jnp
from jax import lax
from jax.experimental import pallas as pl
from jax.experimental.pallas import tpu as pltpu

F32 = jnp.float32
BF16 = jnp.bfloat16

D_MODEL = 1024
D_CONV = 512
D_RWKV = 512
HEAD = 64
N_RHEADS = 8
CONV_W = 31
R_LO = 64
R_G = 128
D_FF = 2560
RW_COLS = 3 * D_RWKV + 4 * R_LO + R_G
P_IN = 2 * D_CONV + RW_COLS
GRID_W = 64
EPS_RMS = 1e-6
EPS_LN = 1e-5
EPS_GN = 64e-5
L2_EPS = 1e-24
EXP_M05 = math.exp(-0.5)

CHUNK = 64
GROUP = 256
HEADS_PER_GROUP = GROUP // HEAD
N_GROUPS = D_RWKV // GROUP
CONV_HALO = 16
assert CONV_HALO - CONV_W // 2 == 1
FFN_HALO = GRID_W
VMEM_LIMIT = 56 * 1024 * 1024

MIX_TILE = 1024
WKV_UNROLL = 2

NN = (((1,), (0,)), ((), ()))
NT = (((1,), (1,)), ((), ()))


def _dot(a, b, dims=NN):
    return lax.dot_general(a, b, dims, preferred_element_type=F32)


def _split2(x):
    hi = x.astype(BF16)
    lo = (x - hi.astype(F32)).astype(BF16)
    return hi, lo


def _bdot(a, b, dims=NN):
    return _dot(a.astype(BF16), b.astype(BF16), dims)


def _sigmoid(x):
    return jax.nn.sigmoid(x)


def _head_ones():
    r = lax.broadcasted_iota(jnp.int32, (GROUP, GROUP), 0) // HEAD
    c = lax.broadcasted_iota(jnp.int32, (GROUP, GROUP), 1) // HEAD
    return jnp.where(r == c, 1.0, 0.0).astype(BF16)


def _head_sum(x, ones_bd):
    t = x.shape[0]
    pieces = []
    for g in range(N_GROUPS):
        pieces.extend(_split2(x[:, g * GROUP:(g + 1) * GROUP]))
    s = _dot(jnp.concatenate(pieces, axis=0), ones_bd)
    return jnp.concatenate([s[(2 * g) * t:(2 * g + 1) * t] + s[(2 * g + 1) * t:(2 * g + 2) * t]
                            for g in range(N_GROUPS)], axis=1)


def _rmsnorm(x, g):
    ms = jnp.mean(x * x, axis=-1, keepdims=True)
    return x * lax.rsqrt(ms + EPS_RMS) * g


def _adaln_kernel(c_ref, w_ref, b_ref, win_ref, o_ref, win_b_ref):
    c = c_ref[...]
    s = (c * _sigmoid(c)).astype(BF16)
    o_ref[...] = _dot(s, w_ref[...].astype(BF16)) + b_ref[...]
    win_b_ref[...] = win_ref[...].astype(BF16)


def _adaln(cond8, ada_w, ada_b, w_in):
    n_out = ada_w.shape[1]
    n_steps = 8
    bn = n_out // n_steps
    slab = w_in.shape[0] // n_steps
    assert bn * n_steps == n_out and bn % 128 == 0
    assert slab * n_steps == w_in.shape[0] and slab % 16 == 0
    return pl.pallas_call(
        _adaln_kernel,
        grid=(n_steps,),
        in_specs=[
            pl.BlockSpec((8, D_MODEL), lambda j: (0, 0)),
            pl.BlockSpec((D_MODEL, bn), lambda j: (0, j)),
            pl.BlockSpec((1, bn), lambda j: (0, j)),
            pl.BlockSpec((slab, w_in.shape[1]), lambda j: (j, 0)),
        ],
        out_specs=[pl.BlockSpec((8, bn), lambda j: (0, j)),
                   pl.BlockSpec((slab, w_in.shape[1]), lambda j: (j, 0))],
        out_shape=[jax.ShapeDtypeStruct((8, n_out), F32),
                   jax.ShapeDtypeStruct(w_in.shape, BF16)],
        compiler_params=pltpu.CompilerParams(dimension_semantics=("parallel",)),
        name="adaln",
    )(cond8, ada_w, ada_b, w_in)


def _pre_kernel(*refs, tile, n_tiles, has_halo):
    if has_halo:
        (x_ref, xp_ref, xn_ref, mods_ref, n1g_ref, win_ref, sk_ref, cdw_ref, cdb_ref,
         lng_ref, lnb_ref, u_ref, rws_ref, ubuf, rwbuf, pbuf) = refs
    else:
        (x_ref, mods_ref, n1g_ref, win_ref, sk_ref, cdw_ref, cdb_ref,
         lng_ref, lnb_ref, u_ref, rws_ref, ubuf, rwbuf, pbuf) = refs
    i = pl.program_id(1)
    sh1 = mods_ref[0, 0:1, :]
    sc1 = mods_ref[0, 1:2, :]

    if has_halo:
        x = jnp.concatenate([xp_ref[0], x_ref[0], xn_ref[0]], axis=0)
    else:
        x = x_ref[0]
    h = _rmsnorm(x, n1g_ref[...]) * (1.0 + sc1) + sh1
    proj = _dot(h.astype(BF16), win_ref[...])
    cv = proj[:, 0:D_CONV]
    cg = proj[:, D_CONV:2 * D_CONV]
    rw = proj[:, 2 * D_CONV:]
    u0 = cv * _sigmoid(cg)

    if has_halo:
        pv = jnp.where(i > 0, 1.0, 0.0).astype(F32)
        nv = jnp.where(i < n_tiles - 1, 1.0, 0.0).astype(F32)
        ubuf[0:CONV_HALO, :] = u0[0:CONV_HALO] * pv
        ubuf[CONV_HALO:CONV_HALO + tile, :] = u0[CONV_HALO:CONV_HALO + tile]
        ubuf[CONV_HALO + tile:, :] = u0[CONV_HALO + tile:] * nv
        rwbuf[0:8, :] = rw[CONV_HALO - 8:CONV_HALO] * pv
        rwbuf[8:8 + tile, :] = rw[CONV_HALO:CONV_HALO + tile]
        rwbuf[8 + tile:, :] = rw[CONV_HALO + tile:CONV_HALO + tile + 8] * nv
    else:
        ubuf[0:CONV_HALO, :] = jnp.zeros((CONV_HALO, D_CONV), F32)
        ubuf[CONV_HALO:CONV_HALO + tile, :] = u0
        ubuf[CONV_HALO + tile:, :] = jnp.zeros((CONV_HALO, D_CONV), F32)
        rwbuf[0:8, :] = jnp.zeros((8, RW_COLS), F32)
        rwbuf[8:8 + tile, :] = rw
        rwbuf[8 + tile:, :] = jnp.zeros((8, RW_COLS), F32)

    ext = tile + 8
    for s in range(8):
        ps = None
        for m in range(4):
            j = 8 * m + s - 1
            if 0 <= j < CONV_W:
                term = ubuf[pl.ds(8 * m, ext), :] * cdw_ref[j:j + 1, :]
                ps = term if ps is None else ps + term
        pbuf[s] = ps
    acc = pbuf[0, 0:tile, :] + cdb_ref[...]
    for s in range(1, 8):
        acc = acc + pbuf[s, pl.ds(s, tile), :]
    mu = jnp.mean(acc, axis=-1, keepdims=True)
    dv = acc - mu
    var = jnp.mean(dv * dv, axis=-1, keepdims=True)
    ln = dv * lax.rsqrt(var + EPS_LN) * lng_ref[...] + lnb_ref[...]
    u_ref[0] = (ln * _sigmoid(ln)).astype(BF16)

    rws = rwbuf[pl.ds(7, tile), :] * sk_ref[0:1, :]
    rws = rws + rwbuf[pl.ds(8, tile), :] * sk_ref[1:2, :]
    rws = rws + rwbuf[pl.ds(9, tile), :] * sk_ref[2:3, :]
    rws_ref[0] = rws


def _const_spec(shape):
    nd = len(shape)
    return pl.BlockSpec(shape, lambda b, i: (0,) * nd)


def _pre(x, mods, cond_of_b, p, tile, has_halo):
    bsz, seq, _ = x.shape
    n_tiles = seq // tile
    hb = tile // CONV_HALO
    n_hblk = seq // CONV_HALO
    in_specs = [pl.BlockSpec((1, tile, D_MODEL), lambda b, i: (b, i, 0))]
    args = [x]
    if has_halo:
        in_specs += [
            pl.BlockSpec((1, CONV_HALO, D_MODEL), lambda b, i: (b, jnp.maximum(i * hb - 1, 0), 0)),
            pl.BlockSpec((1, CONV_HALO, D_MODEL),
                         lambda b, i: (b, jnp.minimum((i + 1) * hb, n_hblk - 1), 0)),
        ]
        args += [x, x]
    in_specs += [
        pl.BlockSpec((1, 6, D_MODEL), lambda b, i: (cond_of_b(b), 0, 0)),
        _const_spec((1, D_MODEL)),
        _const_spec((D_MODEL, P_IN)),
        _const_spec((3, RW_COLS)),
        _const_spec((32, D_CONV)),
        _const_spec((1, D_CONV)),
        _const_spec((1, D_CONV)),
        _const_spec((1, D_CONV)),
    ]
    args += [mods, p['norm1_g'], p['w_in'], p['shift_k'], p['conv_dw'], p['conv_dw_b'],
             p['conv_ln_g'], p['conv_ln_b']]
    return pl.pallas_call(
        functools.partial(_pre_kernel, tile=tile, n_tiles=n_tiles, has_halo=has_halo),
        grid=(bsz, n_tiles),
        in_specs=in_specs,
        out_specs=[
            pl.BlockSpec((1, tile, D_CONV), lambda b, i: (b, i, 0)),
            pl.BlockSpec((1, tile, RW_COLS), lambda b, i: (b, i, 0)),
        ],
        out_shape=[
            jax.ShapeDtypeStruct((bsz, seq, D_CONV), BF16),
            jax.ShapeDtypeStruct((bsz, seq, RW_COLS), F32),
        ],
        scratch_shapes=[
            pltpu.VMEM((tile + 2 * CONV_HALO, D_CONV), F32),
            pltpu.VMEM((tile + 16, RW_COLS), F32),
            pltpu.VMEM((8, tile + 8, D_CONV), F32),
        ],
        compiler_params=pltpu.CompilerParams(
            dimension_semantics=("parallel", "parallel"), vmem_limit_bytes=VMEM_LIMIT),
        name="pre_halo" if has_halo else "pre",
    )(*args)


def _wkv_kernel(*refs, tile, n_tiles, has_s0, has_sout, n_casts):
    merged = n_tiles == 1
    refs = list(refs)
    if merged:
        rf_ref = rb_ref = refs[0]
        pos = 1
    else:
        rf_ref, rb_ref = refs[0], refs[1]
        pos = 2
    s0_ref = None
    if has_s0:
        s0_ref = refs[pos]
        pos += 1
    w0_ref, w2_ref, a0_ref, a2_ref, kk_ref, ka_ref = refs[pos:pos + 6]
    pos += 6
    cast_in = refs[pos:pos + n_casts]
    pos += n_casts
    y_ref = refs[pos]
    pos += 1
    sout_ref = None
    if has_sout:
        sout_ref = refs[pos]
        pos += 1
    cast_out = refs[pos:pos + n_casts]
    pos += n_casts
    (mst,) = refs[pos:]

    for src, dst in zip(cast_in, cast_out):
        dst[...] = src[...].astype(BF16)

    i = pl.program_id(1)
    n_chunks = tile // CHUNK

    lane_head = lax.broadcasted_iota(jnp.int32, (1, GROUP), 1) // HEAD
    head_masks = [jnp.where(lane_head == hh, 1.0, 0.0).astype(BF16)
                  for hh in range(HEADS_PER_GROUP)]
    t_idx = lax.broadcasted_iota(jnp.int32, (CHUNK, GROUP), 0)
    s_idx = lax.broadcasted_iota(jnp.int32, (CHUNK, GROUP), 1) % CHUNK
    eye_a = jnp.where(s_idx == t_idx, 1.0, 0.0).astype(F32)

    def bd(x):
        xb = x.astype(BF16)
        return jnp.concatenate([xb * m for m in head_masks], axis=0)

    def block_transpose(x):
        xt = x.T
        return jnp.concatenate([xt[hh * HEAD:(hh + 1) * HEAD, :] for hh in range(HEADS_PER_GROUP)],
                               axis=1)

    @pl.when(i == 0)
    def _init():
        if has_s0:
            for d in range(2):
                for g in range(N_GROUPS):
                    heads = [s0_ref[0, d, g * HEADS_PER_GROUP + hh] for hh in range(HEADS_PER_GROUP)]
                    mst[d, g] = jnp.concatenate(heads, axis=0).T
        else:
            mst[...] = jnp.zeros(mst.shape, F32)
        y_ref[...] = jnp.zeros(y_ref.shape, F32)

    ones_bd = _head_ones()
    slab = WKV_UNROLL * CHUNK
    row = lax.broadcasted_iota(jnp.int32, (slab, slab), 0)
    col = lax.broadcasted_iota(jnp.int32, (slab, slab), 1)
    same_chunk = (row // CHUNK) == (col // CHUNK)
    tri = [jnp.where(same_chunk & (col <= row), 1.0, 0.0).astype(BF16),
           jnp.where(same_chunk & (col >= row), 1.0, 0.0).astype(BF16)]
    strict = [jnp.where(s_idx < t_idx, 1.0, 0.0).astype(F32),
              jnp.where(s_idx > t_idx, 1.0, 0.0).astype(F32)]
    incl = [jnp.where(s_idx <= t_idx, 1.0, 0.0).astype(F32),
            jnp.where(s_idx >= t_idx, 1.0, 0.0).astype(F32)]

    chains = [(d, g, u) for u in range(WKV_UNROLL) for d in range(2) for g in range(N_GROUPS)]

    def loop_step(it, carry):
        slab0 = [pl.multiple_of(it * slab, slab), pl.multiple_of(tile - (it + 1) * slab, slab)]

        def chunk_lo(d, u):
            return (u if d == 0 else WKV_UNROLL - 1 - u) * CHUNK

        kt, q, kh, bh, khp, bhp, v, epc = ({} for _ in range(8))
        for d in range(2):
            src = rf_ref if d == 0 else rb_ref
            srows = pl.ds(slab0[d], slab)
            r_t = src[0, srows, 0:D_RWKV]
            k_t = src[0, srows, D_RWKV:2 * D_RWKV]
            v_t = src[0, srows, 2 * D_RWKV:3 * D_RWKV]
            wlo = src[0, srows, 3 * D_RWKV:3 * D_RWKV + 2 * R_LO]
            alo = src[0, srows, 3 * D_RWKV + 2 * R_LO:3 * D_RWKV + 4 * R_LO]
            kk = k_t * kk_ref[...]
            ss = _head_sum(kk * kk, ones_bd)
            kk = kk * lax.rsqrt(jnp.maximum(ss, L2_EPS))
            w_raw = w0_ref[d:d + 1, :] + _dot(jnp.tanh(wlo).astype(BF16), w2_ref[d])
            logw = -_sigmoid(w_raw) * EXP_M05
            a = _sigmoid(a0_ref[d:d + 1, :] + _dot(alo.astype(BF16), a2_ref[d]))
            kdir = k_t * (1.0 + (a - 1.0) * ka_ref[...])
            bvec = kk * a
            lw_hi, lw_lo = _split2(logw)
            cum = _dot(tri[d], lw_hi) + _dot(tri[d], lw_lo)
            edge = CHUNK - 1 if d == 0 else 0
            tot = jnp.concatenate(
                [jnp.broadcast_to(cum[c * CHUNK + edge:c * CHUNK + edge + 1, :], (CHUNK, D_RWKV))
                 for c in range(WKV_UNROLL)], axis=0)
            e_in = jnp.exp(cum)
            e_neg = jnp.exp(-cum)
            e_tot = jnp.exp(tot)
            e_rem = e_tot * e_neg
            full = {'kt': kk * jnp.exp(cum - logw), 'q': r_t * e_in, 'kh': kdir * e_neg,
                    'bh': bvec * e_neg, 'khp': kdir * e_rem, 'bhp': bvec * e_rem, 'v': v_t,
                    'epc': e_tot}
            for g in range(N_GROUPS):
                lanes = slice(g * GROUP, (g + 1) * GROUP)
                for u in range(WKV_UNROLL):
                    lo = chunk_lo(d, u)
                    ch = (d, g, u)
                    for dst, name in ((kt, 'kt'), (q, 'q'), (kh, 'kh'), (bh, 'bh'), (khp, 'khp'),
                                      (bhp, 'bhp'), (v, 'v')):
                        dst[ch] = full[name][lo:lo + CHUNK, lanes]
                    epc[ch] = full['epc'][lo:lo + 1, lanes]

        def stack(*xs):
            return jnp.concatenate([x.astype(BF16) for x in xs], axis=0)

        lhs2 = {ch: stack(kt[ch], q[ch]) for ch in chains}
        a1 = {ch: _dot(lhs2[ch], bd(kh[ch]), NT) for ch in chains}
        a2 = {ch: _dot(lhs2[ch], bd(bh[ch]), NT) for ch in chains}
        a_ak = {ch: a1[ch][0:CHUNK] * strict[ch[0]] for ch in chains}
        a_qk = {ch: a1[ch][CHUNK:] * incl[ch[0]] for ch in chains}
        a_ab = {ch: a2[ch][0:CHUNK] * strict[ch[0]] for ch in chains}
        a_qb = {ch: a2[ch][CHUNK:] * incl[ch[0]] for ch in chains}

        khp_t = {ch: block_transpose(khp[ch]) for ch in chains}
        bhp_t = {ch: block_transpose(bhp[ch]) for ch in chains}

        t_inv = {ch: eye_a - a_ab[ch] for ch in chains}
        n_pow = {ch: _bdot(a_ab[ch], bd(a_ab[ch])) for ch in chains}
        for _ in range(4):
            both = {ch: _dot(stack(t_inv[ch], n_pow[ch]), bd(n_pow[ch])) for ch in chains}
            t_inv = {ch: t_inv[ch] + both[ch][0:CHUNK] for ch in chains}
            n_pow = {ch: both[ch][CHUNK:] for ch in chains}
        t_inv = {ch: t_inv[ch] + _bdot(t_inv[ch], bd(n_pow[ch])) for ch in chains}

        on_v = {ch: _dot(stack(a_ak[ch], a_qk[ch], khp_t[ch]), bd(v[ch])) for ch in chains}
        av = {ch: on_v[ch][0:CHUNK] for ch in chains}
        lhs_qb = {ch: _dot(stack(a_qb[ch], bhp_t[ch]), bd(t_inv[ch])).astype(BF16) for ch in chains}
        on_ktp = {ch: _dot(lhs_qb[ch], bd(kt[ch])) for ch in chains}
        on_vp = {ch: _dot(lhs_qb[ch], bd(av[ch])) for ch in chains}
        qp = {ch: q[ch] - on_ktp[ch][0:CHUNK] for ch in chains}
        yp = {ch: on_v[ch][CHUNK:2 * CHUNK] - on_vp[ch][0:CHUNK] for ch in chains}
        gp = {ch: eye_a * epc[ch] - on_ktp[ch][CHUNK:] for ch in chains}
        hp = {ch: on_v[ch][2 * CHUNK:] - on_vp[ch][CHUNK:] for ch in chains}

        tile0 = [i * tile, (n_tiles - 1 - i) * tile]
        for d in range(2):
            for g in range(N_GROUPS):
                m = mst[d, g]
                for u in range(WKV_UNROLL):
                    ch = (d, g, u)
                    on_m = _dot(stack(qp[ch], gp[ch]), bd(m))
                    out_rows = pl.ds(pl.multiple_of(tile0[d] + slab0[d] + chunk_lo(d, u), CHUNK), CHUNK)
                    y_ref[0, out_rows, g * GROUP:(g + 1) * GROUP] += on_m[0:CHUNK] + yp[ch]
                    m = on_m[CHUNK:] + hp[ch]
                mst[d, g] = m
        return carry

    lax.fori_loop(0, n_chunks // WKV_UNROLL, loop_step, 0)

    if has_sout:
        @pl.when(i == n_tiles - 1)
        def _fin():
            for d in range(2):
                for g in range(N_GROUPS):
                    st = mst[d, g].T
                    for hh in range(HEADS_PER_GROUP):
                        sout_ref[0, d, g * HEADS_PER_GROUP + hh] = st[hh * HEAD:(hh + 1) * HEAD, :]


def _wkv(rws, s0_bd, p, tile, has_sout, side_casts=()):
    bsz, seq, _ = rws.shape
    n_tiles = seq // tile
    has_s0 = s0_bd is not None
    merged = n_tiles == 1
    in_specs = [pl.BlockSpec((1, tile, RW_COLS), lambda b, i: (b, i, 0))]
    args = [rws]
    if not merged:
        in_specs.append(pl.BlockSpec((1, tile, RW_COLS), lambda b, i: (b, n_tiles - 1 - i, 0)))
        args.append(rws)
    if has_s0:
        in_specs.append(pl.BlockSpec((1, 2, N_RHEADS, HEAD, HEAD), lambda b, i: (b, 0, 0, 0, 0)))
        args.append(s0_bd)
    in_specs += [
        _const_spec((2, D_RWKV)),
        _const_spec((2, 2 * R_LO, D_RWKV)),
        _const_spec((2, D_RWKV)),
        _const_spec((2, 2 * R_LO, D_RWKV)),
        _const_spec((1, D_RWKV)),
        _const_spec((1, D_RWKV)),
    ]
    args += [p['w0'], p['w2p'], p['a0'], p['a2p'], p['k_k'], p['k_a']]
    out_specs = [pl.BlockSpec((1, seq, D_RWKV), lambda b, i: (b, 0, 0))]
    out_shape = [jax.ShapeDtypeStruct((bsz, seq, D_RWKV), F32)]
    if has_sout:
        out_specs.append(pl.BlockSpec((1, 2, N_RHEADS, HEAD, HEAD), lambda b, i: (b, 0, 0, 0, 0)))
        out_shape.append(jax.ShapeDtypeStruct((bsz, 2, N_RHEADS, HEAD, HEAD), F32))
    n_steps = bsz * n_tiles
    for w in side_casts:
        slab_rows = w.shape[0] // n_steps
        assert slab_rows * n_steps == w.shape[0] and slab_rows % 16 == 0
        spec = pl.BlockSpec((slab_rows, w.shape[1]), lambda b, i: (b * n_tiles + i, 0))
        in_specs.append(spec)
        args.append(w)
        out_specs.append(spec)
        out_shape.append(jax.ShapeDtypeStruct(w.shape, BF16))
    return pl.pallas_call(
        functools.partial(_wkv_kernel, tile=tile, n_tiles=n_tiles, has_s0=has_s0,
                          has_sout=has_sout, n_casts=len(side_casts)),
        grid=(bsz, n_tiles),
        in_specs=in_specs,
        out_specs=out_specs,
        out_shape=out_shape,
        scratch_shapes=[pltpu.VMEM((2, N_GROUPS, HEAD, GROUP), F32)],
        compiler_params=pltpu.CompilerParams(
            dimension_semantics=("arbitrary", "arbitrary"), vmem_limit_bytes=VMEM_LIMIT),
        name="wkv_s0" if has_s0 else "wkv",
    )(*args)


def _mix_kernel(x_ref, y_ref, rkv_ref, glo_ref, u_ref, mods_ref, rk_ref, lxg_ref, lxb_ref,
                g2_ref, wout_ref, x1_ref):
    ones_bd = _head_ones()
    g1 = mods_ref[0, 2:3, :]
    o = y_ref[0]
    mu = _head_sum(o, ones_bd) * (1.0 / HEAD)
    dv = o - mu
    var = _head_sum(dv * dv, ones_bd) * (1.0 / HEAD)
    on = dv * lax.rsqrt(var + EPS_GN) * lxg_ref[...] + lxb_ref[...]
    r = rkv_ref[0, :, 0:D_RWKV]
    k = rkv_ref[0, :, D_RWKV:2 * D_RWKV]
    v = rkv_ref[0, :, 2 * D_RWKV:3 * D_RWKV]
    glo = glo_ref[0]
    bonus = _head_sum(r * k * rk_ref[...], ones_bd) * v
    gate = _dot(_sigmoid(glo).astype(BF16), g2_ref[...])
    o2 = (on + bonus) * gate
    mixed = _dot(u_ref[0].astype(BF16), wout_ref[0:D_CONV, :])
    mixed = mixed + _dot(o2.astype(BF16), wout_ref[D_CONV:, :])
    x1_ref[0] = x_ref[0] + g1 * mixed


def _mix(x, y, rws, u, mods, cond_of_b, p, tile):
    bsz, seq, _ = x.shape
    n_tiles = seq // tile

    def tok(width, col_block=0):
        return pl.BlockSpec((1, tile, width), lambda b, i: (b, i, col_block))

    glo_block = (RW_COLS - R_G) // R_G
    return pl.pallas_call(
        _mix_kernel,
        grid=(bsz, n_tiles),
        in_specs=[
            tok(D_MODEL), tok(D_RWKV), tok(3 * D_RWKV), tok(R_G, glo_block), tok(D_CONV),
            pl.BlockSpec((1, 6, D_MODEL), lambda b, i: (cond_of_b(b), 0, 0)),
            _const_spec((1, D_RWKV)),
            _const_spec((1, D_RWKV)),
            _const_spec((1, D_RWKV)),
            _const_spec((R_G, D_RWKV)),
            _const_spec((D_MODEL, D_MODEL)),
        ],
        out_specs=tok(D_MODEL),
        out_shape=jax.ShapeDtypeStruct((bsz, seq, D_MODEL), F32),
        compiler_params=pltpu.CompilerParams(
            dimension_semantics=("parallel", "parallel"), vmem_limit_bytes=VMEM_LIMIT),
        name="mix",
    )(x, y, rws, rws, u, mods, p['r_k'], p['ln_x_g'], p['ln_x_b'], p['g2'], p['w_out'])


FFN_COLS = 256
FFN_PAD = 8


def _ffn_kernel(*refs, tile, n_tiles, grid_conv, period):
    if grid_conv:
        (x_ref, xp_ref, xn_ref, mods_ref, n2g_ref, up_ref, dw_ref, dwb_ref, down_ref, fg_ref,
         o_ref, vbuf, gbuf, vl, vr, gl, gr, act_s) = refs
    else:
        (x_ref, mods_ref, n2g_ref, up_ref, dw_ref, dwb_ref, down_ref, fg_ref,
         o_ref, vbuf, gbuf, vl, vr, gl, gr, act_s) = refs
    i = pl.program_id(1)
    sh2 = mods_ref[0, 3:4, :]
    sc2 = mods_ref[0, 4:5, :]
    g2 = mods_ref[0, 5:6, :]
    halo = FFN_HALO if grid_conv else 0
    if grid_conv:
        xe = jnp.concatenate([xp_ref[0], x_ref[0], xn_ref[0]], axis=0)
    else:
        xe = x_ref[0]
    rows_e = tile + 2 * halo
    h2 = _rmsnorm(xe, n2g_ref[...]) * (1.0 + sc2) + sh2

    if grid_conv:
        e_idx = lax.broadcasted_iota(jnp.int32, (rows_e, 1), 0)
        pv = jnp.where(i > 0, 1.0, 0.0).astype(F32)
        nv = jnp.where(i < n_tiles - 1, 1.0, 0.0).astype(F32)
        h2 = h2 * (jnp.where(e_idx >= halo, 1.0, pv) * jnp.where(e_idx < halo + tile, 1.0, nv))
        taps = [(dr, dc) for dr in (-1, 0, 1) for dc in (-1, 0, 1)]
    else:
        taps = [(0, -1), (0, 0), (0, 1)]
    masked = period < tile
    if masked:
        colpos = lax.broadcasted_iota(jnp.int32, (tile, 1), 0) % period
        m_left = jnp.where(colpos >= 1, 1.0, 0.0)
        m_right = jnp.where(colpos <= period - 2, 1.0, 0.0)

    zpad = jnp.zeros((FFN_PAD, FFN_COLS), F32)
    vbuf[0:FFN_PAD, :] = zpad
    vbuf[FFN_PAD + rows_e:, :] = zpad
    gbuf[0:FFN_PAD, :] = zpad
    gbuf[FFN_PAD + rows_e:, :] = zpad

    def conv(buf, side_l, side_r, col0):
        ext = tile + 2 * FFN_PAD
        sums = {}
        for ti, (dr, dc) in enumerate(taps):
            win = buf[pl.ds(halo + GRID_W * dr, ext), :]
            term = win * dw_ref[ti:ti + 1, col0:col0 + FFN_COLS]
            sums[dc] = term if dc not in sums else sums[dc] + term
        side_l[...] = sums[-1]
        side_r[...] = sums[1]
        left = side_l[pl.ds(FFN_PAD - 1, tile), :]
        right = side_r[pl.ds(FFN_PAD + 1, tile), :]
        if masked:
            left = left * m_left
            right = right * m_right
        centre = sums[0][FFN_PAD:FFN_PAD + tile]
        return centre + left + right + dwb_ref[:, col0:col0 + FFN_COLS]

    hb = h2.astype(BF16)
    for cc in range(D_FF // FFN_COLS):
        c0 = cc * FFN_COLS
        vbuf[FFN_PAD:FFN_PAD + rows_e, :] = _dot(hb, up_ref[:, c0:c0 + FFN_COLS])
        gbuf[FFN_PAD:FFN_PAD + rows_e, :] = _dot(hb, up_ref[:, D_FF + c0:D_FF + c0 + FFN_COLS])
        val = conv(vbuf, vl, vr, c0)
        gate = conv(gbuf, gl, gr, D_FF + c0)
        act_s[:, c0:c0 + FFN_COLS] = (val * (gate * _sigmoid(gate))).astype(BF16)

    x2 = x_ref[0] + g2 * _dot(act_s[...], down_ref[...])
    o_ref[0] = _rmsnorm(x2, fg_ref[...])


def _ffn(x1, mods, cond_of_b, p, tile, grid_conv, period):
    bsz, seq, _ = x1.shape
    n_tiles = seq // tile
    in_specs = [pl.BlockSpec((1, tile, D_MODEL), lambda b, i: (b, i, 0))]
    args = [x1]
    if grid_conv:
        hb = tile // FFN_HALO
        n_hblk = seq // FFN_HALO
        in_specs += [
            pl.BlockSpec((1, FFN_HALO, D_MODEL), lambda b, i: (b, jnp.maximum(i * hb - 1, 0), 0)),
            pl.BlockSpec((1, FFN_HALO, D_MODEL),
                         lambda b, i: (b, jnp.minimum((i + 1) * hb, n_hblk - 1), 0)),
        ]
        args += [x1, x1]
        dw = p['ffn_dw9']
    else:
        dw = p['ffn_dw3']
    n_taps = dw.shape[0]
    in_specs += [
        pl.BlockSpec((1, 6, D_MODEL), lambda b, i: (cond_of_b(b), 0, 0)),
        _const_spec((1, D_MODEL)),
        _const_spec((D_MODEL, 2 * D_FF)),
        _const_spec((n_taps, 2 * D_FF)),
        _const_spec((1, 2 * D_FF)),
        _const_spec((D_FF, D_MODEL)),
        _const_spec((1, D_MODEL)),
    ]
    args += [mods, p['norm2_g'], p['ffn_up'], dw, p['ffn_dw_b'], p['ffn_down'], p['final_g']]
    halo = FFN_HALO if grid_conv else 0
    buf_rows = tile + 2 * halo + 2 * FFN_PAD
    return pl.pallas_call(
        functools.partial(_ffn_kernel, tile=tile, n_tiles=n_tiles, grid_conv=grid_conv,
                          period=period),
        grid=(bsz, n_tiles),
        in_specs=in_specs,
        out_specs=pl.BlockSpec((1, tile, D_MODEL), lambda b, i: (b, i, 0)),
        out_shape=jax.ShapeDtypeStruct((bsz, seq, D_MODEL), F32),
        scratch_shapes=[pltpu.VMEM((buf_rows, FFN_COLS), F32),
                        pltpu.VMEM((buf_rows, FFN_COLS), F32)]
        + [pltpu.VMEM((tile + 2 * FFN_PAD, FFN_COLS), F32) for _ in range(4)]
        + [pltpu.VMEM((tile, D_FF), BF16)],
        compiler_params=pltpu.CompilerParams(
            dimension_semantics=("parallel", "parallel"), vmem_limit_bytes=VMEM_LIMIT),
        name="ffn_grid" if grid_conv else "ffn",
    )(*args)


def _pad_lowrank(w):
    z = jnp.zeros_like(w[0])
    return jnp.stack([jnp.concatenate([w[0], z], axis=0),
                      jnp.concatenate([z, w[1]], axis=0)]).astype(BF16)


def kernel(x_prompt, x_sample, state_wkv, c, c_ctx, ada_w, ada_b, norm1_g, w_in, shift_k, conv_dw, conv_dw_b, conv_ln_g, conv_ln_b, w0, w2, a0, a2, g2, k_k, k_a, r_k, ln_x_g, ln_x_b, w_out, norm2_g, ffn_up, ffn_dw, ffn_dw_b, ffn_down, final_g):
    assert ada_w.shape[0] == 1, "single layer"
    dec_b = x_sample.shape[0]
    cond8 = jnp.zeros((8, D_MODEL), F32).at[0].set(c_ctx).at[1:1 + dec_b].set(c)
    mods, w_in_b = _adaln(cond8, ada_w[0], ada_b, w_in[0])
    mods = mods.reshape(8, 6, D_MODEL)

    p = {
        'norm1_g': norm1_g, 'w_in': w_in_b, 'shift_k': shift_k[0],
        'conv_dw': jnp.concatenate([conv_dw[0], jnp.zeros((1, D_CONV), F32)], axis=0),
        'conv_dw_b': conv_dw_b, 'conv_ln_g': conv_ln_g, 'conv_ln_b': conv_ln_b,
        'w0': w0[0], 'w2p': _pad_lowrank(w2[0]), 'a0': a0[0], 'a2p': _pad_lowrank(a2[0]),
        'k_k': k_k, 'k_a': k_a, 'r_k': r_k.reshape(1, D_RWKV),
        'ln_x_g': ln_x_g, 'ln_x_b': ln_x_b, 'g2': g2[0].astype(BF16),
        'norm2_g': norm2_g, 'ffn_dw9': ffn_dw[0].reshape(9, 2 * D_FF),
        'ffn_dw3': ffn_dw[0, 1], 'ffn_dw_b': ffn_dw_b,
        'final_g': final_g.reshape(1, D_MODEL),
    }

    def ctx_cond(b):
        return 0

    def lat_cond(b):
        return b + 1

    seq = x_prompt.shape[1]
    u, rws = _pre(x_prompt, mods, ctx_cond, p, tile=seq, has_halo=False)
    y, sfin, w_out_b, ffn_up_b, ffn_down_b = _wkv(
        rws, None, p, tile=seq, has_sout=True, side_casts=(w_out[0], ffn_up[0], ffn_down[0]))
    p = dict(p, w_out=w_out_b, ffn_up=ffn_up_b, ffn_down=ffn_down_b)
    bsz = x_prompt.shape[0]
    n_join = MIX_TILE // seq

    def pair(t):
        return t.reshape(bsz // n_join, MIX_TILE, t.shape[-1])

    x1 = _mix(pair(x_prompt), pair(y), pair(rws), pair(u), mods, ctx_cond, p, tile=MIX_TILE)
    y_prompt = _ffn(x1.reshape(x_prompt.shape), mods, ctx_cond, p, tile=seq, grid_conv=False,
                    period=seq)
    new_state = sfin[:, None].astype(state_wkv.dtype)

    s0_bd = state_wkv[:, 0].astype(F32)
    u, rws = _pre(x_sample, mods, lat_cond, p, tile=512, has_halo=True)
    (y,) = _wkv(rws, s0_bd, p, tile=256, has_sout=False)
    x1 = _mix(x_sample, y, rws, u, mods, lat_cond, p, tile=MIX_TILE)
    y_sample = _ffn(x1, mods, lat_cond, p, tile=512, grid_conv=True, period=GRID_W)

    return (y_prompt, y_sample, new_state)
```

```python
import functools
import math

import jax
import jax.numpy as jnp
from jax import lax
from jax.experimental import pallas as pl
from jax.experimental.pallas import tpu as pltpu

F32 = jnp.float32
BF16 = jnp.bfloat16

D_MODEL = 1024
D_CONV = 512
D_RWKV = 512
HEAD = 64
N_RHEADS = 8
CONV_W = 31
R_LO = 64
R_G = 128
D_FF = 2560
RW_COLS = 3 * D_RWKV + 4 * R_LO + R_G
P_IN = 2 * D_CONV + RW_COLS
GRID_W = 64
EPS_RMS = 1e-6
EPS_LN = 1e-5
EPS_GN = 64e-5
L2_EPS = 1e-24
EXP_M05 = math.exp(-0.5)

CHUNK = 64
GROUP = 256
HEADS_PER_GROUP = GROUP // HEAD
N_GROUPS = D_RWKV // GROUP
CONV_HALO = 16
assert CONV_HALO - CONV_W // 2 == 1
FFN_HALO = GRID_W
VMEM_LIMIT = 56 * 1024 * 1024

MIX_TILE = 1024
WKV_UNROLL = 2

NN = (((1,), (0,)), ((), ()))
NT = (((1,), (1,)), ((), ()))


def _dot(a, b, dims=NN):
    return lax.dot_general(a, b, dims, preferred_element_type=F32)


def _split2(x):
    hi = x.astype(BF16)
    lo = (x - hi.astype(F32)).astype(BF16)
    return hi, lo


def _bdot(a, b, dims=NN):
    return _dot(a.astype(BF16), b.astype(BF16), dims)


def _sigmoid(x):
    return jax.nn.sigmoid(x)


def _head_ones():
    r = lax.broadcasted_iota(jnp.int32, (GROUP, GROUP), 0) // HEAD
    c = lax.broadcasted_iota(jnp.int32, (GROUP, GROUP), 1) // HEAD
    return jnp.where(r == c, 1.0, 0.0).astype(BF16)


def _head_sum(x, ones_bd):
    t = x.shape[0]
    pieces = []
    for g in range(N_GROUPS):
        pieces.extend(_split2(x[:, g * GROUP:(g + 1) * GROUP]))
    s = _dot(jnp.concatenate(pieces, axis=0), ones_bd)
    return jnp.concatenate([s[(2 * g) * t:(2 * g + 1) * t] + s[(2 * g + 1) * t:(2 * g + 2) * t]
                            for g in range(N_GROUPS)], axis=1)


def _rmsnorm(x, g):
    ms = jnp.mean(x * x, axis=-1, keepdims=True)
    return x * lax.rsqrt(ms + EPS_RMS) * g


def _adaln_kernel(c_ref, w_ref, b_ref, win_ref, o_ref, win_b_ref):
    c = c_ref[...]
    s = (c * _sigmoid(c)).astype(BF16)
    o_ref[...] = _dot(s, w_ref[...].astype(BF16)) + b_ref[...]
    win_b_ref[...] = win_ref[...].astype(BF16)


def _adaln(cond8, ada_w, ada_b, w_in):
    n_out = ada_w.shape[1]
    n_steps = 8
    bn = n_out // n_steps
    slab = w_in.shape[0] // n_steps
    assert bn * n_steps == n_out and bn % 128 == 0
    assert slab * n_steps == w_in.shape[0] and slab % 16 == 0
    return pl.pallas_call(
        _adaln_kernel,
        grid=(n_steps,),
        in_specs=[
            pl.BlockSpec((8, D_MODEL), lambda j: (0, 0)),
            pl.BlockSpec((D_MODEL, bn), lambda j: (0, j)),
            pl.BlockSpec((1, bn), lambda j: (0, j)),
            pl.BlockSpec((slab, w_in.shape[1]), lambda j: (j, 0)),
        ],
        out_specs=[pl.BlockSpec((8, bn), lambda j: (0, j)),
                   pl.BlockSpec((slab, w_in.shape[1]), lambda j: (j, 0))],
        out_shape=[jax.ShapeDtypeStruct((8, n_out), F32),
                   jax.ShapeDtypeStruct(w_in.shape, BF16)],
        compiler_params=pltpu.CompilerParams(dimension_semantics=("parallel",)),
        name="adaln",
    )(cond8, ada_w, ada_b, w_in)


def _pre_kernel(*refs, tile, n_tiles, has_halo):
    if has_halo:
        (x_ref, xp_ref, xn_ref, mods_ref, n1g_ref, win_ref, sk_ref, cdw_ref, cdb_ref,
         lng_ref, lnb_ref, u_ref, rws_ref, ubuf, rwbuf, pbuf) = refs
    else:
        (x_ref, mods_ref, n1g_ref, win_ref, sk_ref, cdw_ref, cdb_ref,
         lng_ref, lnb_ref, u_ref, rws_ref, ubuf, rwbuf, pbuf) = refs
    i = pl.program_id(1)
    sh1 = mods_ref[0, 0:1, :]
    sc1 = mods_ref[0, 1:2, :]

    if has_halo:
        x = jnp.concatenate([xp_ref[0], x_ref[0], xn_ref[0]], axis=0)
    else:
        x = x_ref[0]
    h = _rmsnorm(x, n1g_ref[...]) * (1.0 + sc1) + sh1
    proj = _dot(h.astype(BF16), win_ref[...])
    cv = proj[:, 0:D_CONV]
    cg = proj[:, D_CONV:2 * D_CONV]
    rw = proj[:, 2 * D_CONV:]
    u0 = cv * _sigmoid(cg)

    if has_halo:
        pv = jnp.where(i > 0, 1.0, 0.0).astype(F32)
        nv = jnp.where(i < n_tiles - 1, 1.0, 0.0).astype(F32)
        ubuf[0:CONV_HALO, :] = u0[0:CONV_HALO] * pv
        ubuf[CONV_HALO:CONV_HALO + tile, :] = u0[CONV_HALO:CONV_HALO + tile]
        ubuf[CONV_HALO + tile:, :] = u0[CONV_HALO + tile:] * nv
        rwbuf[0:8, :] = rw[CONV_HALO - 8:CONV_HALO] * pv
        rwbuf[8:8 + tile, :] = rw[CONV_HALO:CONV_HALO + tile]
        rwbuf[8 + tile:, :] = rw[CONV_HALO + tile:CONV_HALO + tile + 8] * nv
    else:
        ubuf[0:CONV_HALO, :] = jnp.zeros((CONV_HALO, D_CONV), F32)
        ubuf[CONV_HALO:CONV_HALO + tile, :] = u0
        ubuf[CONV_HALO + tile:, :] = jnp.zeros((CONV_HALO, D_CONV), F32)
        rwbuf[0:8, :] = jnp.zeros((8, RW_COLS), F32)
        rwbuf[8:8 + tile, :] = rw
        rwbuf[8 + tile:, :] = jnp.zeros((8, RW_COLS), F32)

    ext = tile + 8
    for s in range(8):
        ps = None
        for m in range(4):
            j = 8 * m + s - 1
            if 0 <= j < CONV_W:
                term = ubuf[pl.ds(8 * m, ext), :] * cdw_ref[j:j + 1, :]
                ps = term if ps is None else ps + term
        pbuf[s] = ps
    acc = pbuf[0, 0:tile, :] + cdb_ref[...]
    for s in range(1, 8):
        acc = acc + pbuf[s, pl.ds(s, tile), :]
    mu = jnp.mean(acc, axis=-1, keepdims=True)
    dv = acc - mu
    var = jnp.mean(dv * dv, axis=-1, keepdims=True)
    ln = dv * lax.rsqrt(var + EPS_LN) * lng_ref[...] + lnb_ref[...]
    u_ref[0] = (ln * _sigmoid(ln)).astype(BF16)

    rws = rwbuf[pl.ds(7, tile), :] * sk_ref[0:1, :]
    rws = rws + rwbuf[pl.ds(8, tile), :] * sk_ref[1:2, :]
    rws = rws + rwbuf[pl.ds(9, tile), :] * sk_ref[2:3, :]
    rws_ref[0] = rws


def _const_spec(shape):
    nd = len(shape)
    return pl.BlockSpec(shape, lambda b, i: (0,) * nd)


def _pre(x, mods, cond_of_b, p, tile, has_halo):
    bsz, seq, _ = x.shape
    n_tiles = seq // tile
    hb = tile // CONV_HALO
    n_hblk = seq // CONV_HALO
    in_specs = [pl.BlockSpec((1, tile, D_MODEL), lambda b, i: (b, i, 0))]
    args = [x]
    if has_halo:
        in_specs += [
            pl.BlockSpec((1, CONV_HALO, D_MODEL), lambda b, i: (b, jnp.maximum(i * hb - 1, 0), 0)),
            pl.BlockSpec((1, CONV_HALO, D_MODEL),
                         lambda b, i: (b, jnp.minimum((i + 1) * hb, n_hblk - 1), 0)),
        ]
        args += [x, x]
    in_specs += [
        pl.BlockSpec((1, 6, D_MODEL), lambda b, i: (cond_of_b(b), 0, 0)),
        _const_spec((1, D_MODEL)),
        _const_spec((D_MODEL, P_IN)),
        _const_spec((3, RW_COLS)),
        _const_spec((32, D_CONV)),
        _const_spec((1, D_CONV)),
        _const_spec((1, D_CONV)),
        _const_spec((1, D_CONV)),
    ]
    args += [mods, p['norm1_g'], p['w_in'], p['shift_k'], p['conv_dw'], p['conv_dw_b'],
             p['conv_ln_g'], p['conv_ln_b']]
    return pl.pallas_call(
        functools.partial(_pre_kernel, tile=tile, n_tiles=n_tiles, has_halo=has_halo),
        grid=(bsz, n_tiles),
        in_specs=in_specs,
        out_specs=[
            pl.BlockSpec((1, tile, D_CONV), lambda b, i: (b, i, 0)),
            pl.BlockSpec((1, tile, RW_COLS), lambda b, i: (b, i, 0)),
        ],
        out_shape=[
            jax.ShapeDtypeStruct((bsz, seq, D_CONV), BF16),
            jax.ShapeDtypeStruct((bsz, seq, RW_COLS), F32),
        ],
        scratch_shapes=[
            pltpu.VMEM((tile + 2 * CONV_HALO, D_CONV), F32),
            pltpu.VMEM((tile + 16, RW_COLS), F32),
            pltpu.VMEM((8, tile + 8, D_CONV), F32),
        ],
        compiler_params=pltpu.CompilerParams(
            dimension_semantics=("parallel", "parallel"), vmem_limit_bytes=VMEM_LIMIT),
        name="pre_halo" if has_halo else "pre",
    )(*args)


def _wkv_kernel(*refs, tile, n_tiles, has_s0, has_sout, n_casts):
    merged = n_tiles == 1
    refs = list(refs)
    if merged:
        rf_ref = rb_ref = refs[0]
        pos = 1
    else:
        rf_ref, rb_ref = refs[0], refs[1]
        pos = 2
    s0_ref = None
    if has_s0:
        s0_ref = refs[pos]
        pos += 1
    w0_ref, w2_ref, a0_ref, a2_ref, kk_ref, ka_ref = refs[pos:pos + 6]
    pos += 6
    cast_in = refs[pos:pos + n_casts]
    pos += n_casts
    y_ref = refs[pos]
    pos += 1
    sout_ref = None
    if has_sout:
        sout_ref = refs[pos]
        pos += 1
    cast_out = refs[pos:pos + n_casts]
    pos += n_casts
    (mst,) = refs[pos:]

    for src, dst in zip(cast_in, cast_out):
        dst[...] = src[...].astype(BF16)

    i = pl.program_id(1)
    n_chunks = tile // CHUNK

    lane_head = lax.broadcasted_iota(jnp.int32, (1, GROUP), 1) // HEAD
    head_masks = [jnp.where(lane_head == hh, 1.0, 0.0).astype(BF16)
                  for hh in range(HEADS_PER_GROUP)]
    t_idx = lax.broadcasted_iota(jnp.int32, (CHUNK, GROUP), 0)
    s_idx = lax.broadcasted_iota(jnp.int32, (CHUNK, GROUP), 1) % CHUNK
    eye_a = jnp.where(s_idx == t_idx, 1.0, 0.0).astype(F32)

    def bd(x):
        xb = x.astype(BF16)
        return jnp.concatenate([xb * m for m in head_masks], axis=0)

    def block_transpose(x):
        xt = x.T
        return jnp.concatenate([xt[hh * HEAD:(hh + 1) * HEAD, :] for hh in range(HEADS_PER_GROUP)],
                               axis=1)

    @pl.when(i == 0)
    def _init():
        if has_s0:
            for d in range(2):
                for g in range(N_GROUPS):
                    heads = [s0_ref[0, d, g * HEADS_PER_GROUP + hh] for hh in range(HEADS_PER_GROUP)]
                    mst[d, g] = jnp.concatenate(heads, axis=0).T
        else:
            mst[...] = jnp.zeros(mst.shape, F32)
        y_ref[...] = jnp.zeros(y_ref.shape, F32)

    ones_bd = _head_ones()
    slab = WKV_UNROLL * CHUNK
    row = lax.broadcasted_iota(jnp.int32, (slab, slab), 0)
    col = lax.broadcasted_iota(jnp.int32, (slab, slab), 1)
    same_chunk = (row // CHUNK) == (col // CHUNK)
    tri = [jnp.where(same_chunk & (col <= row), 1.0, 0.0).astype(BF16),
           jnp.where(same_chunk & (col >= row), 1.0, 0.0).astype(BF16)]
    strict = [jnp.where(s_idx < t_idx, 1.0, 0.0).astype(F32),
              jnp.where(s_idx > t_idx, 1.0, 0.0).astype(F32)]
    incl = [jnp.where(s_idx <= t_idx, 1.0, 0.0).astype(F32),
            jnp.where(s_idx >= t_idx, 1.0, 0.0).astype(F32)]

    chains = [(d, g, u) for u in range(WKV_UNROLL) for d in range(2) for g in range(N_GROUPS)]

    def loop_step(it, carry):
        slab0 = [pl.multiple_of(it * slab, slab), pl.multiple_of(tile - (it + 1) * slab, slab)]

        def chunk_lo(d, u):
            return (u if d == 0 else WKV_UNROLL - 1 - u) * CHUNK

        kt, q, kh, bh, khp, bhp, v, epc = ({} for _ in range(8))
        for d in range(2):
            src = rf_ref if d == 0 else rb_ref
            srows = pl.ds(slab0[d], slab)
            r_t = src[0, srows, 0:D_RWKV]
            k_t = src[0, srows, D_RWKV:2 * D_RWKV]
            v_t = src[0, srows, 2 * D_RWKV:3 * D_RWKV]
            wlo = src[0, srows, 3 * D_RWKV:3 * D_RWKV + 2 * R_LO]
            alo = src[0, srows, 3 * D_RWKV + 2 * R_LO:3 * D_RWKV + 4 * R_LO]
            kk = k_t * kk_ref[...]
            ss = _head_sum(kk * kk, ones_bd)
            kk = kk * lax.rsqrt(jnp.maximum(ss, L2_EPS))
            w_raw = w0_ref[d:d + 1, :] + _dot(jnp.tanh(wlo).astype(BF16), w2_ref[d])
            logw = -_sigmoid(w_raw) * EXP_M05
            a = _sigmoid(a0_ref[d:d + 1, :] + _dot(alo.astype(BF16), a2_ref[d]))
            kdir = k_t * (1.0 + (a - 1.0) * ka_ref[...])
            bvec = kk * a
            lw_hi, lw_lo = _split2(logw)
            cum = _dot(tri[d], lw_hi) + _dot(tri[d], lw_lo)
            edge = CHUNK - 1 if d == 0 else 0
            tot = jnp.concatenate(
                [jnp.broadcast_to(cum[c * CHUNK + edge:c * CHUNK + edge + 1, :], (CHUNK, D_RWKV))
                 for c in range(WKV_UNROLL)], axis=0)
            e_in = jnp.exp(cum)
            e_neg = jnp.exp(-cum)
            e_tot = jnp.exp(tot)
            e_rem = e_tot * e_neg
            full = {'kt': kk * jnp.exp(cum - logw), 'q': r_t * e_in, 'kh': kdir * e_neg,
                    'bh': bvec * e_neg, 'khp': kdir * e_rem, 'bhp': bvec * e_rem, 'v': v_t,
                    'epc': e_tot}
            for g in range(N_GROUPS):
                lanes = slice(g * GROUP, (g + 1) * GROUP)
                for u in range(WKV_UNROLL):
                    lo = chunk_lo(d, u)
                    ch = (d, g, u)
                    for dst, name in ((kt, 'kt'), (q, 'q'), (kh, 'kh'), (bh, 'bh'), (khp, 'khp'),
                                      (bhp, 'bhp'), (v, 'v')):
                        dst[ch] = full[name][lo:lo + CHUNK, lanes]
                    epc[ch] = full['epc'][lo:lo + 1, lanes]

        def stack(*xs):
            return jnp.concatenate([x.astype(BF16) for x in xs], axis=0)

        lhs2 = {ch: stack(kt[ch], q[ch]) for ch in chains}
        a1 = {ch: _dot(lhs2[ch], bd(kh[ch]), NT) for ch in chains}
        a2 = {ch: _dot(lhs2[ch], bd(bh[ch]), NT) for ch in chains}
        a_ak = {ch: a1[ch][0:CHUNK] * strict[ch[0]] for ch in chains}
        a_qk = {ch: a1[ch][CHUNK:] * incl[ch[0]] for ch in chains}
        a_ab = {ch: a2[ch][0:CHUNK] * strict[ch[0]] for ch in chains}
        a_qb = {ch: a2[ch][CHUNK:] * incl[ch[0]] for ch in chains}

        khp_t = {ch: block_transpose(khp[ch]) for ch in chains}
        bhp_t = {ch: block_transpose(bhp[ch]) for ch in chains}

        t_inv = {ch: eye_a - a_ab[ch] for ch in chains}
        n_pow = {ch: _bdot(a_ab[ch], bd(a_ab[ch])) for ch in chains}
        for _ in range(4):
            both = {ch: _dot(stack(t_inv[ch], n_pow[ch]), bd(n_pow[ch])) for ch in chains}
            t_inv = {ch: t_inv[ch] + both[ch][0:CHUNK] for ch in chains}
            n_pow = {ch: both[ch][CHUNK:] for ch in chains}
        t_inv = {ch: t_inv[ch] + _bdot(t_inv[ch], bd(n_pow[ch])) for ch in chains}

        on_v = {ch: _dot(stack(a_ak[ch], a_qk[ch], khp_t[ch]), bd(v[ch])) for ch in chains}
        av = {ch: on_v[ch][0:CHUNK] for ch in chains}
        lhs_qb = {ch: _dot(stack(a_qb[ch], bhp_t[ch]), bd(t_inv[ch])).astype(BF16) for ch in chains}
        on_ktp = {ch: _dot(lhs_qb[ch], bd(kt[ch])) for ch in chains}
        on_vp = {ch: _dot(lhs_qb[ch], bd(av[ch])) for ch in chains}
        qp = {ch: q[ch] - on_ktp[ch][0:CHUNK] for ch in chains}
        yp = {ch: on_v[ch][CHUNK:2 * CHUNK] - on_vp[ch][0:CHUNK] for ch in chains}
        gp = {ch: eye_a * epc[ch] - on_ktp[ch][CHUNK:] for ch in chains}
        hp = {ch: on_v[ch][2 * CHUNK:] - on_vp[ch][CHUNK:] for ch in chains}

        tile0 = [i * tile, (n_tiles - 1 - i) * tile]
        for d in range(2):
            for g in range(N_GROUPS):
                m = mst[d, g]
                for u in range(WKV_UNROLL):
                    ch = (d, g, u)
                    on_m = _dot(stack(qp[ch], gp[ch]), bd(m))
                    out_rows = pl.ds(pl.multiple_of(tile0[d] + slab0[d] + chunk_lo(d, u), CHUNK), CHUNK)
                    y_ref[0, out_rows, g * GROUP:(g + 1) * GROUP] += on_m[0:CHUNK] + yp[ch]
                    m = on_m[CHUNK:] + hp[ch]
                mst[d, g] = m
        return carry

    lax.fori_loop(0, n_chunks // WKV_UNROLL, loop_step, 0)

    if has_sout:
        @pl.when(i == n_tiles - 1)
        def _fin():
            for d in range(2):
                for g in range(N_GROUPS):
                    st = mst[d, g].T
                    for hh in range(HEADS_PER_GROUP):
                        sout_ref[0, d, g * HEADS_PER_GROUP + hh] = st[hh * HEAD:(hh + 1) * HEAD, :]


def _wkv(rws, s0_bd, p, tile, has_sout, side_casts=()):
    bsz, seq, _ = rws.shape
    n_tiles = seq // tile
    has_s0 = s0_bd is not None
    merged = n_tiles == 1
    in_specs = [pl.BlockSpec((1, tile, RW_COLS), lambda b, i: (b, i, 0))]
    args = [rws]
    if not merged:
        in_specs.append(pl.BlockSpec((1, tile, RW_COLS), lambda b, i: (b, n_tiles - 1 - i, 0)))
        args.append(rws)
    if has_s0:
        in_specs.append(pl.BlockSpec((1, 2, N_RHEADS, HEAD, HEAD), lambda b, i: (b, 0, 0, 0, 0)))
        args.append(s0_bd)
    in_specs += [
        _const_spec((2, D_RWKV)),
        _const_spec((2, 2 * R_LO, D_RWKV)),
        _const_spec((2, D_RWKV)),
        _const_spec((2, 2 * R_LO, D_RWKV)),
        _const_spec((1, D_RWKV)),
        _const_spec((1, D_RWKV)),
    ]
    args += [p['w0'], p['w2p'], p['a0'], p['a2p'], p['k_k'], p['k_a']]
    out_specs = [pl.BlockSpec((1, seq, D_RWKV), lambda b, i: (b, 0, 0))]
    out_shape = [jax.ShapeDtypeStruct((bsz, seq, D_RWKV), F32)]
    if has_sout:
        out_specs.append(pl.BlockSpec((1, 2, N_RHEADS, HEAD, HEAD), lambda b, i: (b, 0, 0, 0, 0)))
        out_shape.append(jax.ShapeDtypeStruct((bsz, 2, N_RHEADS, HEAD, HEAD), F32))
    n_steps = bsz * n_tiles
    for w in side_casts:
        slab_rows = w.shape[0] // n_steps
        assert slab_rows * n_steps == w.shape[0] and slab_rows % 16 == 0
        spec = pl.BlockSpec((slab_rows, w.shape[1]), lambda b, i: (b * n_tiles + i, 0))
        in_specs.append(spec)
        args.append(w)
        out_specs.append(spec)
        out_shape.append(jax.ShapeDtypeStruct(w.shape, BF16))
    return pl.pallas_call(
        functools.partial(_wkv_kernel, tile=tile, n_tiles=n_tiles, has_s0=has_s0,
                          has_sout=has_sout, n_casts=len(side_casts)),
        grid=(bsz, n_tiles),
        in_specs=in_specs,
        out_specs=out_specs,
        out_shape=out_shape,
        scratch_shapes=[pltpu.VMEM((2, N_GROUPS, HEAD, GROUP), F32)],
        compiler_params=pltpu.CompilerParams(
            dimension_semantics=("arbitrary", "arbitrary"), vmem_limit_bytes=VMEM_LIMIT),
        name="wkv_s0" if has_s0 else "wkv",
    )(*args)


def _mix_kernel(x_ref, y_ref, rkv_ref, glo_ref, u_ref, mods_ref, rk_ref, lxg_ref, lxb_ref,
                g2_ref, wout_ref, x1_ref):
    ones_bd = _head_ones()
    g1 = mods_ref[0, 2:3, :]
    o = y_ref[0]
    mu = _head_sum(o, ones_bd) * (1.0 / HEAD)
    dv = o - mu
    var = _head_sum(dv * dv, ones_bd) * (1.0 / HEAD)
    on = dv * lax.rsqrt(var + EPS_GN) * lxg_ref[...] + lxb_ref[...]
    r = rkv_ref[0, :, 0:D_RWKV]
    k = rkv_ref[0, :, D_RWKV:2 * D_RWKV]
    v = rkv_ref[0, :, 2 * D_RWKV:3 * D_RWKV]
    glo = glo_ref[0]
    bonus = _head_sum(r * k * rk_ref[...], ones_bd) * v
    gate = _dot(_sigmoid(glo).astype(BF16), g2_ref[...])
    o2 = (on + bonus) * gate
    mixed = _dot(u_ref[0].astype(BF16), wout_ref[0:D_CONV, :])
    mixed = mixed + _dot(o2.astype(BF16), wout_ref[D_CONV:, :])
    x1_ref[0] = x_ref[0] + g1 * mixed


def _mix(x, y, rws, u, mods, cond_of_b, p, tile):
    bsz, seq, _ = x.shape
    n_tiles = seq // tile

    def tok(width, col_block=0):
        return pl.BlockSpec((1, tile, width), lambda b, i: (b, i, col_block))

    glo_block = (RW_COLS - R_G) // R_G
    return pl.pallas_call(
        _mix_kernel,
        grid=(bsz, n_tiles),
        in_specs=[
            tok(D_MODEL), tok(D_RWKV), tok(3 * D_RWKV), tok(R_G, glo_block), tok(D_CONV),
            pl.BlockSpec((1, 6, D_MODEL), lambda b, i: (cond_of_b(b), 0, 0)),
            _const_spec((1, D_RWKV)),
            _const_spec((1, D_RWKV)),
            _const_spec((1, D_RWKV)),
            _const_spec((R_G, D_RWKV)),
            _const_spec((D_MODEL, D_MODEL)),
        ],
        out_specs=tok(D_MODEL),
        out_shape=jax.ShapeDtypeStruct((bsz, seq, D_MODEL), F32),
        compiler_params=pltpu.CompilerParams(
            dimension_semantics=("parallel", "parallel"), vmem_limit_bytes=VMEM_LIMIT),
        name="mix",
    )(x, y, rws, rws, u, mods, p['r_k'], p['ln_x_g'], p['ln_x_b'], p['g2'], p['w_out'])


FFN_COLS = 256
FFN_PAD = 8


def _ffn_kernel(*refs, tile, n_tiles, grid_conv, period):
    if grid_conv:
        (x_ref, xp_ref, xn_ref, mods_ref, n2g_ref, up_ref, dw_ref, dwb_ref, down_ref, fg_ref,
         o_ref, vbuf, gbuf, vl, vr, gl, gr, act_s) = refs
    else:
        (x_ref, mods_ref, n2g_ref, up_ref, dw_ref, dwb_ref, down_ref, fg_ref,
         o_ref, vbuf, gbuf, vl, vr, gl, gr, act_s) = refs
    i = pl.program_id(1)
    sh2 = mods_ref[0, 3:4, :]
    sc2 = mods_ref[0, 4:5, :]
    g2 = mods_ref[0, 5:6, :]
    halo = FFN_HALO if grid_conv else 0
    if grid_conv:
        xe = jnp.concatenate([xp_ref[0], x_ref[0], xn_ref[0]], axis=0)
    else:
        xe = x_ref[0]
    rows_e = tile + 2 * halo
    h2 = _rmsnorm(xe, n2g_ref[...]) * (1.0 + sc2) + sh2

    if grid_conv:
        e_idx = lax.broadcasted_iota(jnp.int32, (rows_e, 1), 0)
        pv = jnp.where(i > 0, 1.0, 0.0).astype(F32)
        nv = jnp.where(i < n_tiles - 1, 1.0, 0.0).astype(F32)
        h2 = h2 * (jnp.where(e_idx >= halo, 1.0, pv) * jnp.where(e_idx < halo + tile, 1.0, nv))
        taps = [(dr, dc) for dr in (-1, 0, 1) for dc in (-1, 0, 1)]
    else:
        taps = [(0, -1), (0, 0), (0, 1)]
    masked = period < tile
    if masked:
        colpos = lax.broadcasted_iota(jnp.int32, (tile, 1), 0) % period
        m_left = jnp.where(colpos >= 1, 1.0, 0.0)
        m_right = jnp.where(colpos <= period - 2, 1.0, 0.0)

    zpad = jnp.zeros((FFN_PAD, FFN_COLS), F32)
    vbuf[0:FFN_PAD, :] = zpad
    vbuf[FFN_PAD + rows_e:, :] = zpad
    gbuf[0:FFN_PAD, :] = zpad
    gbuf[FFN_PAD + rows_e:, :] = zpad

    def conv(buf, side_l, side_r, col0):
        ext = tile + 2 * FFN_PAD
        sums = {}
        for ti, (dr, dc) in enumerate(taps):
            win = buf[pl.ds(halo + GRID_W * dr, ext), :]
            term = win * dw_ref[ti:ti + 1, col0:col0 + FFN_COLS]
            sums[dc] = term if dc not in sums else sums[dc] + term
        side_l[...] = sums[-1]
        side_r[...] = sums[1]
        left = side_l[pl.ds(FFN_PAD - 1, tile), :]
        right = side_r[pl.ds(FFN_PAD + 1, tile), :]
        if masked:
            left = left * m_left
            right = right * m_right
        centre = sums[0][FFN_PAD:FFN_PAD + tile]
        return centre + left + right + dwb_ref[:, col0:col0 + FFN_COLS]

    hb = h2.astype(BF16)
    for cc in range(D_FF // FFN_COLS):
        c0 = cc * FFN_COLS
        vbuf[FFN_PAD:FFN_PAD + rows_e, :] = _dot(hb, up_ref[:, c0:c0 + FFN_COLS])
        gbuf[FFN_PAD:FFN_PAD + rows_e, :] = _dot(hb, up_ref[:, D_FF + c0:D_FF + c0 + FFN_COLS])
        val = conv(vbuf, vl, vr, c0)
        gate = conv(gbuf, gl, gr, D_FF + c0)
        act_s[:, c0:c0 + FFN_COLS] = (val * (gate * _sigmoid(gate))).astype(BF16)

    x2 = x_ref[0] + g2 * _dot(act_s[...], down_ref[...])
    o_ref[0] = _rmsnorm(x2, fg_ref[...])


def _ffn(x1, mods, cond_of_b, p, tile, grid_conv, period):
    bsz, seq, _ = x1.shape
    n_tiles = seq // tile
    in_specs = [pl.BlockSpec((1, tile, D_MODEL), lambda b, i: (b, i, 0))]
    args = [x1]
    if grid_conv:
        hb = tile // FFN_HALO
        n_hblk = seq // FFN_HALO
        in_specs += [
            pl.BlockSpec((1, FFN_HALO, D_MODEL), lambda b, i: (b, jnp.maximum(i * hb - 1, 0), 0)),
            pl.BlockSpec((1, FFN_HALO, D_MODEL),
                         lambda b, i: (b, jnp.minimum((i + 1) * hb, n_hblk - 1), 0)),
        ]
        args += [x1, x1]
        dw = p['ffn_dw9']
    else:
        dw = p['ffn_dw3']
    n_taps = dw.shape[0]
    in_specs += [
        pl.BlockSpec((1, 6, D_MODEL), lambda b, i: (cond_of_b(b), 0, 0)),
        _const_spec((1, D_MODEL)),
        _const_spec((D_MODEL, 2 * D_FF)),
        _const_spec((n_taps, 2 * D_FF)),
        _const_spec((1, 2 * D_FF)),
        _const_spec((D_FF, D_MODEL)),
        _const_spec((1, D_MODEL)),
    ]
    args += [mods, p['norm2_g'], p['ffn_up'], dw, p['ffn_dw_b'], p['ffn_down'], p['final_g']]
    halo = FFN_HALO if grid_conv else 0
    buf_rows = tile + 2 * halo + 2 * FFN_PAD
    return pl.pallas_call(
        functools.partial(_ffn_kernel, tile=tile, n_tiles=n_tiles, grid_conv=grid_conv,
                          period=period),
        grid=(bsz, n_tiles),
        in_specs=in_specs,
        out_specs=pl.BlockSpec((1, tile, D_MODEL), lambda b, i: (b, i, 0)),
        out_shape=jax.ShapeDtypeStruct((bsz, seq, D_MODEL), F32),
        scratch_shapes=[pltpu.VMEM((buf_rows, FFN_COLS), F32),
                        pltpu.VMEM((buf_rows, FFN_COLS), F32)]
        + [pltpu.VMEM((tile + 2 * FFN_PAD, FFN_COLS), F32) for _ in range(4)]
        + [pltpu.VMEM((tile, D_FF), BF16)],
        compiler_params=pltpu.CompilerParams(
            dimension_semantics=("parallel", "parallel"), vmem_limit_bytes=VMEM_LIMIT),
        name="ffn_grid" if grid_conv else "ffn",
    )(*args)


def _pad_lowrank(w):
    z = jnp.zeros_like(w[0])
    return jnp.stack([jnp.concatenate([w[0], z], axis=0),
                      jnp.concatenate([z, w[1]], axis=0)]).astype(BF16)


def kernel(x_prompt, x_sample, state_wkv, c, c_ctx, ada_w, ada_b, norm1_g, w_in, shift_k, conv_dw, conv_dw_b, conv_ln_g, conv_ln_b, w0, w2, a0, a2, g2, k_k, k_a, r_k, ln_x_g, ln_x_b, w_out, norm2_g, ffn_up, ffn_dw, ffn_dw_b, ffn_down, final_g):
    assert ada_w.shape[0] == 1, "single layer"
    dec_b = x_sample.shape[0]
    cond8 = jnp.zeros((8, D_MODEL), F32).at[0].set(c_ctx).at[1:1 + dec_b].set(c)
    mods, w_in_b = _adaln(cond8, ada_w[0], ada_b, w_in[0])
    mods = mods.reshape(8, 6, D_MODEL)

    p = {
        'norm1_g': norm1_g, 'w_in': w_in_b, 'shift_k': shift_k[0],
        'conv_dw': jnp.concatenate([conv_dw[0], jnp.zeros((1, D_CONV), F32)], axis=0),
        'conv_dw_b': conv_dw_b, 'conv_ln_g': conv_ln_g, 'conv_ln_b': conv_ln_b,
        'w0': w0[0], 'w2p': _pad_lowrank(w2[0]), 'a0': a0[0], 'a2p': _pad_lowrank(a2[0]),
        'k_k': k_k, 'k_a': k_a, 'r_k': r_k.reshape(1, D_RWKV),
        'ln_x_g': ln_x_g, 'ln_x_b': ln_x_b, 'g2': g2[0].astype(BF16),
        'norm2_g': norm2_g, 'ffn_dw9': ffn_dw[0].reshape(9, 2 * D_FF),
        'ffn_dw3': ffn_dw[0, 1], 'ffn_dw_b': ffn_dw_b,
        'final_g': final_g.reshape(1, D_MODEL),
    }

    def ctx_cond(b):
        return 0

    def lat_cond(b):
        return b + 1

    seq = x_prompt.shape[1]
    u, rws = _pre(x_prompt, mods, ctx_cond, p, tile=seq, has_halo=False)
    y, sfin, w_out_b, ffn_up_b, ffn_down_b = _wkv(
        rws, None, p, tile=seq, has_sout=True, side_casts=(w_out[0], ffn_up[0], ffn_down[0]))
    p = dict(p, w_out=w_out_b, ffn_up=ffn_up_b, ffn_down=ffn_down_b)
    bsz = x_prompt.shape[0]
    n_join = MIX_TILE // seq

    def pair(t):
        return t.reshape(bsz // n_join, MIX_TILE, t.shape[-1])

    x1 = _mix(pair(x_prompt), pair(y), pair(rws), pair(u), mods, ctx_cond, p, tile=MIX_TILE)
    y_prompt = _ffn(x1.reshape(x_prompt.shape), mods, ctx_cond, p, tile=seq, grid_conv=False,
                    period=seq)
    new_state = sfin[:, None].astype(state_wkv.dtype)

    s0_bd = state_wkv[:, 0].astype(F32)
    u, rws = _pre(x_sample, mods, lat_cond, p, tile=512, has_halo=True)
    (y,) = _wkv(rws, s0_bd, p, tile=512, has_sout=False)
    x1 = _mix(x_sample, y, rws, u, mods, lat_cond, p, tile=MIX_TILE)
    y_sample = _ffn(x1, mods, lat_cond, p, tile=512, grid_conv=True, period=GRID_W)

    return (y_prompt, y_sample, new_state)
```
